```python
import math
import jax
import jax.numpy as jnp
from jax import lax
import numpy as np

D_MODEL = 2048
BATCH = 16
SEQ = 2048
DEPTH = 2

NSA_HEADS = 8
NSA_KV_GROUPS = 2
NSA_HPG = NSA_HEADS // NSA_KV_GROUPS
NSA_HEAD_DIM = 128
NSA_WIDTH = NSA_HEADS * NSA_HEAD_DIM
NSA_KV_WIDTH = NSA_KV_GROUPS * NSA_HEAD_DIM
CMP_BLOCK = 32
CMP_STRIDE = 16
CMP_HIDDEN = 256
SEL_BLOCK = 64
SEL_TOP_N = 16
SEL_FORCE = 1000.0
WINDOW = 512
NSA_QBLOCK = 32

GDN_HEADS = 8
GDN_HEAD_DIM = 128
GDN_WIDTH = GDN_HEADS * GDN_HEAD_DIM
CONV_WIDTH = 4
GDN_CHUNK = 64

REL_BUCKETS = 32
REL_MAX_DIST = 128

D_FF = 4 * D_MODEL
NORM_EPS = 1e-6

IN_SPLIT_SIZES = (NSA_WIDTH, 6 * NSA_KV_WIDTH, 3 * NSA_HEADS, 3 * GDN_WIDTH, GDN_WIDTH, GDN_HEADS, GDN_HEADS, 2 * D_MODEL)
IN_COLS = NSA_WIDTH + 6 * NSA_KV_WIDTH + 3 * NSA_HEADS + 3 * GDN_WIDTH + GDN_WIDTH + GDN_HEADS + GDN_HEADS + 2 * D_MODEL

kernel_name = 'hybrid_nsa_gdn_block'


def rms_norm(x, w):
    xf = x.astype(jnp.float32)
    y = xf * lax.rsqrt(jnp.mean(xf * xf, axis=-1, keepdims=True) + NORM_EPS)
    return (y * w.astype(jnp.float32)).astype(x.dtype)


def masked_softmax(s, valid):
    s = jnp.where(valid, s.astype(jnp.float32), -jnp.inf)
    m = jnp.max(s, axis=-1, keepdims=True)
    m = jnp.where(jnp.isfinite(m), m, 0.0)
    e = jnp.exp(s - m)
    return e / jnp.maximum(jnp.sum(e, axis=-1, keepdims=True), 1e-30)


def t5_bucket(dist):
    n = jnp.maximum(dist, 0)
    exact = REL_BUCKETS // 2
    nf = jnp.maximum(n, 1).astype(jnp.float32)
    large = exact + (jnp.log(nf / exact) / math.log(REL_MAX_DIST / exact) * (REL_BUCKETS - exact)).astype(jnp.int32)
    large = jnp.minimum(large, REL_BUCKETS - 1)
    return jnp.where(n < exact, n, large)


def rel_bias_grid(rel_table, dist):
    tq, tk = dist.shape
    b = rel_table[t5_bucket(dist)]
    return b.reshape(tq, tk, NSA_KV_GROUPS, NSA_HPG).transpose(2, 3, 0, 1)


def compress_tokens(kv, pe, w1, w2):
    b, s, g, dh = kv.shape
    n_sub = CMP_BLOCK // CMP_STRIDE
    n_cmp = s // CMP_STRIDE - n_sub + 1
    r = kv.reshape(b, s // CMP_STRIDE, CMP_STRIDE, g, dh)
    blocks = jnp.concatenate([r[:, j:j + n_cmp] for j in range(n_sub)], axis=2)
    blocks = blocks + pe[None, None, :, None, :]
    flat = blocks.transpose(0, 1, 3, 2, 4).reshape(b, n_cmp, g, CMP_BLOCK * dh)
    return jax.nn.gelu(flat @ w1) @ w2


def nsa_attention(q, kv, gate_logits, rel_table, pe_k, pe_v, w1_k, w2_k, w1_v, w2_v):
    b, s, _ = q.shape
    g_, j_, dh = NSA_KV_GROUPS, NSA_HPG, NSA_HEAD_DIM
    scale = dh ** -0.5
    q = q.reshape(b, s, g_, j_, dh)
    k_c, v_c, k_s, v_s, k_w, v_w = [t.reshape(b, s, g_, dh) for t in jnp.split(kv, 6, axis=-1)]
    kc = compress_tokens(k_c, pe_k, w1_k, w2_k)
    vc = compress_tokens(v_c, pe_v, w1_v, w2_v)
    n_cmp = kc.shape[1]
    n_sel = s // SEL_BLOCK
    top_n = min(SEL_TOP_N, n_sel)
    ks = k_s.reshape(b, n_sel, SEL_BLOCK, g_, dh).transpose(0, 3, 1, 2, 4)
    vs = v_s.reshape(b, n_sel, SEL_BLOCK, g_, dh).transpose(0, 3, 1, 2, 4)
    pad = ((0, 0), (WINDOW, 0), (0, 0), (0, 0))
    kw = jnp.pad(k_w, pad)
    vw = jnp.pad(v_w, pad)
    gates = jax.nn.sigmoid(gate_logits).reshape(b, s, g_, j_, 3)

    cmp_end = jnp.arange(n_cmp) * CMP_STRIDE + (CMP_BLOCK - 1)
    c_start = np.arange(n_cmp) * CMP_STRIDE
    s_start = np.arange(n_sel) * SEL_BLOCK
    cover = jnp.asarray(((c_start[:, None] <= s_start[None, :] + SEL_BLOCK - 1)
                         & (c_start[:, None] + CMP_BLOCK - 1 >= s_start[None, :])).astype(np.float32))
    sel_ids = jnp.arange(n_sel)
    rel_flat = rel_table.reshape(REL_BUCKETS, g_, j_).transpose(1, 0, 2).reshape(g_ * REL_BUCKETS, j_)
    group_off = jnp.arange(g_)[None, :, None, None, None] * REL_BUCKETS
    gather_blocks = jax.vmap(jax.vmap(lambda blk, ix: blk[ix]))

    def query_block(i):
        q0 = i * NSA_QBLOCK
        t = q0 + jnp.arange(NSA_QBLOCK)
        qb = lax.dynamic_slice_in_dim(q, q0, NSA_QBLOCK, axis=1)
        gb = lax.dynamic_slice_in_dim(gates, q0, NSA_QBLOCK, axis=1)

        dist_c = t[:, None] - cmp_end[None, :]
        s_c = jnp.einsum('btgjd,bngd->bgjtn', qb, kc) * scale + rel_bias_grid(rel_table, dist_c)
        p_c = masked_softmax(s_c, dist_c >= 0)
        o_c = jnp.einsum('bgjtn,bngd->btgjd', p_c.astype(vc.dtype), vc)

        imp = jnp.einsum('bgjtn,nm->btgm', p_c, cover)
        cur = t // SEL_BLOCK
        forced = (sel_ids[None, :] == 0) | (sel_ids[None, :] == cur[:, None]) | (sel_ids[None, :] == cur[:, None] - 1)
        causal_blk = sel_ids[None, :] <= cur[:, None]
        score = jnp.where(causal_blk[:, None, :], imp + jnp.where(forced, SEL_FORCE, 0.0)[:, None, :], -jnp.inf)
        top_val, top_idx = lax.top_k(score, top_n)
        top_idx = top_idx.transpose(0, 2, 1, 3)
        k_sel = gather_blocks(ks, top_idx)
        v_sel = gather_blocks(vs, top_idx)
        key_pos = top_idx[..., None] * SEL_BLOCK + jnp.arange(SEL_BLOCK)
        dist_s = t[None, None, :, None, None] - key_pos
        valid_s = jnp.isfinite(top_val).transpose(0, 2, 1, 3)[..., None] & (dist_s >= 0)
        bias_s = rel_flat[group_off + t5_bucket(dist_s)]
        s_s = jnp.einsum('btgjd,bgtnrd->bgjtnr', qb, k_sel) * scale + jnp.moveaxis(bias_s, -1, 2)
        n_keys = top_n * SEL_BLOCK
        p_s = masked_softmax(s_s.reshape(b, g_, j_, NSA_QBLOCK, n_keys),
                             valid_s.reshape(b, g_, 1, NSA_QBLOCK, n_keys))
        o_s = jnp.einsum('bgjtnr,bgtnrd->btgjd', p_s.reshape(s_s.shape).astype(v_sel.dtype), v_sel)

        kwb = lax.dynamic_slice_in_dim(kw, q0, WINDOW + NSA_QBLOCK, axis=1)
        vwb = lax.dynamic_slice_in_dim(vw, q0, WINDOW + NSA_QBLOCK, axis=1)
        kpos = q0 - WINDOW + jnp.arange(WINDOW + NSA_QBLOCK)
        dist_w = t[:, None] - kpos[None, :]
        valid_w = (dist_w >= 0) & (dist_w < WINDOW) & (kpos[None, :] >= 0)
        s_w = jnp.einsum('btgjd,bsgd->bgjts', qb, kwb) * scale + rel_bias_grid(rel_table, dist_w)
        p_w = masked_softmax(s_w, valid_w)
        o_w = jnp.einsum('bgjts,bsgd->btgjd', p_w.astype(vwb.dtype), vwb)

        return gb[..., 0:1] * o_c + gb[..., 1:2] * o_s + gb[..., 2:3] * o_w

    out = lax.map(query_block, jnp.arange(s // NSA_QBLOCK))
    return out.transpose(1, 0, 2, 3, 4, 5).reshape(b, s, NSA_WIDTH)


def causal_depthwise_conv(x, w):
    return lax.conv_general_dilated(x, w[:, None, :], window_strides=(1,), padding=[(CONV_WIDTH - 1, 0)],
                                    dimension_numbers=('NWC', 'WIO', 'NWC'), feature_group_count=x.shape[-1])


def l2_normalize(x):
    return x * lax.rsqrt(jnp.sum(x * x, axis=-1, keepdims=True) + NORM_EPS)


def gated_delta_rule(q, k, v, g, beta):
    b, s, h, dk = q.shape
    dv = v.shape[-1]
    c = GDN_CHUNK
    n = s // c
    q = l2_normalize(q) * dk ** -0.5
    k = l2_normalize(k)

    def chunks(t):
        return jnp.moveaxis(t.reshape(b, n, c, h, *t.shape[3:]), 3, 1)

    q, k, v, g, beta = (chunks(t) for t in (q, k, v, g, beta))
    g = jnp.cumsum(g, axis=-1)
    kb = k * beta[..., None]
    vb = v * beta[..., None]
    tri = jnp.tril(jnp.ones((c, c), dtype=bool))
    strict = jnp.tril(jnp.ones((c, c), dtype=bool), -1)
    diff = g[..., :, None] - g[..., None, :]
    decay = jnp.where(tri, jnp.exp(jnp.where(tri, diff, 0.0)), 0.0)
    lower = jnp.where(strict, jnp.einsum('bhncd,bhnsd->bhncs', kb, k) * decay, 0.0)
    eye = jnp.eye(c, dtype=q.dtype)
    tinv = lax.linalg.triangular_solve(eye + lower, jnp.broadcast_to(eye, lower.shape),
                                       left_side=True, lower=True, unit_diagonal=True)
    u = tinv @ vb
    w = tinv @ (kb * jnp.exp(g)[..., None])
    qk = jnp.einsum('bhncd,bhnsd->bhncs', q, k) * decay

    def step(state, inp):
        qn, kn, un, wn, gn, qkn = inp
        v_new = un - wn @ state
        o = (qn * jnp.exp(gn)[..., None]) @ state + qkn @ v_new
        g_last = gn[..., -1:]
        state = state * jnp.exp(g_last)[..., None] + jnp.einsum('bhck,bhcv->bhkv', kn * jnp.exp(g_last - gn)[..., None], v_new)
        return state, o

    xs = tuple(jnp.moveaxis(t, 2, 0) for t in (q, k, u, w, g, qk))
    state0 = jnp.zeros((b, h, dk, dv), q.dtype)
    _, o = lax.scan(step, state0, xs)
    return o.transpose(1, 0, 3, 2, 4).reshape(b, s, h, dv)


def gated_deltanet(qkv, z, a, bl, conv_w, a_log, dt_bias, norm_w):
    b, s, _ = qkv.shape
    dtype = qkv.dtype
    qkv = jax.nn.silu(causal_depthwise_conv(qkv, conv_w)).astype(jnp.float32)
    q, k, v = jnp.split(qkv, 3, axis=-1)
    shape = (b, s, GDN_HEADS, GDN_HEAD_DIM)
    beta = jax.nn.sigmoid(bl.astype(jnp.float32))
    g = -jnp.exp(a_log.astype(jnp.float32)) * jax.nn.softplus(a.astype(jnp.float32) + dt_bias.astype(jnp.float32))
    o = gated_delta_rule(q.reshape(shape), k.reshape(shape), v.reshape(shape), g, beta)
    o = o * lax.rsqrt(jnp.mean(o * o, axis=-1, keepdims=True) + NORM_EPS) * norm_w.astype(jnp.float32)
    o = o * jax.nn.silu(z.astype(jnp.float32)).reshape(shape)
    return o.reshape(b, s, GDN_WIDTH).astype(dtype)


def hybrid_layer(x, rel_table, ln1_w, w_in, pe_k, pe_v, w1_k, w2_k, w1_v, w2_v, conv_w, a_log, dt_bias,
                 gdn_norm_w, w_pa, w_pb, w_o, ln2_w, w_up, w_down):
    h = rms_norm(x, ln1_w)
    proj = h @ w_in
    offs = [int(o) for o in np.cumsum(IN_SPLIT_SIZES)[:-1]]
    nsa_q, nsa_kv, nsa_gate, gdn_qkv, gdn_z, gdn_a, gdn_b, merge = jnp.split(proj, offs, axis=-1)
    o_a = nsa_attention(nsa_q, nsa_kv, nsa_gate, rel_table, pe_k, pe_v, w1_k, w2_k, w1_v, w2_v)
    o_b = gated_deltanet(gdn_qkv, gdn_z, gdn_a, gdn_b, conv_w, a_log, dt_bias, gdn_norm_w)
    g_a, g_b = jnp.split(jax.nn.sigmoid(merge), 2, axis=-1)
    x = x + (g_a * (o_a @ w_pa) + g_b * (o_b @ w_pb)) @ w_o
    h = rms_norm(x, ln2_w)
    return x + jnp.square(jax.nn.relu(h @ w_up)) @ w_down


def setup_inputs(seed: int = 0) -> dict:
    key = jax.random.key(seed)
    k = jax.random.split(key, 21)
    f32 = jnp.float32
    L = DEPTH

    def normal(kk, shape, scale):
        return jax.random.normal(kk, shape, f32) * scale

    cmp_in = CMP_BLOCK * NSA_HEAD_DIM
    dt = jnp.exp(jax.random.uniform(k[12], (L, GDN_HEADS), f32, math.log(1e-3), math.log(1e-1)))
    return {
        'x': normal(k[0], (BATCH, SEQ, D_MODEL), 1.0),
        'rel_table': normal(k[1], (REL_BUCKETS, NSA_HEADS), 0.5),
        'ln1_w': 1.0 + normal(k[2], (L, D_MODEL), 0.02),
        'w_in': normal(k[3], (L, D_MODEL, IN_COLS), D_MODEL ** -0.5),
        'cmp_pe_k': normal(k[4], (L, CMP_BLOCK, NSA_HEAD_DIM), 0.1),
        'cmp_pe_v': normal(k[5], (L, CMP_BLOCK, NSA_HEAD_DIM), 0.1),
        'cmp_w1_k': normal(k[6], (L, cmp_in, CMP_HIDDEN), cmp_in ** -0.5),
        'cmp_w2_k': normal(k[7], (L, CMP_HIDDEN, NSA_HEAD_DIM), CMP_HIDDEN ** -0.5),
        'cmp_w1_v': normal(k[8], (L, cmp_in, CMP_HIDDEN), cmp_in ** -0.5),
        'cmp_w2_v': normal(k[9], (L, CMP_HIDDEN, NSA_HEAD_DIM), CMP_HIDDEN ** -0.5),
        'conv_w': normal(k[10], (L, CONV_WIDTH, 3 * GDN_WIDTH), CONV_WIDTH ** -0.5),
        'a_log': jnp.log(jax.random.uniform(k[11], (L, GDN_HEADS), f32, 1.0, 16.0)),
        'dt_bias': dt + jnp.log(-jnp.expm1(-dt)),
        'gdn_norm_w': 1.0 + normal(k[13], (L, GDN_HEAD_DIM), 0.02),
        'w_pa': normal(k[14], (L, NSA_WIDTH, D_MODEL), NSA_WIDTH ** -0.5),
        'w_pb': normal(k[15], (L, GDN_WIDTH, D_MODEL), GDN_WIDTH ** -0.5),
        'w_o': normal(k[16], (L, D_MODEL, D_MODEL), D_MODEL ** -0.5),
        'ln2_w': 1.0 + normal(k[17], (L, D_MODEL), 0.02),
        'w_up': normal(k[18], (L, D_MODEL, D_FF), D_MODEL ** -0.5),
        'w_down': normal(k[19], (L, D_FF, D_MODEL), D_FF ** -0.5),
        'ln_f_w': 1.0 + normal(k[20], (D_MODEL,), 0.02),
    }


def reference(x, rel_table, ln1_w, w_in, cmp_pe_k, cmp_pe_v, cmp_w1_k, cmp_w2_k, cmp_w1_v, cmp_w2_v, conv_w,
              a_log, dt_bias, gdn_norm_w, w_pa, w_pb, w_o, ln2_w, w_up, w_down, ln_f_w):
    for l in range(DEPTH):
        x = hybrid_layer(x, rel_table, ln1_w[l], w_in[l], cmp_pe_k[l], cmp_pe_v[l], cmp_w1_k[l], cmp_w2_k[l],
                         cmp_w1_v[l], cmp_w2_v[l], conv_w[l], a_log[l], dt_bias[l], gdn_norm_w[l],
                         w_pa[l], w_pb[l], w_o[l], ln2_w[l], w_up[l], w_down[l])
    return rms_norm(x, ln_f_w)
```

```python
import functools
import math

import numpy as np
import jax
import jax.numpy as jnp
from jax import lax
from jax.experimental import pallas as pl
from jax.experimental.pallas import tpu as pltpu

F32 = jnp.float32
BF16 = jnp.bfloat16
HIGHEST = lax.Precision.HIGHEST

D_MODEL = 2048
DEPTH = 2
NSA_HEADS = 8
NSA_KV_GROUPS = 2
NSA_HPG = NSA_HEADS // NSA_KV_GROUPS
NSA_HEAD_DIM = 128
NSA_WIDTH = NSA_HEADS * NSA_HEAD_DIM
NSA_KV_WIDTH = NSA_KV_GROUPS * NSA_HEAD_DIM
CMP_BLOCK = 32
CMP_STRIDE = 16
CMP_HIDDEN = 256
SEL_BLOCK = 64
SEL_TOP_N = 16
SEL_FORCE = 1000.0
WINDOW = 512
GDN_HEADS = 8
GDN_HEAD_DIM = 128
GDN_WIDTH = GDN_HEADS * GDN_HEAD_DIM
CONV_WIDTH = 4
GDN_CHUNK = 64
REL_BUCKETS = 32
REL_MAX_DIST = 128
D_FF = 4 * D_MODEL
NORM_EPS = 1e-6

LANES = 128
VMEM_LIMIT = 56 * 1024 * 1024
TQ = 128
TK = 128
NEG_INF = float("-inf")

R1_COLS = 2048
R1_KS, R1_VS, R1_KW, R1_VW = 8, 10, 12, 14
R2_KC, R2_VC, R2_GQ, R2_GK, R2_GV, R2_Z, R2_MA, R2_MB, R2_SMALL = 0, 2, 4, 12, 20, 28, 36, 52, 68
R2_COLS = 72 * LANES
SMALL_GATE, SMALL_A, SMALL_B = 0, 24, 32


def _t5_thresholds():
    exact = REL_BUCKETS // 2
    n = np.arange(exact, REL_MAX_DIST * 2, dtype=np.int64)
    nf = n.astype(np.float32)
    large = exact + (np.log(nf / np.float32(exact)) / np.float32(math.log(REL_MAX_DIST / exact))
                     * np.float32(REL_BUCKETS - exact)).astype(np.int32)
    large = np.minimum(large, REL_BUCKETS - 1)
    thr = []
    for b in range(exact + 1, REL_BUCKETS):
        thr.append(int(n[np.argmax(large >= b)]))
    return tuple(thr)


T5_THRESHOLDS = _t5_thresholds()


def _cparams(*sem):
    return pltpu.CompilerParams(dimension_semantics=sem, vmem_limit_bytes=VMEM_LIMIT)


def _rmsnorm_kernel(x_ref, w_ref, o_ref):
    x = x_ref[...]
    y = x * lax.rsqrt(jnp.mean(x * x, axis=-1, keepdims=True) + NORM_EPS)
    o_ref[...] = (y * w_ref[...]).astype(o_ref.dtype)


def rmsnorm(x, w, out_dtype, tm=512):
    t, d = x.shape
    return pl.pallas_call(
        _rmsnorm_kernel,
        grid=(t // tm,),
        in_specs=[pl.BlockSpec((tm, d), lambda i: (i, 0)), pl.BlockSpec((1, d), lambda i: (0, 0))],
        out_specs=pl.BlockSpec((tm, d), lambda i: (i, 0)),
        out_shape=jax.ShapeDtypeStruct((t, d), out_dtype),
        compiler_params=_cparams("parallel"),
        name="rmsnorm",
    )(x, w.reshape(1, d))


def _mm_kernel(*refs, nk, epilogue):
    a_ref, w_ref = refs[0], refs[1]
    pos = 2
    res_ref = norm_ref = None
    if epilogue in ("residual", "residual_norm"):
        res_ref = refs[pos]
        pos += 1
    if epilogue == "residual_norm":
        norm_ref = refs[pos]
        pos += 1
    o_ref = refs[pos]
    acc_ref = refs[pos + 1] if nk > 1 else None

    def finish(acc):
        if epilogue == "relu2":
            acc = jnp.square(jnp.maximum(acc, 0.0))
        elif epilogue == "residual":
            acc = res_ref[...] + acc
        elif epilogue == "residual_norm":
            y = res_ref[...] + acc
            acc = y * lax.rsqrt(jnp.mean(y * y, axis=-1, keepdims=True) + NORM_EPS) * norm_ref[...]
        o_ref[...] = acc.astype(o_ref.dtype)

    part = jnp.dot(a_ref[...], w_ref[...], preferred_element_type=F32)
    if nk == 1:
        finish(part)
    else:
        k = pl.program_id(2)

        @pl.when(k == 0)
        def _():
            acc_ref[...] = part

        @pl.when(k > 0)
        def _():
            acc_ref[...] += part

        @pl.when(k == nk - 1)
        def _():
            finish(acc_ref[...])


def matmul(a, w, out_dtype, *, tm, tn, tk, epilogue="none", res=None, norm_w=None):
    m, kdim = a.shape
    n = w.shape[1]
    nk = kdim // tk
    in_specs = [pl.BlockSpec((tm, tk), lambda i, j, k: (i, k)), pl.BlockSpec((tk, tn), lambda i, j, k: (k, j))]
    args = [a, w]
    if epilogue in ("residual", "residual_norm"):
        in_specs.append(pl.BlockSpec((tm, tn), lambda i, j, k: (i, j)))
        args.append(res)
    if epilogue == "residual_norm":
        assert tn == n
        in_specs.append(pl.BlockSpec((1, tn), lambda i, j, k: (0, 0)))
        args.append(norm_w.reshape(1, n))
    scratch = [pltpu.VMEM((tm, tn), F32)] if nk > 1 else []
    return pl.pallas_call(
        functools.partial(_mm_kernel, nk=nk, epilogue=epilogue),
        grid=(m // tm, n // tn, nk),
        in_specs=in_specs,
        out_specs=pl.BlockSpec((tm, tn), lambda i, j, k: (i, j)),
        out_shape=jax.ShapeDtypeStruct((m, n), out_dtype),
        scratch_shapes=scratch,
        compiler_params=_cparams("parallel", "parallel", "arbitrary"),
        name="matmul_" + epilogue,
    )(*args)


def _merge_kernel(oa_ref, ob_ref, wa_ref, wb_ref, ma_ref, mb_ref, o_ref):
    pa = jnp.dot(oa_ref[...], wa_ref[...], preferred_element_type=F32)
    pb = jnp.dot(ob_ref[...], wb_ref[...], preferred_element_type=F32)
    o_ref[...] = (jax.nn.sigmoid(ma_ref[...]) * pa + jax.nn.sigmoid(mb_ref[...]) * pb).astype(o_ref.dtype)


def merge_branches(o_a, o_b, w_pa, w_pb, r2, *, tm=1024, tn=512):
    t, ka = o_a.shape
    kb = o_b.shape[1]
    n = w_pa.shape[1]
    ma0 = R2_MA * LANES // tn
    mb0 = R2_MB * LANES // tn
    return pl.pallas_call(
        _merge_kernel,
        grid=(t // tm, n // tn),
        in_specs=[
            pl.BlockSpec((tm, ka), lambda i, j: (i, 0)),
            pl.BlockSpec((tm, kb), lambda i, j: (i, 0)),
            pl.BlockSpec((ka, tn), lambda i, j: (0, j)),
            pl.BlockSpec((kb, tn), lambda i, j: (0, j)),
            pl.BlockSpec((tm, tn), lambda i, j: (i, ma0 + j)),
            pl.BlockSpec((tm, tn), lambda i, j: (i, mb0 + j)),
        ],
        out_specs=pl.BlockSpec((tm, tn), lambda i, j: (i, j)),
        out_shape=jax.ShapeDtypeStruct((t, n), BF16),
        compiler_params=_cparams("parallel", "parallel"),
        name="merge_branches",
    )(o_a, o_b, w_pa, w_pb, r2, r2)


def _t5_lookup(dist, tab_ref, h):
    n = jnp.maximum(dist, 0)
    exact = REL_BUCKETS // 2
    large = jnp.full(n.shape, exact, jnp.int32)
    for thr in T5_THRESHOLDS:
        large = large + (n >= thr).astype(jnp.int32)
    bucket = jnp.where(n < exact, n, large)
    out = jnp.zeros(n.shape, F32)
    for b in range(REL_BUCKETS):
        out = jnp.where(bucket == b, tab_ref[b, h], out)
    return out


def _bias_kernel(tab_ref, bc_ref, bd_ref, *, rows):
    h = pl.program_id(0)
    s = bc_ref.shape[0]
    for c in range(s // rows):
        t = lax.broadcasted_iota(jnp.int32, (rows, LANES), 0) + c * rows
        n = lax.broadcasted_iota(jnp.int32, (rows, LANES), 1)
        bc_ref[c * rows:(c + 1) * rows, :] = _t5_lookup(t - (n * CMP_STRIDE + CMP_BLOCK - 1), tab_ref, h)
    r = lax.broadcasted_iota(jnp.int32, (TQ, TK), 0)
    j = lax.broadcasted_iota(jnp.int32, (TQ, TK), 1)
    for v in range(3):
        bd_ref[v] = _t5_lookup(v * TK + r - j, tab_ref, h)


def bias_tables(rel_table, s):
    return pl.pallas_call(
        functools.partial(_bias_kernel, rows=256),
        grid=(NSA_HEADS,),
        in_specs=[pl.BlockSpec(memory_space=pltpu.SMEM)],
        out_specs=[pl.BlockSpec((None, s, LANES), lambda h: (h, 0, 0)),
                   pl.BlockSpec((None, 3, TQ, TK), lambda h: (h, 0, 0, 0))],
        out_shape=[jax.ShapeDtypeStruct((NSA_HEADS, s, LANES), F32),
                   jax.ShapeDtypeStruct((NSA_HEADS, 3, TQ, TK), F32)],
        compiler_params=_cparams("arbitrary"),
        name="t5_bias_tables",
    )(rel_table)


def _gelu_tanh(x):
    return 0.5 * x * (1.0 + jnp.tanh(math.sqrt(2.0 / math.pi) * (x + 0.044715 * (x * x * x))))


def _compress_kernel(x_ref, pe_ref, w1_ref, w2_ref, o_ref, *, nblk):
    pe = pe_ref[...]
    acc_a = jnp.zeros((nblk, CMP_HIDDEN), F32)
    acc_b = jnp.zeros((nblk, CMP_HIDDEN), F32)
    for r in range(CMP_STRIDE):
        xr = x_ref[pl.ds(r, nblk, stride=CMP_STRIDE), :]
        xa = (xr + pe[r:r + 1, :]).astype(BF16)
        xb = (xr + pe[CMP_STRIDE + r:CMP_STRIDE + r + 1, :]).astype(BF16)
        acc_a = acc_a + jnp.dot(xa, w1_ref[r * NSA_HEAD_DIM:(r + 1) * NSA_HEAD_DIM, :], preferred_element_type=F32)
        acc_b = acc_b + jnp.dot(xb, w1_ref[(CMP_STRIDE + r) * NSA_HEAD_DIM:(CMP_STRIDE + r + 1) * NSA_HEAD_DIM, :],
                                preferred_element_type=F32)
    hidden = acc_a + jnp.concatenate([acc_b[1:], acc_b[:1]], axis=0)
    act = _gelu_tanh(hidden).astype(BF16)
    o_ref[...] = jnp.dot(act, w2_ref[...], preferred_element_type=F32)


def nsa_compress(r2, pe, w1, w2):
    b, s, _ = r2.shape
    nblk = s // CMP_STRIDE
    assert CMP_BLOCK == 2 * CMP_STRIDE
    return pl.pallas_call(
        functools.partial(_compress_kernel, nblk=nblk),
        grid=(b, NSA_KV_GROUPS, 2),
        in_specs=[
            pl.BlockSpec((None, s, LANES), lambda bi, g, c: (bi, 0, R2_KC + 2 * c + g)),
            pl.BlockSpec((None, CMP_BLOCK, NSA_HEAD_DIM), lambda bi, g, c: (c, 0, 0)),
            pl.BlockSpec((None, CMP_BLOCK * NSA_HEAD_DIM, CMP_HIDDEN), lambda bi, g, c: (c, 0, 0)),
            pl.BlockSpec((None, CMP_HIDDEN, NSA_HEAD_DIM), lambda bi, g, c: (c, 0, 0)),
        ],
        out_specs=pl.BlockSpec((None, None, None, nblk, NSA_HEAD_DIM), lambda bi, g, c: (bi, g, c, 0, 0)),
        out_shape=jax.ShapeDtypeStruct((b, NSA_KV_GROUPS, 2, nblk, NSA_HEAD_DIM), F32),
        compiler_params=_cparams("parallel", "parallel", "arbitrary"),
        name="nsa_compress",
    )(r2, pe, w1, w2)


def _nsa_attn_kernel(q_ref, ks_ref, vs_ref, kw_ref, vw_ref, kcvc_ref, gate_ref, biasc_ref, biasd_ref,
                     cover_ref, expand_ref, o_ref, sel_ref, m_ref, l_ref, acc_ref, res_ref, *, n_sel, top_n):
    g = pl.program_id(1)
    i = pl.program_id(2)
    rows = NSA_HPG * TQ
    scale = NSA_HEAD_DIM ** -0.5
    nt = (((1,), (1,)), ((), ()))

    qf = q_ref[...]
    qs = jnp.concatenate([qf[:, j * NSA_HEAD_DIM:(j + 1) * NSA_HEAD_DIM] for j in range(NSA_HPG)], axis=0)
    row_t = lax.broadcasted_iota(jnp.int32, (TQ, TK), 0)
    lane = lax.broadcasted_iota(jnp.int32, (TQ, TK), 1)
    t_abs = i * TQ + row_t

    sg = jax.nn.sigmoid(gate_ref[...])

    def gate_col(c):
        cols = []
        for j in range(NSA_HPG):
            c0 = SMALL_GATE + j * 3 + c
            c1 = SMALL_GATE + (NSA_HPG + j) * 3 + c
            cols.append(jnp.where(g == 0, sg[:, c0:c0 + 1], sg[:, c1:c1 + 1]))
        return jnp.concatenate(cols, axis=0)

    kc = kcvc_ref[0].astype(BF16)
    vc = kcvc_ref[1].astype(BF16)
    s3 = (lax.dot_general(qs, kc, nt, preferred_element_type=F32) * scale).reshape(NSA_HPG, TQ, TK) + biasc_ref[...]
    valid_c = (lane * CMP_STRIDE + (CMP_BLOCK - 1)) <= t_abs
    s3 = jnp.where(valid_c[None], s3, NEG_INF)
    mx = jnp.max(s3, axis=-1, keepdims=True)
    mx = jnp.where(mx == NEG_INF, 0.0, mx)
    e3 = jnp.exp(s3 - mx)
    p3 = e3 / jnp.maximum(jnp.sum(e3, axis=-1, keepdims=True), 1e-30)
    o_c = jnp.dot(p3.reshape(rows, TK).astype(BF16), vc, preferred_element_type=F32)
    res_ref[...] = gate_col(0) * o_c

    psum = p3[0]
    for j in range(1, NSA_HPG):
        psum = psum + p3[j]
    imp = jnp.dot(psum, cover_ref[...], preferred_element_type=F32, precision=HIGHEST)
    cur = t_abs // SEL_BLOCK
    causal_blk = lane <= cur
    forced = (lane == 0) | (lane == cur) | (lane == cur - 1)
    score = jnp.where(causal_blk, imp + jnp.where(forced, SEL_FORCE, 0.0), NEG_INF)
    rank = jnp.zeros((TQ, TK), jnp.int32)
    for mp in range(n_sel):
        col = score[:, mp:mp + 1]
        beats = (col > score) | ((col == score) & (lane > mp))
        rank = rank + beats.astype(jnp.int32)
    sel = jnp.where((rank < top_n) & causal_blk, 1.0, 0.0).astype(BF16)
    selmask = jnp.dot(sel, expand_ref[...], preferred_element_type=F32)
    for k in range(sel_ref.shape[0]):
        sel_ref[k] = selmask[:, k * TK:(k + 1) * TK]

    def reset():
        m_ref[...] = jnp.full(m_ref.shape, NEG_INF, F32)
        l_ref[...] = jnp.zeros(l_ref.shape, F32)
        acc_ref[...] = jnp.zeros(acc_ref.shape, F32)

    def attend(k_tile, v_tile, variant, valid):
        s = lax.dot_general(qs, k_tile, nt, preferred_element_type=F32) * scale
        s = s.reshape(NSA_HPG, TQ, TK) + biasd_ref[:, variant]
        if valid is not None:
            s = jnp.where(valid[None], s, NEG_INF)
        s = s.reshape(rows, TK)
        m_prev = m_ref[...]
        m_new = jnp.maximum(m_prev, jnp.max(s, axis=-1, keepdims=True))
        m_safe = jnp.where(m_new == NEG_INF, 0.0, m_new)
        alpha = jnp.exp(m_prev - m_safe)
        p = jnp.exp(s - m_safe)
        l_ref[...] = alpha * l_ref[...] + jnp.sum(p, axis=-1, keepdims=True)
        acc_ref[...] = alpha * acc_ref[...] + jnp.dot(p.astype(BF16), v_tile, preferred_element_type=F32)
        m_ref[...] = m_new

    def finish():
        return acc_ref[...] / jnp.maximum(l_ref[...], 1e-30)

    def tile(ref, k):
        return ref[pl.ds(pl.multiple_of(k * TK, TK), TK), :]

    causal = lane <= row_t

    reset()

    def sel_far(k, carry):
        attend(tile(ks_ref, k), tile(vs_ref, k), 2, sel_ref[k] > 0.5)
        return carry

    lax.fori_loop(0, jnp.maximum(i - 1, 0), sel_far, 0)

    @pl.when(i >= 1)
    def _():
        attend(tile(ks_ref, i - 1), tile(vs_ref, i - 1), 1, sel_ref[i - 1] > 0.5)

    attend(tile(ks_ref, i), tile(vs_ref, i), 0, (sel_ref[i] > 0.5) & causal)
    res_ref[...] += gate_col(1) * finish()

    reset()
    n_win = WINDOW // TK
    for d in range(n_win, 0, -1):
        @pl.when(i >= d)
        def _(d=d):
            valid = (lane > row_t) if d == n_win else None
            attend(tile(kw_ref, i - d), tile(vw_ref, i - d), min(d, 2), valid)

    attend(tile(kw_ref, i), tile(vw_ref, i), 0, causal)
    res = res_ref[...] + gate_col(2) * finish()
    for j in range(NSA_HPG):
        o_ref[:, j * NSA_HEAD_DIM:(j + 1) * NSA_HEAD_DIM] = res[j * TQ:(j + 1) * TQ].astype(o_ref.dtype)


def nsa_attention(r1, r2, kcvc, bias_c, bias_d):
    b, s, _ = r1.shape
    n_cmp_rows = s // CMP_STRIDE
    n_sel = s // SEL_BLOCK
    assert n_cmp_rows == TK and n_sel <= LANES and WINDOW % TK == 0 and TQ == TK and 2 * TK > REL_MAX_DIST + TQ - 1 - 1
    top_n = min(SEL_TOP_N, n_sel)
    nq = s // TQ
    gw = NSA_HPG * NSA_HEAD_DIM

    n_cmp = n_cmp_rows - CMP_BLOCK // CMP_STRIDE + 1
    c_start = np.arange(n_cmp_rows) * CMP_STRIDE
    s_start = np.arange(LANES) * SEL_BLOCK
    cover = ((c_start[:, None] <= s_start[None, :] + SEL_BLOCK - 1) & (c_start[:, None] + CMP_BLOCK - 1 >= s_start[None, :])
             & (np.arange(n_cmp_rows)[:, None] < n_cmp) & (np.arange(LANES)[None, :] < n_sel))
    cover = jnp.asarray(cover.astype(np.float32))
    expand = jnp.asarray((np.arange(LANES)[:, None] == (np.arange(s)[None, :] // SEL_BLOCK)).astype(np.float32), BF16)

    kv_spec = lambda blk: pl.BlockSpec((None, s, LANES), lambda bi, g, i, blk=blk: (bi, 0, blk + g))
    return pl.pallas_call(
        functools.partial(_nsa_attn_kernel, n_sel=n_sel, top_n=top_n),
        grid=(b, NSA_KV_GROUPS, nq),
        in_specs=[
            pl.BlockSpec((None, TQ, gw), lambda bi, g, i: (bi, i, g)),
            kv_spec(R1_KS), kv_spec(R1_VS), kv_spec(R1_KW), kv_spec(R1_VW),
            pl.BlockSpec((None, None, 2, n_cmp_rows, NSA_HEAD_DIM), lambda bi, g, i: (bi, g, 0, 0, 0)),
            pl.BlockSpec((None, TQ, LANES), lambda bi, g, i: (bi, i, R2_SMALL)),
            pl.BlockSpec((NSA_HPG, TQ, LANES), lambda bi, g, i: (g, i, 0)),
            pl.BlockSpec((NSA_HPG, 3, TQ, TK), lambda bi, g, i: (g, 0, 0, 0)),
            pl.BlockSpec((n_cmp_rows, LANES), lambda bi, g, i: (0, 0)),
            pl.BlockSpec((LANES, s), lambda bi, g, i: (0, 0)),
        ],
        out_specs=pl.BlockSpec((None, TQ, gw), lambda bi, g, i: (bi, i, g)),
        out_shape=jax.ShapeDtypeStruct((b, s, NSA_WIDTH), BF16),
        scratch_shapes=[
            pltpu.VMEM((s // TK, TQ, TK), F32),
            pltpu.VMEM((NSA_HPG * TQ, 1), F32),
            pltpu.VMEM((NSA_HPG * TQ, 1), F32),
            pltpu.VMEM((NSA_HPG * TQ, NSA_HEAD_DIM), F32),
            pltpu.VMEM((NSA_HPG * TQ, NSA_HEAD_DIM), F32),
        ],
        compiler_params=_cparams("parallel", "parallel", "arbitrary"),
        name="nsa_attention",
    )(r1, r1, r1, r1, r1, kcvc, r2, bias_c, bias_d, cover, expand)


def _gdn_kernel(alog_ref, dtb_ref, q_ref, k_ref, v_ref, z_ref, sm_ref, cwq_ref, cwk_ref, cwv_ref, nw_ref, o_ref,
                qs_ref, ks_ref, vs_ref, g_ref, beta_ref, os_ref):
    h = pl.program_id(1)
    s = q_ref.shape[0]
    c = GDN_CHUNK
    dk = GDN_HEAD_DIM
    nt = (((1,), (1,)), ((), ()))
    tn = (((0,), (0,)), ((), ()))
    row = lax.broadcasted_iota(jnp.int32, (s, dk), 0)
    lane = lax.broadcasted_iota(jnp.int32, (s, dk), 1)

    def conv_silu(x_ref, w_ref):
        x = x_ref[...]
        w = w_ref[...]
        y = x * w[CONV_WIDTH - 1:CONV_WIDTH, :]
        for sft in range(1, CONV_WIDTH):
            xs = jnp.where(row >= sft, pltpu.roll(x, sft, axis=0), 0.0)
            y = y + xs * w[CONV_WIDTH - 1 - sft:CONV_WIDTH - sft, :]
        return y * jax.nn.sigmoid(y)

    def l2n(x):
        return x * lax.rsqrt(jnp.sum(x * x, axis=-1, keepdims=True) + NORM_EPS)

    qs_ref[...] = l2n(conv_silu(q_ref, cwq_ref)) * (dk ** -0.5)
    ks_ref[...] = l2n(conv_silu(k_ref, cwk_ref))
    vs_ref[...] = conv_silu(v_ref, cwv_ref)

    sm = sm_ref[...]
    a_col = jnp.sum(jnp.where(lane == SMALL_A + h, sm, 0.0), axis=-1, keepdims=True)
    b_col = jnp.sum(jnp.where(lane == SMALL_B + h, sm, 0.0), axis=-1, keepdims=True)
    a_neg = -jnp.exp(jnp.full((1, 1), alog_ref[h], F32))
    xg = a_col + dtb_ref[h]
    softplus = jnp.maximum(xg, 0.0) + jnp.log1p(jnp.exp(-jnp.abs(xg)))
    g_ref[...] = jnp.broadcast_to(a_neg * softplus, (s, dk))
    beta_ref[...] = jnp.broadcast_to(jax.nn.sigmoid(b_col), (s, dk))

    r64 = lax.broadcasted_iota(jnp.int32, (c, c), 0)
    c64 = lax.broadcasted_iota(jnp.int32, (c, c), 1)
    tri = r64 >= c64
    strict = r64 > c64
    eye = r64 == c64
    tri_f = tri.astype(F32)
    eye_f = eye.astype(F32)
    ones_f = jnp.ones((c, c), F32)

    def chunk(ci, state):
        r0 = pl.multiple_of(ci * c, c)
        q = qs_ref[pl.ds(r0, c), :]
        k = ks_ref[pl.ds(r0, c), :]
        v = vs_ref[pl.ds(r0, c), :]
        bt = beta_ref[pl.ds(r0, c), :]
        gcum = jnp.dot(tri_f, g_ref[pl.ds(r0, c), :], preferred_element_type=F32, precision=HIGHEST)
        gc = gcum[:, :c]
        g_row = jnp.dot(ones_f, jnp.where(eye, gc, 0.0), preferred_element_type=F32, precision=HIGHEST)
        decay = jnp.where(tri, jnp.exp(jnp.where(tri, gc - g_row, 0.0)), 0.0)
        kb = k * bt
        vb = v * bt
        kbf = k.astype(BF16)
        lower = jnp.where(strict, lax.dot_general(kb.astype(BF16), kbf, nt, preferred_element_type=F32) * decay, 0.0)
        tinv = eye_f - lower
        pw = lower
        for _ in range(int(math.log2(c)) - 1):
            pw = jnp.dot(pw, pw, preferred_element_type=F32, precision=HIGHEST)
            tinv = tinv + jnp.dot(tinv, pw, preferred_element_type=F32, precision=HIGHEST)
        eg = jnp.exp(gcum)
        tb = tinv.astype(BF16)
        u = jnp.dot(tb, vb.astype(BF16), preferred_element_type=F32)
        w = jnp.dot(tb, (kb * eg).astype(BF16), preferred_element_type=F32)
        qk = lax.dot_general(q.astype(BF16), kbf, nt, preferred_element_type=F32) * decay
        sb = state.astype(BF16)
        v_new = u - jnp.dot(w.astype(BF16), sb, preferred_element_type=F32)
        vnb = v_new.astype(BF16)
        o = jnp.dot((q * eg).astype(BF16), sb, preferred_element_type=F32) + jnp.dot(qk.astype(BF16), vnb,
                                                                                    preferred_element_type=F32)
        os_ref[pl.ds(r0, c), :] = o
        g_last = gcum[c - 1:c, :]
        kd = (k * jnp.exp(g_last - gcum)).astype(BF16)
        return state * jnp.exp(g_last) + lax.dot_general(kd, vnb, tn, preferred_element_type=F32)

    lax.fori_loop(0, s // c, chunk, jnp.zeros((dk, dk), F32))

    o = os_ref[...]
    o = o * lax.rsqrt(jnp.mean(o * o, axis=-1, keepdims=True) + NORM_EPS) * nw_ref[...]
    z = z_ref[...]
    o_ref[...] = (o * (z * jax.nn.sigmoid(z))).astype(o_ref.dtype)


def gated_deltanet(r2, conv_w, a_log, dt_bias, norm_w):
    b, s, _ = r2.shape
    dk = GDN_HEAD_DIM
    col = lambda blk: pl.BlockSpec((None, s, dk), lambda bi, h, blk=blk: (bi, 0, blk + h))
    cw = lambda blk: pl.BlockSpec((CONV_WIDTH, dk), lambda bi, h, blk=blk: (0, blk + h))
    smem = pl.BlockSpec(memory_space=pltpu.SMEM)
    return pl.pallas_call(
        _gdn_kernel,
        grid=(b, GDN_HEADS),
        in_specs=[smem, smem, col(R2_GQ), col(R2_GK), col(R2_GV), col(R2_Z),
                  pl.BlockSpec((None, s, LANES), lambda bi, h: (bi, 0, R2_SMALL)),
                  cw(0), cw(GDN_HEADS), cw(2 * GDN_HEADS),
                  pl.BlockSpec((1, dk), lambda bi, h: (0, 0))],
        out_specs=pl.BlockSpec((None, s, dk), lambda bi, h: (bi, 0, h)),
        out_shape=jax.ShapeDtypeStruct((b, s, GDN_WIDTH), BF16),
        scratch_shapes=[pltpu.VMEM((s, dk), F32) for _ in range(6)],
        compiler_params=_cparams("parallel", "parallel"),
        name="gated_deltanet",
    )(a_log, dt_bias, r2, r2, r2, r2, r2, conv_w, conv_w, conv_w, norm_w.reshape(1, dk))


def _split_w_in(w_in):
    o_q = 0
    o_kv = NSA_WIDTH
    o_gate = o_kv + 6 * NSA_KV_WIDTH
    o_qkv = o_gate + 3 * NSA_HEADS
    o_z = o_qkv + 3 * GDN_WIDTH
    o_a = o_z + GDN_WIDTH
    o_b = o_a + GDN_HEADS
    o_m = o_b + GDN_HEADS
    kvw = NSA_KV_WIDTH
    w1 = jnp.concatenate([w_in[:, o_q:o_kv], w_in[:, o_kv + 2 * kvw:o_kv + 6 * kvw]], axis=1)
    small = jnp.concatenate([w_in[:, o_gate:o_qkv], w_in[:, o_a:o_m]], axis=1)
    used = (R2_SMALL * LANES) + small.shape[1]
    w2 = jnp.concatenate([w_in[:, o_kv:o_kv + 2 * kvw], w_in[:, o_qkv:o_a], w_in[:, o_m:], small,
                          jnp.zeros((w_in.shape[0], R2_COLS - used), w_in.dtype)], axis=1)
    return w1.astype(BF16), w2.astype(BF16)


def _layer(x, bias_c, bias_d, ln1_w, w_in, pe_k, pe_v, w1_k, w2_k, w1_v, w2_v, conv_w, a_log, dt_bias,
           gdn_norm_w, w_pa, w_pb, w_o, ln2_w, w_up, w_down, final_norm_w, b, s):
    t = b * s
    w_r1, w_r2 = _split_w_in(w_in)
    h = rmsnorm(x, ln1_w, BF16)
    r1 = matmul(h, w_r1, BF16, tm=1024, tn=1024, tk=D_MODEL)
    r2 = matmul(h, w_r2, F32, tm=1024, tn=1024, tk=D_MODEL)
    r1 = r1.reshape(b, s, R1_COLS)
    r2 = r2.reshape(b, s, R2_COLS)
    kcvc = nsa_compress(r2, jnp.stack([pe_k, pe_v]), jnp.stack([w1_k, w1_v]).astype(BF16),
                        jnp.stack([w2_k, w2_v]).astype(BF16))
    o_a = nsa_attention(r1, r2, kcvc, bias_c, bias_d).reshape(t, NSA_WIDTH)
    o_b = gated_deltanet(r2, conv_w, a_log, dt_bias, gdn_norm_w).reshape(t, GDN_WIDTH)
    m = merge_branches(o_a, o_b, w_pa.astype(BF16), w_pb.astype(BF16), r2.reshape(t, R2_COLS))
    x = matmul(m, w_o.astype(BF16), F32, tm=1024, tn=1024, tk=D_MODEL, epilogue="residual", res=x)
    h2 = rmsnorm(x, ln2_w, BF16)
    u = matmul(h2, w_up.astype(BF16), BF16, tm=1024, tn=1024, tk=D_MODEL, epilogue="relu2")
    if final_norm_w is None:
        return matmul(u, w_down.astype(BF16), F32, tm=512, tn=D_MODEL, tk=1024, epilogue="residual", res=x)
    return matmul(u, w_down.astype(BF16), F32, tm=512, tn=D_MODEL, tk=1024, epilogue="residual_norm", res=x,
                  norm_w=final_norm_w)


def kernel(x, rel_table, ln1_w, w_in, cmp_pe_k, cmp_pe_v, cmp_w1_k, cmp_w2_k, cmp_w1_v, cmp_w2_v, conv_w, a_log,
           dt_bias, gdn_norm_w, w_pa, w_pb, w_o, ln2_w, w_up, w_down, ln_f_w):
    b, s, d = x.shape
    depth = w_in.shape[0]
    bias_c, bias_d = bias_tables(rel_table, s)
    xt = x.reshape(b * s, d)
    for l in range(depth):
        xt = _layer(xt, bias_c, bias_d, ln1_w[l], w_in[l], cmp_pe_k[l], cmp_pe_v[l], cmp_w1_k[l], cmp_w2_k[l],
                    cmp_w1_v[l], cmp_w2_v[l], conv_w[l], a_log[l], dt_bias[l], gdn_norm_w[l], w_pa[l], w_pb[l],
                    w_o[l], ln2_w[l], w_up[l], w_down[l], ln_f_w if l == depth - 1 else None, b, s)
    return xt.reshape(b, s, d)
```

```python
import functools
import math

import numpy as np
import jax
import jax.numpy as jnp
from jax import lax
from jax.experimental import pallas as pl
from jax.experimental.pallas import tpu as pltpu

F32 = jnp.float32
BF16 = jnp.bfloat16
HIGHEST = lax.Precision.HIGHEST

D_MODEL = 2048
DEPTH = 2
NSA_HEADS = 8
NSA_KV_GROUPS = 2
NSA_HPG = NSA_HEADS // NSA_KV_GROUPS
NSA_HEAD_DIM = 128
NSA_WIDTH = NSA_HEADS * NSA_HEAD_DIM
NSA_KV_WIDTH = NSA_KV_GROUPS * NSA_HEAD_DIM
CMP_BLOCK = 32
CMP_STRIDE = 16
CMP_HIDDEN = 256
SEL_BLOCK = 64
SEL_TOP_N = 16
SEL_FORCE = 1000.0
WINDOW = 512
GDN_HEADS = 8
GDN_HEAD_DIM = 128
GDN_WIDTH = GDN_HEADS * GDN_HEAD_DIM
CONV_WIDTH = 4
GDN_CHUNK = 64
REL_BUCKETS = 32
REL_MAX_DIST = 128
D_FF = 4 * D_MODEL
NORM_EPS = 1e-6

LANES = 128
VMEM_LIMIT = 56 * 1024 * 1024
TQ = 128
TK = 128
NEG_INF = float("-inf")

R1_COLS = 2048
R1_KS, R1_VS, R1_KW, R1_VW = 8, 10, 12, 14
R2_Z, R2_KC, R2_VC, R2_GQ, R2_GK, R2_GV, R2_MA, R2_MB, R2_SMALL = 0, 8, 10, 12, 20, 28, 36, 52, 68
R2_COLS = 72 * LANES
SMALL_GATE, SMALL_A, SMALL_B = 0, 24, 32


def _t5_thresholds():
    exact = REL_BUCKETS // 2
    n = np.arange(exact, REL_MAX_DIST * 2, dtype=np.int64)
    nf = n.astype(np.float32)
    large = exact + (np.log(nf / np.float32(exact)) / np.float32(math.log(REL_MAX_DIST / exact))
                     * np.float32(REL_BUCKETS - exact)).astype(np.int32)
    large = np.minimum(large, REL_BUCKETS - 1)
    thr = []
    for b in range(exact + 1, REL_BUCKETS):
        thr.append(int(n[np.argmax(large >= b)]))
    return tuple(thr)


T5_THRESHOLDS = _t5_thresholds()


def _cparams(*sem):
    return pltpu.CompilerParams(dimension_semantics=sem, vmem_limit_bytes=VMEM_LIMIT)


def _rmsnorm_kernel(x_ref, w_ref, o_ref):
    x = x_ref[...]
    y = x * lax.rsqrt(jnp.mean(x * x, axis=-1, keepdims=True) + NORM_EPS)
    o_ref[...] = (y * w_ref[...]).astype(o_ref.dtype)


def rmsnorm(x, w, out_dtype, tm=512):
    t, d = x.shape
    return pl.pallas_call(
        _rmsnorm_kernel,
        grid=(t // tm,),
        in_specs=[pl.BlockSpec((tm, d), lambda i: (i, 0)), pl.BlockSpec((1, d), lambda i: (0, 0))],
        out_specs=pl.BlockSpec((tm, d), lambda i: (i, 0)),
        out_shape=jax.ShapeDtypeStruct((t, d), out_dtype),
        compiler_params=_cparams("parallel"),
        name="rmsnorm",
    )(x, w.reshape(1, d))


def _mm_kernel(*refs, nk, epilogue):
    a_ref, w_ref = refs[0], refs[1]
    pos = 2
    res_ref = norm_ref = None
    if epilogue in ("residual", "residual_norm"):
        res_ref = refs[pos]
        pos += 1
    if epilogue == "residual_norm":
        norm_ref = refs[pos]
        pos += 1
    o_ref = refs[pos]
    acc_ref = refs[pos + 1] if nk > 1 else None

    def finish(acc):
        if epilogue == "relu2":
            acc = jnp.square(jnp.maximum(acc, 0.0))
        elif epilogue == "residual":
            acc = res_ref[...] + acc
        elif epilogue == "residual_norm":
            y = res_ref[...] + acc
            acc = y * lax.rsqrt(jnp.mean(y * y, axis=-1, keepdims=True) + NORM_EPS) * norm_ref[...]
        o_ref[...] = acc.astype(o_ref.dtype)

    part = jnp.dot(a_ref[...], w_ref[...], preferred_element_type=F32)
    if nk == 1:
        finish(part)
    else:
        k = pl.program_id(2)

        @pl.when(k == 0)
        def _():
            acc_ref[...] = part

        @pl.when(k > 0)
        def _():
            acc_ref[...] += part

        @pl.when(k == nk - 1)
        def _():
            finish(acc_ref[...])


def matmul(a, w, out_dtype, *, tm, tn, tk, epilogue="none", res=None, norm_w=None):
    m, kdim = a.shape
    n = w.shape[1]
    nk = kdim // tk
    in_specs = [pl.BlockSpec((tm, tk), lambda i, j, k: (i, k)), pl.BlockSpec((tk, tn), lambda i, j, k: (k, j))]
    args = [a, w]
    if epilogue in ("residual", "residual_norm"):
        in_specs.append(pl.BlockSpec((tm, tn), lambda i, j, k: (i, j)))
        args.append(res)
    if epilogue == "residual_norm":
        assert tn == n
        in_specs.append(pl.BlockSpec((1, tn), lambda i, j, k: (0, 0)))
        args.append(norm_w.reshape(1, n))
    scratch = [pltpu.VMEM((tm, tn), F32)] if nk > 1 else []
    return pl.pallas_call(
        functools.partial(_mm_kernel, nk=nk, epilogue=epilogue),
        grid=(m // tm, n // tn, nk),
        in_specs=in_specs,
        out_specs=pl.BlockSpec((tm, tn), lambda i, j, k: (i, j)),
        out_shape=jax.ShapeDtypeStruct((m, n), out_dtype),
        scratch_shapes=scratch,
        compiler_params=_cparams("parallel", "parallel", "arbitrary"),
        name="matmul_" + epilogue,
    )(*args)


def _merge_kernel(oa_ref, ob_ref, wa_ref, wb_ref, ma_ref, mb_ref, o_ref):
    pa = jnp.dot(oa_ref[...], wa_ref[...], preferred_element_type=F32)
    pb = jnp.dot(ob_ref[...], wb_ref[...], preferred_element_type=F32)
    o_ref[...] = (jax.nn.sigmoid(ma_ref[...]) * pa + jax.nn.sigmoid(mb_ref[...]) * pb).astype(o_ref.dtype)


def merge_branches(o_a, o_b, w_pa, w_pb, r2, *, tm=1024, tn=512):
    t, ka = o_a.shape
    kb = o_b.shape[1]
    n = w_pa.shape[1]
    ma0 = R2_MA * LANES // tn
    mb0 = R2_MB * LANES // tn
    return pl.pallas_call(
        _merge_kernel,
        grid=(t // tm, n // tn),
        in_specs=[
            pl.BlockSpec((tm, ka), lambda i, j: (i, 0)),
            pl.BlockSpec((tm, kb), lambda i, j: (i, 0)),
            pl.BlockSpec((ka, tn), lambda i, j: (0, j)),
            pl.BlockSpec((kb, tn), lambda i, j: (0, j)),
            pl.BlockSpec((tm, tn), lambda i, j: (i, ma0 + j)),
            pl.BlockSpec((tm, tn), lambda i, j: (i, mb0 + j)),
        ],
        out_specs=pl.BlockSpec((tm, tn), lambda i, j: (i, j)),
        out_shape=jax.ShapeDtypeStruct((t, n), BF16),
        compiler_params=_cparams("parallel", "parallel"),
        name="merge_branches",
    )(o_a, o_b, w_pa, w_pb, r2, r2)


def _t5_lookup(dist, tab_ref, h):
    n = jnp.maximum(dist, 0)
    exact = REL_BUCKETS // 2
    large = jnp.full(n.shape, exact, jnp.int32)
    for thr in T5_THRESHOLDS:
        large = large + (n >= thr).astype(jnp.int32)
    bucket = jnp.where(n < exact, n, large)
    out = jnp.zeros(n.shape, F32)
    for b in range(REL_BUCKETS):
        out = jnp.where(bucket == b, tab_ref[b, h], out)
    return out


def _bias_kernel(tab_ref, bc_ref, bd_ref, *, rows):
    h = pl.program_id(0)
    s = bc_ref.shape[0]
    for c in range(s // rows):
        t = lax.broadcasted_iota(jnp.int32, (rows, LANES), 0) + c * rows
        n = lax.broadcasted_iota(jnp.int32, (rows, LANES), 1)
        bc_ref[c * rows:(c + 1) * rows, :] = _t5_lookup(t - (n * CMP_STRIDE + CMP_BLOCK - 1), tab_ref, h)
    r = lax.broadcasted_iota(jnp.int32, (TQ, TK), 0)
    j = lax.broadcasted_iota(jnp.int32, (TQ, TK), 1)
    for v in range(3):
        bd_ref[v] = _t5_lookup(v * TK + r - j, tab_ref, h)


def bias_tables(rel_table, s):
    return pl.pallas_call(
        functools.partial(_bias_kernel, rows=256),
        grid=(NSA_HEADS,),
        in_specs=[pl.BlockSpec(memory_space=pltpu.SMEM)],
        out_specs=[pl.BlockSpec((None, s, LANES), lambda h: (h, 0, 0)),
                   pl.BlockSpec((None, 3, TQ, TK), lambda h: (h, 0, 0, 0))],
        out_shape=[jax.ShapeDtypeStruct((NSA_HEADS, s, LANES), F32),
                   jax.ShapeDtypeStruct((NSA_HEADS, 3, TQ, TK), F32)],
        compiler_params=_cparams("arbitrary"),
        name="t5_bias_tables",
    )(rel_table)


def _gelu_tanh(x):
    return 0.5 * x * (1.0 + jnp.tanh(math.sqrt(2.0 / math.pi) * (x + 0.044715 * (x * x * x))))


def _compress_kernel(x_ref, pe_ref, w1_ref, w2_ref, o_ref, *, nblk):
    pe = pe_ref[...]
    acc_a = jnp.zeros((nblk, CMP_HIDDEN), F32)
    acc_b = jnp.zeros((nblk, CMP_HIDDEN), F32)
    for r in range(CMP_STRIDE):
        xr = x_ref[pl.ds(r, nblk, stride=CMP_STRIDE), :]
        xa = (xr + pe[r:r + 1, :]).astype(BF16)
        xb = (xr + pe[CMP_STRIDE + r:CMP_STRIDE + r + 1, :]).astype(BF16)
        acc_a = acc_a + jnp.dot(xa, w1_ref[r * NSA_HEAD_DIM:(r + 1) * NSA_HEAD_DIM, :], preferred_element_type=F32)
        acc_b = acc_b + jnp.dot(xb, w1_ref[(CMP_STRIDE + r) * NSA_HEAD_DIM:(CMP_STRIDE + r + 1) * NSA_HEAD_DIM, :],
                                preferred_element_type=F32)
    hidden = acc_a + jnp.concatenate([acc_b[1:], acc_b[:1]], axis=0)
    act = _gelu_tanh(hidden).astype(BF16)
    o_ref[...] = jnp.dot(act, w2_ref[...], preferred_element_type=F32)


def nsa_compress(r2, pe, w1, w2):
    b, s, _ = r2.shape
    nblk = s // CMP_STRIDE
    assert CMP_BLOCK == 2 * CMP_STRIDE
    return pl.pallas_call(
        functools.partial(_compress_kernel, nblk=nblk),
        grid=(b, NSA_KV_GROUPS, 2),
        in_specs=[
            pl.BlockSpec((None, s, LANES), lambda bi, g, c: (bi, 0, R2_KC + 2 * c + g)),
            pl.BlockSpec((None, CMP_BLOCK, NSA_HEAD_DIM), lambda bi, g, c: (c, 0, 0)),
            pl.BlockSpec((None, CMP_BLOCK * NSA_HEAD_DIM, CMP_HIDDEN), lambda bi, g, c: (c, 0, 0)),
            pl.BlockSpec((None, CMP_HIDDEN, NSA_HEAD_DIM), lambda bi, g, c: (c, 0, 0)),
        ],
        out_specs=pl.BlockSpec((None, None, None, nblk, NSA_HEAD_DIM), lambda bi, g, c: (bi, g, c, 0, 0)),
        out_shape=jax.ShapeDtypeStruct((b, NSA_KV_GROUPS, 2, nblk, NSA_HEAD_DIM), F32),
        compiler_params=_cparams("parallel", "parallel", "arbitrary"),
        name="nsa_compress",
    )(r2, pe, w1, w2)


def _nsa_attn_kernel(q_ref, ks_ref, vs_ref, kw_ref, vw_ref, kcvc_ref, gate_ref, biasc_ref, biasd_ref,
                     cover_ref, expand_ref, o_ref, sel_ref, m_ref, l_ref, acc_ref, res_ref, *, n_sel, top_n):
    g = pl.program_id(1)
    i = pl.program_id(2)
    rows = NSA_HPG * TQ
    scale = NSA_HEAD_DIM ** -0.5
    nt = (((1,), (1,)), ((), ()))

    qf = q_ref[...]
    qs = jnp.concatenate([qf[:, j * NSA_HEAD_DIM:(j + 1) * NSA_HEAD_DIM] for j in range(NSA_HPG)], axis=0)
    row_t = lax.broadcasted_iota(jnp.int32, (TQ, TK), 0)
    lane = lax.broadcasted_iota(jnp.int32, (TQ, TK), 1)
    t_abs = i * TQ + row_t

    sg = jax.nn.sigmoid(gate_ref[...])

    def gate_col(c):
        cols = []
        for j in range(NSA_HPG):
            c0 = SMALL_GATE + j * 3 + c
            c1 = SMALL_GATE + (NSA_HPG + j) * 3 + c
            cols.append(jnp.where(g == 0, sg[:, c0:c0 + 1], sg[:, c1:c1 + 1]))
        return jnp.concatenate(cols, axis=0)

    kc = kcvc_ref[0].astype(BF16)
    vc = kcvc_ref[1].astype(BF16)
    s3 = (lax.dot_general(qs, kc, nt, preferred_element_type=F32) * scale).reshape(NSA_HPG, TQ, TK) + biasc_ref[...]
    valid_c = (lane * CMP_STRIDE + (CMP_BLOCK - 1)) <= t_abs
    s3 = jnp.where(valid_c[None], s3, NEG_INF)
    mx = jnp.max(s3, axis=-1, keepdims=True)
    mx = jnp.where(mx == NEG_INF, 0.0, mx)
    e3 = jnp.exp(s3 - mx)
    p3 = e3 / jnp.maximum(jnp.sum(e3, axis=-1, keepdims=True), 1e-30)
    o_c = jnp.dot(p3.reshape(rows, TK).astype(BF16), vc, preferred_element_type=F32)
    res_ref[...] = gate_col(0) * o_c

    psum = p3[0]
    for j in range(1, NSA_HPG):
        psum = psum + p3[j]
    imp = jnp.dot(psum, cover_ref[...], preferred_element_type=F32, precision=HIGHEST)
    cur = t_abs // SEL_BLOCK
    causal_blk = lane <= cur
    forced = (lane == 0) | (lane == cur) | (lane == cur - 1)
    score = jnp.where(causal_blk, imp + jnp.where(forced, SEL_FORCE, 0.0), NEG_INF)
    rank = jnp.zeros((TQ, TK), jnp.int32)
    for mp in range(n_sel):
        col = score[:, mp:mp + 1]
        beats = (col > score) | ((col == score) & (lane > mp))
        rank = rank + beats.astype(jnp.int32)
    sel = jnp.where((rank < top_n) & causal_blk, 1.0, 0.0).astype(BF16)
    selmask = jnp.dot(sel, expand_ref[...], preferred_element_type=F32)
    for k in range(sel_ref.shape[0]):
        sel_ref[k] = selmask[:, k * TK:(k + 1) * TK]

    def reset():
        m_ref[...] = jnp.full(m_ref.shape, NEG_INF, F32)
        l_ref[...] = jnp.zeros(l_ref.shape, F32)
        acc_ref[...] = jnp.zeros(acc_ref.shape, F32)

    def attend(k_tile, v_tile, variant, valid):
        s = lax.dot_general(qs, k_tile, nt, preferred_element_type=F32) * scale
        s = s.reshape(NSA_HPG, TQ, TK) + biasd_ref[:, variant]
        if valid is not None:
            s = jnp.where(valid[None], s, NEG_INF)
        s = s.reshape(rows, TK)
        m_prev = m_ref[...]
        m_new = jnp.maximum(m_prev, jnp.max(s, axis=-1, keepdims=True))
        m_safe = jnp.where(m_new == NEG_INF, 0.0, m_new)
        alpha = jnp.exp(m_prev - m_safe)
        p = jnp.exp(s - m_safe)
        l_ref[...] = alpha * l_ref[...] + jnp.sum(p, axis=-1, keepdims=True)
        acc_ref[...] = alpha * acc_ref[...] + jnp.dot(p.astype(BF16), v_tile, preferred_element_type=F32)
        m_ref[...] = m_new

    def finish():
        return acc_ref[...] / jnp.maximum(l_ref[...], 1e-30)

    def tile(ref, k):
        return ref[pl.ds(pl.multiple_of(k * TK, TK), TK), :]

    causal = lane <= row_t

    reset()

    def sel_far(k, carry):
        attend(tile(ks_ref, k), tile(vs_ref, k), 2, sel_ref[k] > 0.5)
        return carry

    lax.fori_loop(0, jnp.maximum(i - 1, 0), sel_far, 0)

    @pl.when(i >= 1)
    def _():
        attend(tile(ks_ref, i - 1), tile(vs_ref, i - 1), 1, sel_ref[i - 1] > 0.5)

    attend(tile(ks_ref, i), tile(vs_ref, i), 0, (sel_ref[i] > 0.5) & causal)
    res_ref[...] += gate_col(1) * finish()

    reset()
    n_win = WINDOW // TK
    for d in range(n_win, 0, -1):
        @pl.when(i >= d)
        def _(d=d):
            valid = (lane > row_t) if d == n_win else None
            attend(tile(kw_ref, i - d), tile(vw_ref, i - d), min(d, 2), valid)

    attend(tile(kw_ref, i), tile(vw_ref, i), 0, causal)
    res = res_ref[...] + gate_col(2) * finish()
    for j in range(NSA_HPG):
        o_ref[:, j * NSA_HEAD_DIM:(j + 1) * NSA_HEAD_DIM] = res[j * TQ:(j + 1) * TQ].astype(o_ref.dtype)


def nsa_attention(r1, r2, kcvc, bias_c, bias_d):
    b, s, _ = r1.shape
    n_cmp_rows = s // CMP_STRIDE
    n_sel = s // SEL_BLOCK
    assert n_cmp_rows == TK and n_sel <= LANES and WINDOW % TK == 0 and TQ == TK and 2 * TK > REL_MAX_DIST + TQ - 1 - 1
    top_n = min(SEL_TOP_N, n_sel)
    nq = s // TQ
    gw = NSA_HPG * NSA_HEAD_DIM

    n_cmp = n_cmp_rows - CMP_BLOCK // CMP_STRIDE + 1
    c_start = np.arange(n_cmp_rows) * CMP_STRIDE
    s_start = np.arange(LANES) * SEL_BLOCK
    cover = ((c_start[:, None] <= s_start[None, :] + SEL_BLOCK - 1) & (c_start[:, None] + CMP_BLOCK - 1 >= s_start[None, :])
             & (np.arange(n_cmp_rows)[:, None] < n_cmp) & (np.arange(LANES)[None, :] < n_sel))
    cover = jnp.asarray(cover.astype(np.float32))
    expand = jnp.asarray((np.arange(LANES)[:, None] == (np.arange(s)[None, :] // SEL_BLOCK)).astype(np.float32), BF16)

    kv_spec = lambda blk: pl.BlockSpec((None, s, LANES), lambda bi, g, i, blk=blk: (bi, 0, blk + g))
    return pl.pallas_call(
        functools.partial(_nsa_attn_kernel, n_sel=n_sel, top_n=top_n),
        grid=(b, NSA_KV_GROUPS, nq),
        in_specs=[
            pl.BlockSpec((None, TQ, gw), lambda bi, g, i: (bi, i, g)),
            kv_spec(R1_KS), kv_spec(R1_VS), kv_spec(R1_KW), kv_spec(R1_VW),
            pl.BlockSpec((None, None, 2, n_cmp_rows, NSA_HEAD_DIM), lambda bi, g, i: (bi, g, 0, 0, 0)),
            pl.BlockSpec((None, TQ, LANES), lambda bi, g, i: (bi, i, R2_SMALL)),
            pl.BlockSpec((NSA_HPG, TQ, LANES), lambda bi, g, i: (g, i, 0)),
            pl.BlockSpec((NSA_HPG, 3, TQ, TK), lambda bi, g, i: (g, 0, 0, 0)),
            pl.BlockSpec((n_cmp_rows, LANES), lambda bi, g, i: (0, 0)),
            pl.BlockSpec((LANES, s), lambda bi, g, i: (0, 0)),
        ],
        out_specs=pl.BlockSpec((None, TQ, gw), lambda bi, g, i: (bi, i, g)),
        out_shape=jax.ShapeDtypeStruct((b, s, NSA_WIDTH), BF16),
        scratch_shapes=[
            pltpu.VMEM((s // TK, TQ, TK), F32),
            pltpu.VMEM((NSA_HPG * TQ, 1), F32),
            pltpu.VMEM((NSA_HPG * TQ, 1), F32),
            pltpu.VMEM((NSA_HPG * TQ, NSA_HEAD_DIM), F32),
            pltpu.VMEM((NSA_HPG * TQ, NSA_HEAD_DIM), F32),
        ],
        compiler_params=_cparams("parallel", "parallel", "arbitrary"),
        name="nsa_attention",
    )(r1, r1, r1, r1, r1, kcvc, r2, bias_c, bias_d, cover, expand)


GDN_PAIR = 2 * GDN_CHUNK
GDN_UNROLL = 4
GDN_ROWS = 512


def _gdn_prep_kernel(q_ref, k_ref, v_ref, smt_ref, alog_ref, dtb_ref, cwq_ref, cwk_ref, cwv_ref,
                     w_out, qg_out, u_out, qk_out, kdt_out, egl_out,
                     kb_s, kk_s, qq_s, vk_s, kd_s, gcol_s, grow_s):
    h = pl.program_id(1)
    s = q_ref.shape[0]
    c = GDN_CHUNK
    pr = GDN_PAIR
    dk = GDN_HEAD_DIM
    nt = (((1,), (1,)), ((), ()))
    tn = (((0,), (0,)), ((), ()))
    row = lax.broadcasted_iota(jnp.int32, (s, dk), 0)

    xg = smt_ref[0:GDN_HEADS, :] + dtb_ref[...]
    g8 = -jnp.exp(alog_ref[...]) * (jnp.maximum(xg, 0.0) + jnp.log1p(jnp.exp(-jnp.abs(xg))))
    beta8 = jax.nn.sigmoid(smt_ref[GDN_HEADS:2 * GDN_HEADS, :])
    pos = lax.broadcasted_iota(jnp.int32, g8.shape, 1) % c
    gcum8 = g8
    sh = 1
    while sh < c:
        gcum8 = gcum8 + jnp.where(pos >= sh, pltpu.roll(gcum8, sh, axis=1), 0.0)
        sh *= 2
    for p in range(s // pr):
        grow_s[p] = gcum8[:, p * pr:(p + 1) * pr]
    sel_k = lax.broadcasted_iota(jnp.int32, (2 * GDN_HEADS, 2 * dk), 0)
    sel_l = lax.broadcasted_iota(jnp.int32, (2 * GDN_HEADS, 2 * dk), 1)
    pick = jnp.where(((sel_k == h) & (sel_l < dk)) | ((sel_k == GDN_HEADS + h) & (sel_l >= dk)), 1.0, 0.0)
    col = lax.dot_general(jnp.concatenate([gcum8, beta8], axis=0), pick, tn, preferred_element_type=F32,
                          precision=HIGHEST)
    gcol = col[:, :dk]
    bcol = col[:, dk:]
    gcol_s[...] = gcol

    def conv_silu(x_ref, w_ref):
        x = x_ref[...]
        w = w_ref[...]
        y = x * w[CONV_WIDTH - 1:CONV_WIDTH, :]
        for sft in range(1, CONV_WIDTH):
            xs = jnp.where(row >= sft, pltpu.roll(x, sft, axis=0), 0.0)
            y = y + xs * w[CONV_WIDTH - 1 - sft:CONV_WIDTH - sft, :]
        return y * jax.nn.sigmoid(y)

    def l2n(x):
        return x * lax.rsqrt(jnp.sum(x * x, axis=-1, keepdims=True) + NORM_EPS)

    q = l2n(conv_silu(q_ref, cwq_ref)) * (dk ** -0.5)
    k = l2n(conv_silu(k_ref, cwk_ref))
    v = conv_silu(v_ref, cwv_ref)
    eg = jnp.exp(gcol)
    kb = k * bcol
    qg_out[...] = (q * eg).astype(BF16)
    kk_s[...] = k.astype(BF16)
    kb_s[...] = kb.astype(BF16)
    qq_s[...] = q.astype(BF16)
    vk_s[:, :dk] = (v * bcol).astype(BF16)
    vk_s[:, dk:] = (kb * eg).astype(BF16)
    g3 = gcol.reshape(s // c, c, dk)
    g_last = g3[:, c - 1:c, :]
    kd_s[...] = (k * jnp.exp(jnp.broadcast_to(g_last, g3.shape).reshape(s, dk) - gcol)).astype(BF16)
    egl_out[...] = jnp.broadcast_to(jnp.exp(g_last), egl_out.shape)

    r_i = lax.broadcasted_iota(jnp.int32, (pr, pr), 0)
    l_i = lax.broadcasted_iota(jnp.int32, (pr, pr), 1)
    tri = ((r_i // c) == (l_i // c)) & (r_i >= l_i)
    eye = r_i == l_i
    eye_f = jnp.where(eye, 1.0, 0.0)
    eye_b = eye_f.astype(BF16)

    def pair(p):
        rows = pl.ds(pl.multiple_of(p * pr, pr), pr)
        x = lax.dot_general(jnp.concatenate([kb_s[rows, :], qq_s[rows, :]], axis=0), kk_s[rows, :], nt,
                            preferred_element_type=F32)
        diff = gcol_s[rows, :] - grow_s[p, pl.ds(h, 1), :]
        decay = jnp.where(tri, jnp.exp(jnp.where(tri, diff, 0.0)), 0.0)
        qk_out[rows, :] = (x[pr:] * decay).astype(BF16)
        qm = jnp.where(eye, 0.0, -(x[:pr] * decay))
        tinv = eye_f + qm
        qb = qm.astype(BF16)
        qm = jnp.dot(qb, qb, preferred_element_type=F32)
        steps = int(math.log2(c)) - 1
        for it in range(steps):
            qb = qm.astype(BF16)
            if it < steps - 1:
                r = jnp.dot(jnp.concatenate([qb, tinv.astype(BF16)], axis=0), qb, preferred_element_type=F32)
                qm = r[:pr]
                tinv = tinv + r[pr:]
            else:
                tinv = tinv + jnp.dot(tinv.astype(BF16), qb, preferred_element_type=F32)
        uw = jnp.dot(tinv.astype(BF16), vk_s[rows, :], preferred_element_type=F32)
        u_out[rows, :] = uw[:, :dk]
        w_out[rows, :] = uw[:, dk:].astype(BF16)
        kdt_out[p] = lax.dot_general(eye_b, kd_s[rows, :], nt, preferred_element_type=F32).astype(BF16)

    def body(it, carry):
        for uu in range(GDN_UNROLL):
            pair(it * GDN_UNROLL + uu)
        return carry

    lax.fori_loop(0, s // pr // GDN_UNROLL, body, 0)


def _gdn_scan_kernel(w_ref, qg_ref, u_ref, qk_ref, kdt_ref, egl_ref, z_ref, nw_ref, o_ref, state_ref, os_ref):
    c = GDN_CHUNK
    pr = GDN_PAIR
    dk = GDN_HEAD_DIM
    nh = w_ref.shape[0]
    rb = w_ref.shape[1]

    @pl.when(pl.program_id(1) == 0)
    def _():
        state_ref[...] = jnp.zeros(state_ref.shape, F32)

    zeros = jnp.zeros((c, dk), BF16)

    def pair(p, carry):
        r0 = pl.multiple_of(p * pr, pr)
        for half in range(2):
            rows = pl.ds(r0 + half * c, c)
            for h in range(nh):
                state = state_ref[h]
                r1 = jnp.dot(jnp.concatenate([w_ref[h, rows, :], qg_ref[h, rows, :]], axis=0), state.astype(BF16),
                             preferred_element_type=F32)
                vnb = (u_ref[h, rows, :] - r1[:c]).astype(BF16)
                rhs = jnp.concatenate([vnb, zeros] if half == 0 else [zeros, vnb], axis=0)
                r2 = jnp.dot(jnp.concatenate([qk_ref[h, rows, :], kdt_ref[h, p]], axis=0), rhs,
                             preferred_element_type=F32)
                os_ref[rows, h * dk:(h + 1) * dk] = r1[c:] + r2[:c]
                state_ref[h] = state * egl_ref[h, 2 * p + half, 0:1, :] + r2[c:]
        return carry

    lax.fori_loop(0, rb // pr, pair, 0)

    for h in range(nh):
        o = os_ref[:, h * dk:(h + 1) * dk]
        o = o * lax.rsqrt(jnp.mean(o * o, axis=-1, keepdims=True) + NORM_EPS) * nw_ref[...]
        z = z_ref[:, h * dk:(h + 1) * dk]
        o_ref[:, h * dk:(h + 1) * dk] = (o * (z * jax.nn.sigmoid(z))).astype(o_ref.dtype)


def gated_deltanet(r2, conv_w, a_log, dt_bias, norm_w):
    b, s, _ = r2.shape
    dk = GDN_HEAD_DIM
    nh = GDN_HEADS
    npair = s // GDN_PAIR
    nchunk = s // GDN_CHUNK
    rb = GDN_ROWS
    assert s % (GDN_PAIR * GDN_UNROLL) == 0 and s % rb == 0 and (R2_Z * LANES) % GDN_WIDTH == 0
    a0 = R2_SMALL * LANES + SMALL_A
    smt = jnp.swapaxes(r2[:, :, a0:a0 + 2 * nh], 1, 2)
    col = lambda blk: pl.BlockSpec((None, s, dk), lambda bi, h, blk=blk: (bi, 0, blk + h))
    cw = lambda blk: pl.BlockSpec((CONV_WIDTH, dk), lambda bi, h, blk=blk: (0, blk + h))
    vec = pl.BlockSpec((nh, 1), lambda bi, h: (0, 0))
    per_head = lambda *shape: pl.BlockSpec((None, None) + shape, lambda bi, h: (bi, h) + (0,) * len(shape))
    w, qg, u, qk, kdt, egl = pl.pallas_call(
        _gdn_prep_kernel,
        grid=(b, nh),
        in_specs=[col(R2_GQ), col(R2_GK), col(R2_GV),
                  pl.BlockSpec((None, 2 * nh, s), lambda bi, h: (bi, 0, 0)), vec, vec,
                  cw(0), cw(nh), cw(2 * nh)],
        out_specs=[per_head(s, dk), per_head(s, dk), per_head(s, dk), per_head(s, dk),
                   per_head(npair, GDN_PAIR, GDN_PAIR), per_head(nchunk, 8, dk)],
        out_shape=[jax.ShapeDtypeStruct((b, nh, s, dk), BF16), jax.ShapeDtypeStruct((b, nh, s, dk), BF16),
                   jax.ShapeDtypeStruct((b, nh, s, dk), F32), jax.ShapeDtypeStruct((b, nh, s, dk), BF16),
                   jax.ShapeDtypeStruct((b, nh, npair, GDN_PAIR, GDN_PAIR), BF16),
                   jax.ShapeDtypeStruct((b, nh, nchunk, 8, dk), F32)],
        scratch_shapes=[pltpu.VMEM((s, dk), BF16), pltpu.VMEM((s, dk), BF16), pltpu.VMEM((s, dk), BF16),
                        pltpu.VMEM((s, 2 * dk), BF16), pltpu.VMEM((s, dk), BF16), pltpu.VMEM((s, dk), F32),
                        pltpu.VMEM((npair, nh, GDN_PAIR), F32)],
        compiler_params=_cparams("parallel", "parallel"),
        name="gdn_prep",
    )(r2, r2, r2, smt, a_log.reshape(nh, 1), dt_bias.reshape(nh, 1), conv_w, conv_w, conv_w)

    seq = lambda *shape: pl.BlockSpec((None, nh) + shape, lambda bi, j: (bi, 0, j) + (0,) * (len(shape) - 1))
    return pl.pallas_call(
        _gdn_scan_kernel,
        grid=(b, s // rb),
        in_specs=[seq(rb, dk), seq(rb, dk), seq(rb, dk), seq(rb, dk),
                  seq(rb // GDN_PAIR, GDN_PAIR, GDN_PAIR), seq(rb // GDN_CHUNK, 8, dk),
                  pl.BlockSpec((None, rb, GDN_WIDTH), lambda bi, j: (bi, j, R2_Z * LANES // GDN_WIDTH)),
                  pl.BlockSpec((1, dk), lambda bi, j: (0, 0))],
        out_specs=pl.BlockSpec((None, rb, GDN_WIDTH), lambda bi, j: (bi, j, 0)),
        out_shape=jax.ShapeDtypeStruct((b, s, GDN_WIDTH), BF16),
        scratch_shapes=[pltpu.VMEM((nh, dk, dk), F32), pltpu.VMEM((rb, GDN_WIDTH), F32)],
        compiler_params=_cparams("parallel", "arbitrary"),
        name="gdn_scan",
    )(w, qg, u, qk, kdt, egl, r2, norm_w.reshape(1, dk))


def _split_w_in(w_in):
    o_q = 0
    o_kv = NSA_WIDTH
    o_gate = o_kv + 6 * NSA_KV_WIDTH
    o_qkv = o_gate + 3 * NSA_HEADS
    o_z = o_qkv + 3 * GDN_WIDTH
    o_a = o_z + GDN_WIDTH
    o_b = o_a + GDN_HEADS
    o_m = o_b + GDN_HEADS
    kvw = NSA_KV_WIDTH
    w1 = jnp.concatenate([w_in[:, o_q:o_kv], w_in[:, o_kv + 2 * kvw:o_kv + 6 * kvw]], axis=1)
    small = jnp.concatenate([w_in[:, o_gate:o_qkv], w_in[:, o_a:o_m]], axis=1)
    used = (R2_SMALL * LANES) + small.shape[1]
    w2 = jnp.concatenate([w_in[:, o_z:o_a], w_in[:, o_kv:o_kv + 2 * kvw], w_in[:, o_qkv:o_z], w_in[:, o_m:], small,
                          jnp.zeros((w_in.shape[0], R2_COLS - used), w_in.dtype)], axis=1)
    return w1.astype(BF16), w2.astype(BF16)


def _layer(x, bias_c, bias_d, ln1_w, w_in, pe_k, pe_v, w1_k, w2_k, w1_v, w2_v, conv_w, a_log, dt_bias,
           gdn_norm_w, w_pa, w_pb, w_o, ln2_w, w_up, w_down, final_norm_w, b, s):
    t = b * s
    w_r1, w_r2 = _split_w_in(w_in)
    h = rmsnorm(x, ln1_w, BF16)
    r1 = matmul(h, w_r1, BF16, tm=1024, tn=1024, tk=D_MODEL)
    r2 = matmul(h, w_r2, F32, tm=1024, tn=1024, tk=D_MODEL)
    r1 = r1.reshape(b, s, R1_COLS)
    r2 = r2.reshape(b, s, R2_COLS)
    kcvc = nsa_compress(r2, jnp.stack([pe_k, pe_v]), jnp.stack([w1_k, w1_v]).astype(BF16),
                        jnp.stack([w2_k, w2_v]).astype(BF16))
    o_a = nsa_attention(r1, r2, kcvc, bias_c, bias_d).reshape(t, NSA_WIDTH)
    o_b = gated_deltanet(r2, conv_w, a_log, dt_bias, gdn_norm_w).reshape(t, GDN_WIDTH)
    m = merge_branches(o_a, o_b, w_pa.astype(BF16), w_pb.astype(BF16), r2.reshape(t, R2_COLS))
    x = matmul(m, w_o.astype(BF16), F32, tm=1024, tn=1024, tk=D_MODEL, epilogue="residual", res=x)
    h2 = rmsnorm(x, ln2_w, BF16)
    u = matmul(h2, w_up.astype(BF16), BF16, tm=1024, tn=1024, tk=D_MODEL, epilogue="relu2")
    if final_norm_w is None:
        return matmul(u, w_down.astype(BF16), F32, tm=512, tn=D_MODEL, tk=1024, epilogue="residual", res=x)
    return matmul(u, w_down.astype(BF16), F32, tm=512, tn=D_MODEL, tk=1024, epilogue="residual_norm", res=x,
                  norm_w=final_norm_w)


def kernel(x, rel_table, ln1_w, w_in, cmp_pe_k, cmp_pe_v, cmp_w1_k, cmp_w2_k, cmp_w1_v, cmp_w2_v, conv_w, a_log,
           dt_bias, gdn_norm_w, w_pa, w_pb, w_o, ln2_w, w_up, w_down, ln_f_w):
    b, s, d = x.shape
    depth = w_in.shape[0]
    bias_c, bias_d = bias_tables(rel_table, s)
    xt = x.reshape(b * s, d)
    for l in range(depth):
        xt = _layer(xt, bias_c, bias_d, ln1_w[l], w_in[l], cmp_pe_k[l], cmp_pe_v[l], cmp_w1_k[l], cmp_w2_k[l],
                    cmp_w1_v[l], cmp_w2_v[l], conv_w[l], a_log[l], dt_bias[l], gdn_norm_w[l], w_pa[l], w_pb[l],
                    w_o[l], ln2_w[l], w_up[l], w_down[l], ln_f_w if l == depth - 1 else None, b, s)
    return xt.reshape(b, s, d)
```

```python
import functools
import math

import numpy as np
import jax
import jax.numpy as jnp
from jax import lax
from jax.experimental import pallas as pl
from jax.experimental.pallas import tpu as pltpu

F32 = jnp.float32
BF16 = jnp.bfloat16
HIGHEST = lax.Precision.HIGHEST

D_MODEL = 2048
DEPTH = 2
NSA_HEADS = 8
NSA_KV_GROUPS = 2
NSA_HPG = NSA_HEADS // NSA_KV_GROUPS
NSA_HEAD_DIM = 128
NSA_WIDTH = NSA_HEADS * NSA_HEAD_DIM
NSA_KV_WIDTH = NSA_KV_GROUPS * NSA_HEAD_DIM
CMP_BLOCK = 32
CMP_STRIDE = 16
CMP_HIDDEN = 256
SEL_BLOCK = 64
SEL_TOP_N = 16
SEL_FORCE = 1000.0
WINDOW = 512
GDN_HEADS = 8
GDN_HEAD_DIM = 128
GDN_WIDTH = GDN_HEADS * GDN_HEAD_DIM
CONV_WIDTH = 4
GDN_CHUNK = 64
REL_BUCKETS = 32
REL_MAX_DIST = 128
D_FF = 4 * D_MODEL
NORM_EPS = 1e-6

LANES = 128
VMEM_LIMIT = 56 * 1024 * 1024
TQ = 128
TK = 128
NEG_INF = float("-inf")

R1_COLS = 2048
R1_KS, R1_VS, R1_KW, R1_VW = 8, 10, 12, 14
R2_Z, R2_KC, R2_VC, R2_GQ, R2_GK, R2_GV, R2_MA, R2_MB, R2_SMALL = 0, 8, 10, 12, 20, 28, 36, 52, 68
R2_COLS = 72 * LANES
SMALL_GATE, SMALL_A, SMALL_B = 0, 24, 32


def _t5_thresholds():
    exact = REL_BUCKETS // 2
    n = np.arange(exact, REL_MAX_DIST * 2, dtype=np.int64)
    nf = n.astype(np.float32)
    large = exact + (np.log(nf / np.float32(exact)) / np.float32(math.log(REL_MAX_DIST / exact))
                     * np.float32(REL_BUCKETS - exact)).astype(np.int32)
    large = np.minimum(large, REL_BUCKETS - 1)
    thr = []
    for b in range(exact + 1, REL_BUCKETS):
        thr.append(int(n[np.argmax(large >= b)]))
    return tuple(thr)


T5_THRESHOLDS = _t5_thresholds()


def _cparams(*sem):
    return pltpu.CompilerParams(dimension_semantics=sem, vmem_limit_bytes=VMEM_LIMIT)


def _rmsnorm_kernel(x_ref, w_ref, o_ref):
    x = x_ref[...]
    y = x * lax.rsqrt(jnp.mean(x * x, axis=-1, keepdims=True) + NORM_EPS)
    o_ref[...] = (y * w_ref[...]).astype(o_ref.dtype)


def rmsnorm(x, w, out_dtype, tm=512):
    t, d = x.shape
    return pl.pallas_call(
        _rmsnorm_kernel,
        grid=(t // tm,),
        in_specs=[pl.BlockSpec((tm, d), lambda i: (i, 0)), pl.BlockSpec((1, d), lambda i: (0, 0))],
        out_specs=pl.BlockSpec((tm, d), lambda i: (i, 0)),
        out_shape=jax.ShapeDtypeStruct((t, d), out_dtype),
        compiler_params=_cparams("parallel"),
        name="rmsnorm",
    )(x, w.reshape(1, d))


def _mm_kernel(*refs, nk, epilogue):
    a_ref, w_ref = refs[0], refs[1]
    pos = 2
    res_ref = norm_ref = None
    if epilogue in ("residual", "residual_norm"):
        res_ref = refs[pos]
        pos += 1
    if epilogue == "residual_norm":
        norm_ref = refs[pos]
        pos += 1
    o_ref = refs[pos]
    acc_ref = refs[pos + 1] if nk > 1 else None

    def finish(acc):
        if epilogue == "relu2":
            acc = jnp.square(jnp.maximum(acc, 0.0))
        elif epilogue == "residual":
            acc = res_ref[...] + acc
        elif epilogue == "residual_norm":
            y = res_ref[...] + acc
            acc = y * lax.rsqrt(jnp.mean(y * y, axis=-1, keepdims=True) + NORM_EPS) * norm_ref[...]
        o_ref[...] = acc.astype(o_ref.dtype)

    part = jnp.dot(a_ref[...], w_ref[...], preferred_element_type=F32)
    if nk == 1:
        finish(part)
    else:
        k = pl.program_id(2)

        @pl.when(k == 0)
        def _():
            acc_ref[...] = part

        @pl.when(k > 0)
        def _():
            acc_ref[...] += part

        @pl.when(k == nk - 1)
        def _():
            finish(acc_ref[...])


def matmul(a, w, out_dtype, *, tm, tn, tk, epilogue="none", res=None, norm_w=None):
    m, kdim = a.shape
    n = w.shape[1]
    nk = kdim // tk
    in_specs = [pl.BlockSpec((tm, tk), lambda i, j, k: (i, k)), pl.BlockSpec((tk, tn), lambda i, j, k: (k, j))]
    args = [a, w]
    if epilogue in ("residual", "residual_norm"):
        in_specs.append(pl.BlockSpec((tm, tn), lambda i, j, k: (i, j)))
        args.append(res)
    if epilogue == "residual_norm":
        assert tn == n
        in_specs.append(pl.BlockSpec((1, tn), lambda i, j, k: (0, 0)))
        args.append(norm_w.reshape(1, n))
    scratch = [pltpu.VMEM((tm, tn), F32)] if nk > 1 else []
    return pl.pallas_call(
        functools.partial(_mm_kernel, nk=nk, epilogue=epilogue),
        grid=(m // tm, n // tn, nk),
        in_specs=in_specs,
        out_specs=pl.BlockSpec((tm, tn), lambda i, j, k: (i, j)),
        out_shape=jax.ShapeDtypeStruct((m, n), out_dtype),
        scratch_shapes=scratch,
        compiler_params=_cparams("parallel", "parallel", "arbitrary"),
        name="matmul_" + epilogue,
    )(*args)


def _merge_kernel(oa_ref, ob_ref, wa_ref, wb_ref, ma_ref, mb_ref, o_ref):
    pa = jnp.dot(oa_ref[...], wa_ref[...], preferred_element_type=F32)
    pb = jnp.dot(ob_ref[...], wb_ref[...], preferred_element_type=F32)
    o_ref[...] = (jax.nn.sigmoid(ma_ref[...]) * pa + jax.nn.sigmoid(mb_ref[...]) * pb).astype(o_ref.dtype)


def merge_branches(o_a, o_b, w_pa, w_pb, r2, *, tm=1024, tn=512):
    t, ka = o_a.shape
    kb = o_b.shape[1]
    n = w_pa.shape[1]
    ma0 = R2_MA * LANES // tn
    mb0 = R2_MB * LANES // tn
    return pl.pallas_call(
        _merge_kernel,
        grid=(t // tm, n // tn),
        in_specs=[
            pl.BlockSpec((tm, ka), lambda i, j: (i, 0)),
            pl.BlockSpec((tm, kb), lambda i, j: (i, 0)),
            pl.BlockSpec((ka, tn), lambda i, j: (0, j)),
            pl.BlockSpec((kb, tn), lambda i, j: (0, j)),
            pl.BlockSpec((tm, tn), lambda i, j: (i, ma0 + j)),
            pl.BlockSpec((tm, tn), lambda i, j: (i, mb0 + j)),
        ],
        out_specs=pl.BlockSpec((tm, tn), lambda i, j: (i, j)),
        out_shape=jax.ShapeDtypeStruct((t, n), BF16),
        compiler_params=_cparams("parallel", "parallel"),
        name="merge_branches",
    )(o_a, o_b, w_pa, w_pb, r2, r2)


def _t5_lookup(dist, tab_ref, h):
    n = jnp.maximum(dist, 0)
    exact = REL_BUCKETS // 2
    large = jnp.full(n.shape, exact, jnp.int32)
    for thr in T5_THRESHOLDS:
        large = large + (n >= thr).astype(jnp.int32)
    bucket = jnp.where(n < exact, n, large)
    out = jnp.zeros(n.shape, F32)
    for b in range(REL_BUCKETS):
        out = jnp.where(bucket == b, tab_ref[b, h], out)
    return out


def _bias_kernel(tab_ref, bc_ref, bd_ref):
    g = pl.program_id(0)
    i = pl.program_id(1)
    key = lax.broadcasted_iota(jnp.int32, (TK, TQ), 0)
    qry = lax.broadcasted_iota(jnp.int32, (TK, TQ), 1)
    for j in range(NSA_HPG):
        h = g * NSA_HPG + j
        bc_ref[:, j * TQ:(j + 1) * TQ] = _t5_lookup(i * TQ + qry - (key * CMP_STRIDE + CMP_BLOCK - 1), tab_ref, h)

    @pl.when(i == 0)
    def _():
        for v in range(3):
            for j in range(NSA_HPG):
                bd_ref[v, :, j * TQ:(j + 1) * TQ] = _t5_lookup(v * TK + qry - key, tab_ref, g * NSA_HPG + j)


def bias_tables(rel_table, s):
    nq = s // TQ
    cols = NSA_HPG * TQ
    return pl.pallas_call(
        _bias_kernel,
        grid=(NSA_KV_GROUPS, nq),
        in_specs=[pl.BlockSpec(memory_space=pltpu.SMEM)],
        out_specs=[pl.BlockSpec((None, None, TK, cols), lambda g, i: (g, i, 0, 0)),
                   pl.BlockSpec((None, 3, TK, cols), lambda g, i: (g, 0, 0, 0))],
        out_shape=[jax.ShapeDtypeStruct((NSA_KV_GROUPS, nq, TK, cols), F32),
                   jax.ShapeDtypeStruct((NSA_KV_GROUPS, 3, TK, cols), F32)],
        compiler_params=_cparams("arbitrary", "arbitrary"),
        name="t5_bias_tables",
    )(rel_table)


def _gelu_tanh(x):
    return 0.5 * x * (1.0 + jnp.tanh(math.sqrt(2.0 / math.pi) * (x + 0.044715 * (x * x * x))))


def _compress_kernel(x_ref, pe_ref, w1_ref, w2k_ref, w2vt_ref, o_ref, *, nblk):
    pe = pe_ref[...]
    acc_a = jnp.zeros((nblk, CMP_HIDDEN), F32)
    acc_b = jnp.zeros((nblk, CMP_HIDDEN), F32)
    for r in range(CMP_STRIDE):
        xr = x_ref[pl.ds(r, nblk, stride=CMP_STRIDE), :]
        xa = (xr + pe[r:r + 1, :]).astype(BF16)
        xb = (xr + pe[CMP_STRIDE + r:CMP_STRIDE + r + 1, :]).astype(BF16)
        acc_a = acc_a + jnp.dot(xa, w1_ref[r * NSA_HEAD_DIM:(r + 1) * NSA_HEAD_DIM, :], preferred_element_type=F32)
        acc_b = acc_b + jnp.dot(xb, w1_ref[(CMP_STRIDE + r) * NSA_HEAD_DIM:(CMP_STRIDE + r + 1) * NSA_HEAD_DIM, :],
                                preferred_element_type=F32)
    hidden = acc_a + jnp.concatenate([acc_b[1:], acc_b[:1]], axis=0)
    act = _gelu_tanh(hidden).astype(BF16)

    @pl.when(pl.program_id(2) == 0)
    def _():
        o_ref[...] = jnp.dot(act, w2k_ref[...], preferred_element_type=F32).astype(o_ref.dtype)

    @pl.when(pl.program_id(2) == 1)
    def _():
        o_ref[...] = lax.dot_general(w2vt_ref[...], act, (((1,), (1,)), ((), ())),
                                     preferred_element_type=F32).astype(o_ref.dtype)


def nsa_compress(r2, pe, w1, w2k, w2vt):
    b, s, _ = r2.shape
    nblk = s // CMP_STRIDE
    assert CMP_BLOCK == 2 * CMP_STRIDE and nblk == NSA_HEAD_DIM
    return pl.pallas_call(
        functools.partial(_compress_kernel, nblk=nblk),
        grid=(b, NSA_KV_GROUPS, 2),
        in_specs=[
            pl.BlockSpec((None, s, LANES), lambda bi, g, c: (bi, 0, R2_KC + 2 * c + g)),
            pl.BlockSpec((None, CMP_BLOCK, NSA_HEAD_DIM), lambda bi, g, c: (c, 0, 0)),
            pl.BlockSpec((None, CMP_BLOCK * NSA_HEAD_DIM, CMP_HIDDEN), lambda bi, g, c: (c, 0, 0)),
            pl.BlockSpec((CMP_HIDDEN, NSA_HEAD_DIM), lambda bi, g, c: (0, 0)),
            pl.BlockSpec((NSA_HEAD_DIM, CMP_HIDDEN), lambda bi, g, c: (0, 0)),
        ],
        out_specs=pl.BlockSpec((None, None, None, nblk, NSA_HEAD_DIM), lambda bi, g, c: (bi, g, c, 0, 0)),
        out_shape=jax.ShapeDtypeStruct((b, NSA_KV_GROUPS, 2, nblk, NSA_HEAD_DIM), BF16),
        compiler_params=_cparams("parallel", "parallel", "arbitrary"),
        name="nsa_compress",
    )(r2, pe, w1, w2k, w2vt)


def _nsa_attn_kernel(q_ref, ks_ref, kw_ref, vst_ref, vwt_ref, kcvc_ref, gate_ref, biasc_ref, biasd_ref,
                     covert_ref, expandt_ref, o_ref, sel_ref, m_ref, l_ref, acc_ref, res_ref, *, n_sel, top_n):
    g = pl.program_id(1)
    i = pl.program_id(2)
    cols = NSA_HPG * TQ
    scale = NSA_HEAD_DIM ** -0.5
    nt = (((1,), (1,)), ((), ()))

    qf = q_ref[...]
    qs = jnp.concatenate([qf[:, j * NSA_HEAD_DIM:(j + 1) * NSA_HEAD_DIM] for j in range(NSA_HPG)], axis=0)
    key = lax.broadcasted_iota(jnp.int32, (TK, cols), 0)
    t_loc = lax.broadcasted_iota(jnp.int32, (TK, cols), 1) % TQ

    def gate_row(c):
        return jnp.concatenate(
            [jax.nn.sigmoid(gate_ref[pl.ds((g * NSA_HPG + j) * 3 + c, 1), :]) for j in range(NSA_HPG)], axis=1)

    def lanes4(x):
        return jnp.concatenate([x] * NSA_HPG, axis=1)

    s = lax.dot_general(kcvc_ref[0], qs, nt, preferred_element_type=F32) * scale + biasc_ref[...]
    s = jnp.where(key * CMP_STRIDE + (CMP_BLOCK - 1) <= i * TQ + t_loc, s, NEG_INF)
    mx = jnp.max(s, axis=0, keepdims=True)
    mx = jnp.where(mx == NEG_INF, 0.0, mx)
    e = jnp.exp(s - mx)
    p = e / jnp.maximum(jnp.sum(e, axis=0, keepdims=True), 1e-30)
    o_c = jnp.dot(kcvc_ref[1], p.astype(BF16), preferred_element_type=F32)
    res_ref[...] = gate_row(0) * o_c

    psum = p[:, 0:TQ]
    for j in range(1, NSA_HPG):
        psum = psum + p[:, j * TQ:(j + 1) * TQ]
    imp = jnp.dot(covert_ref[...], psum, preferred_element_type=F32, precision=HIGHEST)[:n_sel]
    blk = lax.broadcasted_iota(jnp.int32, (n_sel, TQ), 0)
    cur = (i * TQ + lax.broadcasted_iota(jnp.int32, (n_sel, TQ), 1)) // SEL_BLOCK
    causal_blk = blk <= cur
    forced = (blk == 0) | (blk == cur) | (blk == cur - 1)
    score = jnp.where(causal_blk, imp + jnp.where(forced, SEL_FORCE, 0.0), NEG_INF)
    rank = jnp.zeros((n_sel, TQ), jnp.int32)
    for mp in range(n_sel):
        other = score[mp:mp + 1, :]
        beats = (other > score) | ((other == score) & (blk > mp))
        rank = rank + beats.astype(jnp.int32)
    sel = jnp.where((rank < top_n) & causal_blk, 1.0, 0.0)
    sel = jnp.concatenate([sel, jnp.zeros((LANES - n_sel, TQ), F32)], axis=0).astype(BF16)
    sel_ref[...] = jnp.where(jnp.dot(expandt_ref[...], sel, preferred_element_type=F32) > 0.5, 0.0, NEG_INF)

    def reset():
        m_ref[...] = jnp.full(m_ref.shape, NEG_INF, F32)
        l_ref[...] = jnp.zeros(l_ref.shape, F32)
        acc_ref[...] = jnp.zeros(acc_ref.shape, F32)

    def attend(k_tile, vt_tile, variant, mask_add):
        s = lax.dot_general(k_tile, qs, nt, preferred_element_type=F32) * scale + biasd_ref[variant]
        if mask_add is not None:
            s = s + mask_add
        m_prev = m_ref[...]
        m_new = jnp.maximum(m_prev, jnp.max(s, axis=0, keepdims=True))
        m_safe = jnp.where(m_new == NEG_INF, 0.0, m_new)
        alpha = jnp.exp(m_prev - m_safe)
        p = jnp.exp(s - m_safe)
        l_ref[...] = alpha * l_ref[...] + jnp.sum(p, axis=0, keepdims=True)
        acc_ref[...] = alpha * acc_ref[...] + jnp.dot(vt_tile, p.astype(BF16), preferred_element_type=F32)
        m_ref[...] = m_new

    def finish():
        return acc_ref[...] * (1.0 / jnp.maximum(l_ref[...], 1e-30))

    def tile(ref, k):
        return ref[pl.ds(pl.multiple_of(k * TK, TK), TK), :]

    causal_add = jnp.where(key <= t_loc, 0.0, NEG_INF)

    reset()

    def sel_far(k, carry):
        attend(tile(ks_ref, k), vst_ref[k], 2, lanes4(tile(sel_ref, k)))
        return carry

    lax.fori_loop(0, jnp.maximum(i - 1, 0), sel_far, 0)

    @pl.when(i >= 1)
    def _():
        attend(tile(ks_ref, i - 1), vst_ref[i - 1], 1, lanes4(tile(sel_ref, i - 1)))

    attend(tile(ks_ref, i), vst_ref[i], 0, lanes4(tile(sel_ref, i)) + causal_add)
    res_ref[...] += gate_row(1) * finish()

    reset()
    n_win = WINDOW // TK
    for d in range(n_win, 0, -1):
        @pl.when(i >= d)
        def _(d=d):
            edge_add = jnp.where(key > t_loc, 0.0, NEG_INF) if d == n_win else None
            attend(tile(kw_ref, i - d), vwt_ref[i - d], min(d, 2), edge_add)

    attend(tile(kw_ref, i), vwt_ref[i], 0, causal_add)
    res = res_ref[...] + gate_row(2) * finish()
    for j in range(NSA_HPG):
        o_ref[:, j * NSA_HEAD_DIM:(j + 1) * NSA_HEAD_DIM] = res[:, j * TQ:(j + 1) * TQ].T.astype(o_ref.dtype)


def nsa_attention(r1, r2, kcvc, bias_c, bias_d):
    b, s, _ = r1.shape
    n_cmp_rows = s // CMP_STRIDE
    n_sel = s // SEL_BLOCK
    assert n_cmp_rows == TK and n_sel <= LANES and n_sel % 8 == 0 and WINDOW % TK == 0 and TQ == TK
    assert 2 * TK - (TQ - 1) >= REL_MAX_DIST
    top_n = min(SEL_TOP_N, n_sel)
    nq = s // TQ
    nk = s // TK
    gw = NSA_HPG * NSA_HEAD_DIM
    cols = NSA_HPG * TQ
    dh = NSA_HEAD_DIM

    n_cmp = n_cmp_rows - CMP_BLOCK // CMP_STRIDE + 1
    c_start = np.arange(n_cmp_rows) * CMP_STRIDE
    s_start = np.arange(LANES) * SEL_BLOCK
    cover_t = ((c_start[None, :] <= s_start[:, None] + SEL_BLOCK - 1) & (c_start[None, :] + CMP_BLOCK - 1 >= s_start[:, None])
               & (np.arange(n_cmp_rows)[None, :] < n_cmp) & (np.arange(LANES)[:, None] < n_sel))
    cover_t = jnp.asarray(cover_t.astype(np.float32))
    expand_t = jnp.asarray(((np.arange(s)[:, None] // SEL_BLOCK) == np.arange(LANES)[None, :]).astype(np.float32), BF16)

    v0 = R1_VS * LANES
    v1 = R1_VW * LANES
    vt = jnp.stack([r1[:, :, v0:v0 + NSA_KV_WIDTH], r1[:, :, v1:v1 + NSA_KV_WIDTH]], axis=1)
    vt = vt.reshape(b, 2, nk, TK, NSA_KV_GROUPS, dh).transpose(0, 1, 4, 2, 5, 3)
    g0 = R2_SMALL * LANES + SMALL_GATE
    gate_t = jnp.swapaxes(r2[:, :, g0:g0 + 3 * NSA_HEADS], 1, 2)

    k_spec = lambda blk: pl.BlockSpec((None, s, LANES), lambda bi, g, i, blk=blk: (bi, 0, blk + g))
    vt_spec = lambda w: pl.BlockSpec((None, None, None, nk, dh, TK), lambda bi, g, i, w=w: (bi, w, g, 0, 0, 0))
    return pl.pallas_call(
        functools.partial(_nsa_attn_kernel, n_sel=n_sel, top_n=top_n),
        grid=(b, NSA_KV_GROUPS, nq),
        in_specs=[
            pl.BlockSpec((None, TQ, gw), lambda bi, g, i: (bi, i, g)),
            k_spec(R1_KS), k_spec(R1_KW), vt_spec(0), vt_spec(1),
            pl.BlockSpec((None, None, 2, n_cmp_rows, dh), lambda bi, g, i: (bi, g, 0, 0, 0)),
            pl.BlockSpec((None, 3 * NSA_HEADS, TQ), lambda bi, g, i: (bi, 0, i)),
            pl.BlockSpec((None, None, TK, cols), lambda bi, g, i: (g, i, 0, 0)),
            pl.BlockSpec((None, 3, TK, cols), lambda bi, g, i: (g, 0, 0, 0)),
            pl.BlockSpec((LANES, n_cmp_rows), lambda bi, g, i: (0, 0)),
            pl.BlockSpec((s, LANES), lambda bi, g, i: (0, 0)),
        ],
        out_specs=pl.BlockSpec((None, TQ, gw), lambda bi, g, i: (bi, i, g)),
        out_shape=jax.ShapeDtypeStruct((b, s, NSA_WIDTH), BF16),
        scratch_shapes=[
            pltpu.VMEM((s, TQ), F32),
            pltpu.VMEM((1, cols), F32),
            pltpu.VMEM((1, cols), F32),
            pltpu.VMEM((dh, cols), F32),
            pltpu.VMEM((dh, cols), F32),
        ],
        compiler_params=_cparams("parallel", "parallel", "arbitrary"),
        name="nsa_attention",
    )(r1, r1, r1, vt, vt, kcvc, gate_t, bias_c, bias_d, cover_t, expand_t)


GDN_PAIR = 2 * GDN_CHUNK
GDN_UNROLL = 4
GDN_ROWS = 512


def _gdn_prep_kernel(q_ref, k_ref, v_ref, smt_ref, alog_ref, dtb_ref, cwq_ref, cwk_ref, cwv_ref,
                     w_out, qg_out, u_out, qk_out, kdt_out, egl_out,
                     kb_s, kk_s, qq_s, vk_s, kd_s, gcol_s, grow_s):
    h = pl.program_id(1)
    s = q_ref.shape[0]
    c = GDN_CHUNK
    pr = GDN_PAIR
    dk = GDN_HEAD_DIM
    nt = (((1,), (1,)), ((), ()))
    tn = (((0,), (0,)), ((), ()))
    row = lax.broadcasted_iota(jnp.int32, (s, dk), 0)

    xg = smt_ref[0:GDN_HEADS, :] + dtb_ref[...]
    g8 = -jnp.exp(alog_ref[...]) * (jnp.maximum(xg, 0.0) + jnp.log1p(jnp.exp(-jnp.abs(xg))))
    beta8 = jax.nn.sigmoid(smt_ref[GDN_HEADS:2 * GDN_HEADS, :])
    pos = lax.broadcasted_iota(jnp.int32, g8.shape, 1) % c
    gcum8 = g8
    sh = 1
    while sh < c:
        gcum8 = gcum8 + jnp.where(pos >= sh, pltpu.roll(gcum8, sh, axis=1), 0.0)
        sh *= 2
    for p in range(s // pr):
        grow_s[p] = gcum8[:, p * pr:(p + 1) * pr]
    sel_k = lax.broadcasted_iota(jnp.int32, (2 * GDN_HEADS, 2 * dk), 0)
    sel_l = lax.broadcasted_iota(jnp.int32, (2 * GDN_HEADS, 2 * dk), 1)
    pick = jnp.where(((sel_k == h) & (sel_l < dk)) | ((sel_k == GDN_HEADS + h) & (sel_l >= dk)), 1.0, 0.0)
    col = lax.dot_general(jnp.concatenate([gcum8, beta8], axis=0), pick, tn, preferred_element_type=F32,
                          precision=HIGHEST)
    gcol = col[:, :dk]
    bcol = col[:, dk:]
    gcol_s[...] = gcol

    def conv_silu(x_ref, w_ref):
        x = x_ref[...]
        w = w_ref[...]
        y = x * w[CONV_WIDTH - 1:CONV_WIDTH, :]
        for sft in range(1, CONV_WIDTH):
            xs = jnp.where(row >= sft, pltpu.roll(x, sft, axis=0), 0.0)
            y = y + xs * w[CONV_WIDTH - 1 - sft:CONV_WIDTH - sft, :]
        return y * jax.nn.sigmoid(y)

    def l2n(x):
        return x * lax.rsqrt(jnp.sum(x * x, axis=-1, keepdims=True) + NORM_EPS)

    q = l2n(conv_silu(q_ref, cwq_ref)) * (dk ** -0.5)
    k = l2n(conv_silu(k_ref, cwk_ref))
    v = conv_silu(v_ref, cwv_ref)
    eg = jnp.exp(gcol)
    kb = k * bcol
    qg_out[...] = (q * eg).astype(BF16)
    kk_s[...] = k.astype(BF16)
    kb_s[...] = kb.astype(BF16)
    qq_s[...] = q.astype(BF16)
    vk_s[:, :dk] = (v * bcol).astype(BF16)
    vk_s[:, dk:] = (kb * eg).astype(BF16)
    g3 = gcol.reshape(s // c, c, dk)
    g_last = g3[:, c - 1:c, :]
    kd_s[...] = (k * jnp.exp(jnp.broadcast_to(g_last, g3.shape).reshape(s, dk) - gcol)).astype(BF16)
    egl_out[...] = jnp.broadcast_to(jnp.exp(g_last), egl_out.shape)

    r_i = lax.broadcasted_iota(jnp.int32, (pr, pr), 0)
    l_i = lax.broadcasted_iota(jnp.int32, (pr, pr), 1)
    tri = ((r_i // c) == (l_i // c)) & (r_i >= l_i)
    eye = r_i == l_i
    eye_f = jnp.where(eye, 1.0, 0.0)
    eye_b = eye_f.astype(BF16)

    def pair(p):
        rows = pl.ds(pl.multiple_of(p * pr, pr), pr)
        x = lax.dot_general(jnp.concatenate([kb_s[rows, :], qq_s[rows, :]], axis=0), kk_s[rows, :], nt,
                            preferred_element_type=F32)
        diff = gcol_s[rows, :] - grow_s[p, pl.ds(h, 1), :]
        decay = jnp.where(tri, jnp.exp(jnp.where(tri, diff, 0.0)), 0.0)
        qk_out[rows, :] = (x[pr:] * decay).astype(BF16)
        qm = jnp.where(eye, 0.0, -(x[:pr] * decay))
        tinv = eye_f + qm
        qb = qm.astype(BF16)
        qm = jnp.dot(qb, qb, preferred_element_type=F32)
        steps = int(math.log2(c)) - 1
        for it in range(steps):
            qb = qm.astype(BF16)
            if it < steps - 1:
                r = jnp.dot(jnp.concatenate([qb, tinv.astype(BF16)], axis=0), qb, preferred_element_type=F32)
                qm = r[:pr]
                tinv = tinv + r[pr:]
            else:
                tinv = tinv + jnp.dot(tinv.astype(BF16), qb, preferred_element_type=F32)
        uw = jnp.dot(tinv.astype(BF16), vk_s[rows, :], preferred_element_type=F32)
        u_out[rows, :] = uw[:, :dk]
        w_out[rows, :] = uw[:, dk:].astype(BF16)
        kdt_out[p] = lax.dot_general(eye_b, kd_s[rows, :], nt, preferred_element_type=F32).astype(BF16)

    def body(it, carry):
        for uu in range(GDN_UNROLL):
            pair(it * GDN_UNROLL + uu)
        return carry

    lax.fori_loop(0, s // pr // GDN_UNROLL, body, 0)


def _gdn_scan_kernel(w_ref, qg_ref, u_ref, qk_ref, kdt_ref, egl_ref, z_ref, nw_ref, o_ref, state_ref, os_ref):
    c = GDN_CHUNK
    pr = GDN_PAIR
    dk = GDN_HEAD_DIM
    nh = w_ref.shape[0]
    rb = w_ref.shape[1]

    @pl.when(pl.program_id(1) == 0)
    def _():
        state_ref[...] = jnp.zeros(state_ref.shape, F32)

    zeros = jnp.zeros((c, dk), BF16)

    def pair(p, carry):
        r0 = pl.multiple_of(p * pr, pr)
        for half in range(2):
            rows = pl.ds(r0 + half * c, c)
            for h in range(nh):
                state = state_ref[h]
                r1 = jnp.dot(jnp.concatenate([w_ref[h, rows, :], qg_ref[h, rows, :]], axis=0), state.astype(BF16),
                             preferred_element_type=F32)
                vnb = (u_ref[h, rows, :] - r1[:c]).astype(BF16)
                rhs = jnp.concatenate([vnb, zeros] if half == 0 else [zeros, vnb], axis=0)
                r2 = jnp.dot(jnp.concatenate([qk_ref[h, rows, :], kdt_ref[h, p]], axis=0), rhs,
                             preferred_element_type=F32)
                os_ref[rows, h * dk:(h + 1) * dk] = r1[c:] + r2[:c]
                state_ref[h] = state * egl_ref[h, 2 * p + half, 0:1, :] + r2[c:]
        return carry

    lax.fori_loop(0, rb // pr, pair, 0)

    for h in range(nh):
        o = os_ref[:, h * dk:(h + 1) * dk]
        o = o * lax.rsqrt(jnp.mean(o * o, axis=-1, keepdims=True) + NORM_EPS) * nw_ref[...]
        z = z_ref[:, h * dk:(h + 1) * dk]
        o_ref[:, h * dk:(h + 1) * dk] = (o * (z * jax.nn.sigmoid(z))).astype(o_ref.dtype)


def gated_deltanet(r2, conv_w, a_log, dt_bias, norm_w):
    b, s, _ = r2.shape
    dk = GDN_HEAD_DIM
    nh = GDN_HEADS
    npair = s // GDN_PAIR
    nchunk = s // GDN_CHUNK
    rb = GDN_ROWS
    assert s % (GDN_PAIR * GDN_UNROLL) == 0 and s % rb == 0 and (R2_Z * LANES) % GDN_WIDTH == 0
    a0 = R2_SMALL * LANES + SMALL_A
    smt = jnp.swapaxes(r2[:, :, a0:a0 + 2 * nh], 1, 2)
    col = lambda blk: pl.BlockSpec((None, s, dk), lambda bi, h, blk=blk: (bi, 0, blk + h))
    cw = lambda blk: pl.BlockSpec((CONV_WIDTH, dk), lambda bi, h, blk=blk: (0, blk + h))
    vec = pl.BlockSpec((nh, 1), lambda bi, h: (0, 0))
    per_head = lambda *shape: pl.BlockSpec((None, None) + shape, lambda bi, h: (bi, h) + (0,) * len(shape))
    w, qg, u, qk, kdt, egl = pl.pallas_call(
        _gdn_prep_kernel,
        grid=(b, nh),
        in_specs=[col(R2_GQ), col(R2_GK), col(R2_GV),
                  pl.BlockSpec((None, 2 * nh, s), lambda bi, h: (bi, 0, 0)), vec, vec,
                  cw(0), cw(nh), cw(2 * nh)],
        out_specs=[per_head(s, dk), per_head(s, dk), per_head(s, dk), per_head(s, dk),
                   per_head(npair, GDN_PAIR, GDN_PAIR), per_head(nchunk, 8, dk)],
        out_shape=[jax.ShapeDtypeStruct((b, nh, s, dk), BF16), jax.ShapeDtypeStruct((b, nh, s, dk), BF16),
                   jax.ShapeDtypeStruct((b, nh, s, dk), F32), jax.ShapeDtypeStruct((b, nh, s, dk), BF16),
                   jax.ShapeDtypeStruct((b, nh, npair, GDN_PAIR, GDN_PAIR), BF16),
                   jax.ShapeDtypeStruct((b, nh, nchunk, 8, dk), F32)],
        scratch_shapes=[pltpu.VMEM((s, dk), BF16), pltpu.VMEM((s, dk), BF16), pltpu.VMEM((s, dk), BF16),
                        pltpu.VMEM((s, 2 * dk), BF16), pltpu.VMEM((s, dk), BF16), pltpu.VMEM((s, dk), F32),
                        pltpu.VMEM((npair, nh, GDN_PAIR), F32)],
        compiler_params=_cparams("parallel", "parallel"),
        name="gdn_prep",
    )(r2, r2, r2, smt, a_log.reshape(nh, 1), dt_bias.reshape(nh, 1), conv_w, conv_w, conv_w)

    seq = lambda *shape: pl.BlockSpec((None, nh) + shape, lambda bi, j: (bi, 0, j) + (0,) * (len(shape) - 1))
    return pl.pallas_call(
        _gdn_scan_kernel,
        grid=(b, s // rb),
        in_specs=[seq(rb, dk), seq(rb, dk), seq(rb, dk), seq(rb, dk),
                  seq(rb // GDN_PAIR, GDN_PAIR, GDN_PAIR), seq(rb // GDN_CHUNK, 8, dk),
                  pl.BlockSpec((None, rb, GDN_WIDTH), lambda bi, j: (bi, j, R2_Z * LANES // GDN_WIDTH)),
                  pl.BlockSpec((1, dk), lambda bi, j: (0, 0))],
        out_specs=pl.BlockSpec((None, rb, GDN_WIDTH), lambda bi, j: (bi, j, 0)),
        out_shape=jax.ShapeDtypeStruct((b, s, GDN_WIDTH), BF16),
        scratch_shapes=[pltpu.VMEM((nh, dk, dk), F32), pltpu.VMEM((rb, GDN_WIDTH), F32)],
        compiler_params=_cparams("parallel", "arbitrary"),
        name="gdn_scan",
    )(w, qg, u, qk, kdt, egl, r2, norm_w.reshape(1, dk))


def _split_w_in(w_in):
    o_q = 0
    o_kv = NSA_WIDTH
    o_gate = o_kv + 6 * NSA_KV_WIDTH
    o_qkv = o_gate + 3 * NSA_HEADS
    o_z = o_qkv + 3 * GDN_WIDTH
    o_a = o_z + GDN_WIDTH
    o_b = o_a + GDN_HEADS
    o_m = o_b + GDN_HEADS
    kvw = NSA_KV_WIDTH
    w1 = jnp.concatenate([w_in[:, o_q:o_kv], w_in[:, o_kv + 2 * kvw:o_kv + 6 * kvw]], axis=1)
    small = jnp.concatenate([w_in[:, o_gate:o_qkv], w_in[:, o_a:o_m]], axis=1)
    used = (R2_SMALL * LANES) + small.shape[1]
    w2 = jnp.concatenate([w_in[:, o_z:o_a], w_in[:, o_kv:o_kv + 2 * kvw], w_in[:, o_qkv:o_z], w_in[:, o_m:], small,
                          jnp.zeros((w_in.shape[0], R2_COLS - used), w_in.dtype)], axis=1)
    return w1.astype(BF16), w2.astype(BF16)


def _layer(x, bias_c, bias_d, ln1_w, w_in, pe_k, pe_v, w1_k, w2_k, w1_v, w2_v, conv_w, a_log, dt_bias,
           gdn_norm_w, w_pa, w_pb, w_o, ln2_w, w_up, w_down, final_norm_w, b, s):
    t = b * s
    w_r1, w_r2 = _split_w_in(w_in)
    h = rmsnorm(x, ln1_w, BF16)
    r1 = matmul(h, w_r1, BF16, tm=1024, tn=1024, tk=D_MODEL)
    r2 = matmul(h, w_r2, F32, tm=1024, tn=1024, tk=D_MODEL)
    r1 = r1.reshape(b, s, R1_COLS)
    r2 = r2.reshape(b, s, R2_COLS)
    kcvc = nsa_compress(r2, jnp.stack([pe_k, pe_v]), jnp.stack([w1_k, w1_v]).astype(BF16),
                        w2_k.astype(BF16), w2_v.T.astype(BF16))
    o_a = nsa_attention(r1, r2, kcvc, bias_c, bias_d).reshape(t, NSA_WIDTH)
    o_b = gated_deltanet(r2, conv_w, a_log, dt_bias, gdn_norm_w).reshape(t, GDN_WIDTH)
    m = merge_branches(o_a, o_b, w_pa.astype(BF16), w_pb.astype(BF16), r2.reshape(t, R2_COLS))
    x = matmul(m, w_o.astype(BF16), F32, tm=1024, tn=1024, tk=D_MODEL, epilogue="residual", res=x)
    h2 = rmsnorm(x, ln2_w, BF16)
    u = matmul(h2, w_up.astype(BF16), BF16, tm=1024, tn=1024, tk=D_MODEL, epilogue="relu2")
    if final_norm_w is None:
        return matmul(u, w_down.astype(BF16), F32, tm=512, tn=D_MODEL, tk=1024, epilogue="residual", res=x)
    return matmul(u, w_down.astype(BF16), F32, tm=512, tn=D_MODEL, tk=1024, epilogue="residual_norm", res=x,
                  norm_w=final_norm_w)


def kernel(x, rel_table, ln1_w, w_in, cmp_pe_k, cmp_pe_v, cmp_w1_k, cmp_w2_k, cmp_w1_v, cmp_w2_v, conv_w, a_log,
           dt_bias, gdn_norm_w, w_pa, w_pb, w_o, ln2_w, w_up, w_down, ln_f_w):
    b, s, d = x.shape
    depth = w_in.shape[0]
    bias_c, bias_d = bias_tables(rel_table, s)
    xt = x.reshape(b * s, d)
    for l in range(depth):
        xt = _layer(xt, bias_c, bias_d, ln1_w[l], w_in[l], cmp_pe_k[l], cmp_pe_v[l], cmp_w1_k[l], cmp_w2_k[l],
                    cmp_w1_v[l], cmp_w2_v[l], conv_w[l], a_log[l], dt_bias[l], gdn_norm_w[l], w_pa[l], w_pb[l],
                    w_o[l], ln2_w[l], w_up[l], w_down[l], ln_f_w if l == depth - 1 else None, b, s)
    return xt.reshape(b, s, d)
```

```python
import functools
import math

import numpy as np
import jax
import jax.numpy as jnp
from jax import lax
from jax.experimental import pallas as pl
from jax.experimental.pallas import tpu as pltpu

F32 = jnp.float32
BF16 = jnp.bfloat16
HIGHEST = lax.Precision.HIGHEST

D_MODEL = 2048
DEPTH = 2
NSA_HEADS = 8
NSA_KV_GROUPS = 2
NSA_HPG = NSA_HEADS // NSA_KV_GROUPS
NSA_HEAD_DIM = 128
NSA_WIDTH = NSA_HEADS * NSA_HEAD_DIM
NSA_KV_WIDTH = NSA_KV_GROUPS * NSA_HEAD_DIM
CMP_BLOCK = 32
CMP_STRIDE = 16
CMP_HIDDEN = 256
SEL_BLOCK = 64
SEL_TOP_N = 16
SEL_FORCE = 1000.0
WINDOW = 512
GDN_HEADS = 8
GDN_HEAD_DIM = 128
GDN_WIDTH = GDN_HEADS * GDN_HEAD_DIM
CONV_WIDTH = 4
GDN_CHUNK = 64
REL_BUCKETS = 32
REL_MAX_DIST = 128
D_FF = 4 * D_MODEL
NORM_EPS = 1e-6

LANES = 128
VMEM_LIMIT = 56 * 1024 * 1024
TQ = 128
TK = 128
NEG_INF = float("-inf")

R1_COLS = 2048
R1_KS, R1_VS, R1_KW, R1_VW = 8, 10, 12, 14
R2_Z, R2_KC, R2_VC, R2_GQ, R2_GK, R2_GV, R2_MA, R2_MB, R2_SMALL = 0, 8, 10, 12, 20, 28, 36, 52, 68
R2_COLS = 72 * LANES
SMALL_GATE, SMALL_A, SMALL_B = 0, 24, 32


def _t5_thresholds():
    exact = REL_BUCKETS // 2
    n = np.arange(exact, REL_MAX_DIST * 2, dtype=np.int64)
    nf = n.astype(np.float32)
    large = exact + (np.log(nf / np.float32(exact)) / np.float32(math.log(REL_MAX_DIST / exact))
                     * np.float32(REL_BUCKETS - exact)).astype(np.int32)
    large = np.minimum(large, REL_BUCKETS - 1)
    thr = []
    for b in range(exact + 1, REL_BUCKETS):
        thr.append(int(n[np.argmax(large >= b)]))
    return tuple(thr)


T5_THRESHOLDS = _t5_thresholds()


def _cparams(*sem):
    return pltpu.CompilerParams(dimension_semantics=sem, vmem_limit_bytes=VMEM_LIMIT)


def _rmsnorm_kernel(x_ref, w_ref, o_ref):
    x = x_ref[...]
    y = x * lax.rsqrt(jnp.mean(x * x, axis=-1, keepdims=True) + NORM_EPS)
    o_ref[...] = (y * w_ref[...]).astype(o_ref.dtype)


def rmsnorm(x, w, out_dtype, tm=512):
    t, d = x.shape
    return pl.pallas_call(
        _rmsnorm_kernel,
        grid=(t // tm,),
        in_specs=[pl.BlockSpec((tm, d), lambda i: (i, 0)), pl.BlockSpec((1, d), lambda i: (0, 0))],
        out_specs=pl.BlockSpec((tm, d), lambda i: (i, 0)),
        out_shape=jax.ShapeDtypeStruct((t, d), out_dtype),
        compiler_params=_cparams("parallel"),
        name="rmsnorm",
    )(x, w.reshape(1, d))


def _mm_kernel(*refs, nk, epilogue):
    a_ref, w_ref = refs[0], refs[1]
    pos = 2
    res_ref = norm_ref = None
    if epilogue in ("residual", "residual_norm"):
        res_ref = refs[pos]
        pos += 1
    if epilogue == "residual_norm":
        norm_ref = refs[pos]
        pos += 1
    o_ref = refs[pos]
    acc_ref = refs[pos + 1] if nk > 1 else None

    def finish(acc):
        if epilogue == "relu2":
            acc = jnp.square(jnp.maximum(acc, 0.0))
        elif epilogue == "residual":
            acc = res_ref[...] + acc
        elif epilogue == "residual_norm":
            y = res_ref[...] + acc
            acc = y * lax.rsqrt(jnp.mean(y * y, axis=-1, keepdims=True) + NORM_EPS) * norm_ref[...]
        o_ref[...] = acc.astype(o_ref.dtype)

    part = jnp.dot(a_ref[...], w_ref[...], preferred_element_type=F32)
    if nk == 1:
        finish(part)
    else:
        k = pl.program_id(2)

        @pl.when(k == 0)
        def _():
            acc_ref[...] = part

        @pl.when(k > 0)
        def _():
            acc_ref[...] += part

        @pl.when(k == nk - 1)
        def _():
            finish(acc_ref[...])


def matmul(a, w, out_dtype, *, tm, tn, tk, epilogue="none", res=None, norm_w=None):
    m, kdim = a.shape
    n = w.shape[1]
    nk = kdim // tk
    in_specs = [pl.BlockSpec((tm, tk), lambda i, j, k: (i, k)), pl.BlockSpec((tk, tn), lambda i, j, k: (k, j))]
    args = [a, w]
    if epilogue in ("residual", "residual_norm"):
        in_specs.append(pl.BlockSpec((tm, tn), lambda i, j, k: (i, j)))
        args.append(res)
    if epilogue == "residual_norm":
        assert tn == n
        in_specs.append(pl.BlockSpec((1, tn), lambda i, j, k: (0, 0)))
        args.append(norm_w.reshape(1, n))
    scratch = [pltpu.VMEM((tm, tn), F32)] if nk > 1 else []
    return pl.pallas_call(
        functools.partial(_mm_kernel, nk=nk, epilogue=epilogue),
        grid=(m // tm, n // tn, nk),
        in_specs=in_specs,
        out_specs=pl.BlockSpec((tm, tn), lambda i, j, k: (i, j)),
        out_shape=jax.ShapeDtypeStruct((m, n), out_dtype),
        scratch_shapes=scratch,
        compiler_params=_cparams("parallel", "parallel", "arbitrary"),
        name="matmul_" + epilogue,
    )(*args)


def _merge_kernel(oa_ref, ob_ref, wa_ref, wb_ref, ma_ref, mb_ref, o_ref):
    pa = jnp.dot(oa_ref[...], wa_ref[...], preferred_element_type=F32)
    pb = jnp.dot(ob_ref[...], wb_ref[...], preferred_element_type=F32)
    o_ref[...] = (jax.nn.sigmoid(ma_ref[...]) * pa + jax.nn.sigmoid(mb_ref[...]) * pb).astype(o_ref.dtype)


def merge_branches(o_a, o_b, w_pa, w_pb, r2, *, tm=1024, tn=512):
    t, ka = o_a.shape
    kb = o_b.shape[1]
    n = w_pa.shape[1]
    ma0 = R2_MA * LANES // tn
    mb0 = R2_MB * LANES // tn
    return pl.pallas_call(
        _merge_kernel,
        grid=(t // tm, n // tn),
        in_specs=[
            pl.BlockSpec((tm, ka), lambda i, j: (i, 0)),
            pl.BlockSpec((tm, kb), lambda i, j: (i, 0)),
            pl.BlockSpec((ka, tn), lambda i, j: (0, j)),
            pl.BlockSpec((kb, tn), lambda i, j: (0, j)),
            pl.BlockSpec((tm, tn), lambda i, j: (i, ma0 + j)),
            pl.BlockSpec((tm, tn), lambda i, j: (i, mb0 + j)),
        ],
        out_specs=pl.BlockSpec((tm, tn), lambda i, j: (i, j)),
        out_shape=jax.ShapeDtypeStruct((t, n), BF16),
        compiler_params=_cparams("parallel", "parallel"),
        name="merge_branches",
    )(o_a, o_b, w_pa, w_pb, r2, r2)


def _t5_lookup(dist, tab_ref, h):
    n = jnp.maximum(dist, 0)
    exact = REL_BUCKETS // 2
    large = jnp.full(n.shape, exact, jnp.int32)
    for thr in T5_THRESHOLDS:
        large = large + (n >= thr).astype(jnp.int32)
    bucket = jnp.where(n < exact, n, large)
    out = jnp.zeros(n.shape, F32)
    for b in range(REL_BUCKETS):
        out = jnp.where(bucket == b, tab_ref[b, h], out)
    return out


def _bias_kernel(tab_ref, bc_ref, bd_ref):
    g = pl.program_id(0)
    i = pl.program_id(1)
    key = lax.broadcasted_iota(jnp.int32, (TK, TQ), 0)
    qry = lax.broadcasted_iota(jnp.int32, (TK, TQ), 1)
    for j in range(NSA_HPG):
        h = g * NSA_HPG + j
        bc_ref[:, j * TQ:(j + 1) * TQ] = _t5_lookup(i * TQ + qry - (key * CMP_STRIDE + CMP_BLOCK - 1), tab_ref, h)

    @pl.when(i == 0)
    def _():
        for v in range(3):
            for j in range(NSA_HPG):
                bd_ref[v, :, j * TQ:(j + 1) * TQ] = _t5_lookup(v * TK + qry - key, tab_ref, g * NSA_HPG + j)


def bias_tables(rel_table, s):
    nq = s // TQ
    cols = NSA_HPG * TQ
    return pl.pallas_call(
        _bias_kernel,
        grid=(NSA_KV_GROUPS, nq),
        in_specs=[pl.BlockSpec(memory_space=pltpu.SMEM)],
        out_specs=[pl.BlockSpec((None, None, TK, cols), lambda g, i: (g, i, 0, 0)),
                   pl.BlockSpec((None, 3, TK, cols), lambda g, i: (g, 0, 0, 0))],
        out_shape=[jax.ShapeDtypeStruct((NSA_KV_GROUPS, nq, TK, cols), F32),
                   jax.ShapeDtypeStruct((NSA_KV_GROUPS, 3, TK, cols), F32)],
        compiler_params=_cparams("arbitrary", "arbitrary"),
        name="t5_bias_tables",
    )(rel_table)


def _gelu_tanh(x):
    return 0.5 * x * (1.0 + jnp.tanh(math.sqrt(2.0 / math.pi) * (x + 0.044715 * (x * x * x))))


def _compress_kernel(x_ref, pe_ref, w1_ref, w2k_ref, w2vt_ref, o_ref, *, nblk):
    pe = pe_ref[...]
    acc_a = jnp.zeros((nblk, CMP_HIDDEN), F32)
    acc_b = jnp.zeros((nblk, CMP_HIDDEN), F32)
    for r in range(CMP_STRIDE):
        xr = x_ref[pl.ds(r, nblk, stride=CMP_STRIDE), :]
        xa = (xr + pe[r:r + 1, :]).astype(BF16)
        xb = (xr + pe[CMP_STRIDE + r:CMP_STRIDE + r + 1, :]).astype(BF16)
        acc_a = acc_a + jnp.dot(xa, w1_ref[r * NSA_HEAD_DIM:(r + 1) * NSA_HEAD_DIM, :], preferred_element_type=F32)
        acc_b = acc_b + jnp.dot(xb, w1_ref[(CMP_STRIDE + r) * NSA_HEAD_DIM:(CMP_STRIDE + r + 1) * NSA_HEAD_DIM, :],
                                preferred_element_type=F32)
    hidden = acc_a + jnp.concatenate([acc_b[1:], acc_b[:1]], axis=0)
    act = _gelu_tanh(hidden).astype(BF16)

    @pl.when(pl.program_id(2) == 0)
    def _():
        o_ref[...] = jnp.dot(act, w2k_ref[...], preferred_element_type=F32).astype(o_ref.dtype)

    @pl.when(pl.program_id(2) == 1)
    def _():
        o_ref[...] = lax.dot_general(w2vt_ref[...], act, (((1,), (1,)), ((), ())),
                                     preferred_element_type=F32).astype(o_ref.dtype)


def nsa_compress(r2, pe, w1, w2k, w2vt):
    b, s, _ = r2.shape
    nblk = s // CMP_STRIDE
    assert CMP_BLOCK == 2 * CMP_STRIDE and nblk == NSA_HEAD_DIM
    return pl.pallas_call(
        functools.partial(_compress_kernel, nblk=nblk),
        grid=(b, NSA_KV_GROUPS, 2),
        in_specs=[
            pl.BlockSpec((None, s, LANES), lambda bi, g, c: (bi, 0, R2_KC + 2 * c + g)),
            pl.BlockSpec((None, CMP_BLOCK, NSA_HEAD_DIM), lambda bi, g, c: (c, 0, 0)),
            pl.BlockSpec((None, CMP_BLOCK * NSA_HEAD_DIM, CMP_HIDDEN), lambda bi, g, c: (c, 0, 0)),
            pl.BlockSpec((CMP_HIDDEN, NSA_HEAD_DIM), lambda bi, g, c: (0, 0)),
            pl.BlockSpec((NSA_HEAD_DIM, CMP_HIDDEN), lambda bi, g, c: (0, 0)),
        ],
        out_specs=pl.BlockSpec((None, None, None, nblk, NSA_HEAD_DIM), lambda bi, g, c: (bi, g, c, 0, 0)),
        out_shape=jax.ShapeDtypeStruct((b, NSA_KV_GROUPS, 2, nblk, NSA_HEAD_DIM), BF16),
        compiler_params=_cparams("parallel", "parallel", "arbitrary"),
        name="nsa_compress",
    )(r2, pe, w1, w2k, w2vt)


def _nsa_attn_kernel(q_ref, ks_ref, kw_ref, vst_ref, vwt_ref, kcvc_ref, gate_ref, biasc_ref, biasd_ref,
                     covert_ref, expandt_ref, o_ref, sel_ref, m_ref, l_ref, acc_ref, res_ref, *, n_sel, top_n):
    g = pl.program_id(1)
    i = pl.program_id(2)
    cols = NSA_HPG * TQ
    scale = NSA_HEAD_DIM ** -0.5
    nt = (((1,), (1,)), ((), ()))

    qf = q_ref[...]
    qs = jnp.concatenate([qf[:, j * NSA_HEAD_DIM:(j + 1) * NSA_HEAD_DIM] for j in range(NSA_HPG)], axis=0)
    key = lax.broadcasted_iota(jnp.int32, (TK, cols), 0)
    t_loc = lax.broadcasted_iota(jnp.int32, (TK, cols), 1) % TQ

    def gate_row(c):
        return jnp.concatenate(
            [jax.nn.sigmoid(gate_ref[pl.ds((g * NSA_HPG + j) * 3 + c, 1), :]) for j in range(NSA_HPG)], axis=1)

    def lanes4(x):
        return jnp.concatenate([x] * NSA_HPG, axis=1)

    s = lax.dot_general(kcvc_ref[0], qs, nt, preferred_element_type=F32) * scale + biasc_ref[...]
    s = jnp.where(key * CMP_STRIDE + (CMP_BLOCK - 1) <= i * TQ + t_loc, s, NEG_INF)
    mx = jnp.max(s, axis=0, keepdims=True)
    mx = jnp.where(mx == NEG_INF, 0.0, mx)
    e = jnp.exp(s - mx)
    p = e / jnp.maximum(jnp.sum(e, axis=0, keepdims=True), 1e-30)
    o_c = jnp.dot(kcvc_ref[1], p.astype(BF16), preferred_element_type=F32)
    res_ref[...] = gate_row(0) * o_c

    psum = p[:, 0:TQ]
    for j in range(1, NSA_HPG):
        psum = psum + p[:, j * TQ:(j + 1) * TQ]
    imp = jnp.dot(covert_ref[...], psum, preferred_element_type=F32, precision=HIGHEST)[:n_sel]
    blk = lax.broadcasted_iota(jnp.int32, (n_sel, TQ), 0)
    cur = (i * TQ + lax.broadcasted_iota(jnp.int32, (n_sel, TQ), 1)) // SEL_BLOCK
    causal_blk = blk <= cur
    forced = (blk == 0) | (blk == cur) | (blk == cur - 1)
    score = jnp.where(causal_blk, imp + jnp.where(forced, SEL_FORCE, 0.0), NEG_INF)
    rank = jnp.zeros((n_sel, TQ), jnp.int32)
    for mp in range(n_sel):
        other = score[mp:mp + 1, :]
        beats = (other > score) | ((other == score) & (blk > mp))
        rank = rank + beats.astype(jnp.int32)
    sel = jnp.where((rank < top_n) & causal_blk, 1.0, 0.0)
    sel = jnp.concatenate([sel, jnp.zeros((LANES - n_sel, TQ), F32)], axis=0).astype(BF16)
    sel_ref[...] = jnp.where(jnp.dot(expandt_ref[...], sel, preferred_element_type=F32) > 0.5, 0.0, NEG_INF)

    def reset():
        m_ref[...] = jnp.full(m_ref.shape, NEG_INF, F32)
        l_ref[...] = jnp.zeros(l_ref.shape, F32)
        acc_ref[...] = jnp.zeros(acc_ref.shape, F32)

    def attend(k_tile, vt_tile, variant, mask_add):
        s = lax.dot_general(k_tile, qs, nt, preferred_element_type=F32) * scale + biasd_ref[variant]
        if mask_add is not None:
            s = s + mask_add
        m_prev = m_ref[...]
        m_new = jnp.maximum(m_prev, jnp.max(s, axis=0, keepdims=True))
        m_safe = jnp.where(m_new == NEG_INF, 0.0, m_new)
        alpha = jnp.exp(m_prev - m_safe)
        p = jnp.exp(s - m_safe)
        l_ref[...] = alpha * l_ref[...] + jnp.sum(p, axis=0, keepdims=True)
        acc_ref[...] = alpha * acc_ref[...] + jnp.dot(vt_tile, p.astype(BF16), preferred_element_type=F32)
        m_ref[...] = m_new

    def finish():
        return acc_ref[...] * (1.0 / jnp.maximum(l_ref[...], 1e-30))

    def tile(ref, k):
        return ref[pl.ds(pl.multiple_of(k * TK, TK), TK), :]

    causal_add = jnp.where(key <= t_loc, 0.0, NEG_INF)

    reset()

    def sel_far(k, carry):
        attend(tile(ks_ref, k), vst_ref[k], 2, lanes4(tile(sel_ref, k)))
        return carry

    lax.fori_loop(0, jnp.maximum(i - 1, 0), sel_far, 0)

    @pl.when(i >= 1)
    def _():
        attend(tile(ks_ref, i - 1), vst_ref[i - 1], 1, lanes4(tile(sel_ref, i - 1)))

    attend(tile(ks_ref, i), vst_ref[i], 0, lanes4(tile(sel_ref, i)) + causal_add)
    res_ref[...] += gate_row(1) * finish()

    reset()
    n_win = WINDOW // TK
    for d in range(n_win, 0, -1):
        @pl.when(i >= d)
        def _(d=d):
            edge_add = jnp.where(key > t_loc, 0.0, NEG_INF) if d == n_win else None
            attend(tile(kw_ref, i - d), vwt_ref[i - d], min(d, 2), edge_add)

    attend(tile(kw_ref, i), vwt_ref[i], 0, causal_add)
    res = res_ref[...] + gate_row(2) * finish()
    for j in range(NSA_HPG):
        o_ref[:, j * NSA_HEAD_DIM:(j + 1) * NSA_HEAD_DIM] = res[:, j * TQ:(j + 1) * TQ].T.astype(o_ref.dtype)


def nsa_attention(r1, r2, kcvc, bias_c, bias_d):
    b, s, _ = r1.shape
    n_cmp_rows = s // CMP_STRIDE
    n_sel = s // SEL_BLOCK
    assert n_cmp_rows == TK and n_sel <= LANES and n_sel % 8 == 0 and WINDOW % TK == 0 and TQ == TK
    assert 2 * TK - (TQ - 1) >= REL_MAX_DIST
    top_n = min(SEL_TOP_N, n_sel)
    nq = s // TQ
    nk = s // TK
    gw = NSA_HPG * NSA_HEAD_DIM
    cols = NSA_HPG * TQ
    dh = NSA_HEAD_DIM

    n_cmp = n_cmp_rows - CMP_BLOCK // CMP_STRIDE + 1
    c_start = np.arange(n_cmp_rows) * CMP_STRIDE
    s_start = np.arange(LANES) * SEL_BLOCK
    cover_t = ((c_start[None, :] <= s_start[:, None] + SEL_BLOCK - 1) & (c_start[None, :] + CMP_BLOCK - 1 >= s_start[:, None])
               & (np.arange(n_cmp_rows)[None, :] < n_cmp) & (np.arange(LANES)[:, None] < n_sel))
    cover_t = jnp.asarray(cover_t.astype(np.float32))
    expand_t = jnp.asarray(((np.arange(s)[:, None] // SEL_BLOCK) == np.arange(LANES)[None, :]).astype(np.float32), BF16)

    v0 = R1_VS * LANES
    v1 = R1_VW * LANES
    vt = jnp.stack([r1[:, :, v0:v0 + NSA_KV_WIDTH], r1[:, :, v1:v1 + NSA_KV_WIDTH]], axis=1)
    vt = vt.reshape(b, 2, nk, TK, NSA_KV_GROUPS, dh).transpose(0, 1, 4, 2, 5, 3)
    g0 = R2_SMALL * LANES + SMALL_GATE
    gate_t = jnp.swapaxes(r2[:, :, g0:g0 + 3 * NSA_HEADS], 1, 2)

    k_spec = lambda blk: pl.BlockSpec((None, s, LANES), lambda bi, g, i, blk=blk: (bi, 0, blk + g))
    vt_spec = lambda w: pl.BlockSpec((None, None, None, nk, dh, TK), lambda bi, g, i, w=w: (bi, w, g, 0, 0, 0))
    return pl.pallas_call(
        functools.partial(_nsa_attn_kernel, n_sel=n_sel, top_n=top_n),
        grid=(b, NSA_KV_GROUPS, nq),
        in_specs=[
            pl.BlockSpec((None, TQ, gw), lambda bi, g, i: (bi, i, g)),
            k_spec(R1_KS), k_spec(R1_KW), vt_spec(0), vt_spec(1),
            pl.BlockSpec((None, None, 2, n_cmp_rows, dh), lambda bi, g, i: (bi, g, 0, 0, 0)),
            pl.BlockSpec((None, 3 * NSA_HEADS, TQ), lambda bi, g, i: (bi, 0, i)),
            pl.BlockSpec((None, None, TK, cols), lambda bi, g, i: (g, i, 0, 0)),
            pl.BlockSpec((None, 3, TK, cols), lambda bi, g, i: (g, 0, 0, 0)),
            pl.BlockSpec((LANES, n_cmp_rows), lambda bi, g, i: (0, 0)),
            pl.BlockSpec((s, LANES), lambda bi, g, i: (0, 0)),
        ],
        out_specs=pl.BlockSpec((None, TQ, gw), lambda bi, g, i: (bi, i, g)),
        out_shape=jax.ShapeDtypeStruct((b, s, NSA_WIDTH), BF16),
        scratch_shapes=[
            pltpu.VMEM((s, TQ), F32),
            pltpu.VMEM((1, cols), F32),
            pltpu.VMEM((1, cols), F32),
            pltpu.VMEM((dh, cols), F32),
            pltpu.VMEM((dh, cols), F32),
        ],
        compiler_params=_cparams("parallel", "parallel", "arbitrary"),
        name="nsa_attention",
    )(r1, r1, r1, vt, vt, kcvc, gate_t, bias_c, bias_d, cover_t, expand_t)


GDN_PAIR = 2 * GDN_CHUNK
GDN_GROUP = 4
GDN_ROWS = 512


def _gdn_prep_kernel(q_ref, k_ref, v_ref, smt_ref, alog_ref, dtb_ref, cwq_ref, cwk_ref, cwv_ref,
                     w_out, qg_out, u_out, qk_out, kdt_out, egl_out,
                     kb_s, kk_s, qq_s, vk_s, kd_s, gcol_s, grow_s):
    h = pl.program_id(1)
    s = q_ref.shape[0]
    c = GDN_CHUNK
    pr = GDN_PAIR
    dk = GDN_HEAD_DIM
    nt = (((1,), (1,)), ((), ()))
    tn = (((0,), (0,)), ((), ()))

    xg = smt_ref[0:GDN_HEADS, :] + dtb_ref[...]
    g8 = -jnp.exp(alog_ref[...]) * (jnp.maximum(xg, 0.0) + jnp.log1p(jnp.exp(-jnp.abs(xg))))
    beta8 = jax.nn.sigmoid(smt_ref[GDN_HEADS:2 * GDN_HEADS, :])
    pos = lax.broadcasted_iota(jnp.int32, g8.shape, 1) % c
    gcum8 = g8
    sh = 1
    while sh < c:
        gcum8 = gcum8 + jnp.where(pos >= sh, pltpu.roll(gcum8, sh, axis=1), 0.0)
        sh *= 2
    for p in range(s // pr):
        grow_s[p] = gcum8[:, p * pr:(p + 1) * pr]
    sel_k = lax.broadcasted_iota(jnp.int32, (2 * GDN_HEADS, 2 * dk), 0)
    sel_l = lax.broadcasted_iota(jnp.int32, (2 * GDN_HEADS, 2 * dk), 1)
    pick = jnp.where(((sel_k == h) & (sel_l < dk)) | ((sel_k == GDN_HEADS + h) & (sel_l >= dk)), 1.0, 0.0)
    both = jnp.concatenate([gcum8, beta8], axis=0)
    hi = both.astype(BF16)
    rest = both - hi.astype(F32)
    mid = rest.astype(BF16)
    lo = (rest - mid.astype(F32)).astype(BF16)
    col = lax.dot_general(jnp.concatenate([hi, mid, lo], axis=0), jnp.concatenate([pick] * 3, axis=0).astype(BF16),
                          tn, preferred_element_type=F32)
    gcol = col[:, :dk]
    bcol = col[:, dk:]
    gcol_s[...] = gcol

    row8 = lax.broadcasted_iota(jnp.int32, (8, dk), 0)

    def conv_silu(x_ref, w_ref):
        x = x_ref[...]
        w = w_ref[...]
        y = x * w[CONV_WIDTH - 1:CONV_WIDTH, :]
        for sft in range(1, CONV_WIDTH):
            y = y + pltpu.roll(x, sft, axis=0) * w[CONV_WIDTH - 1 - sft:CONV_WIDTH - sft, :]
        x0 = x[:8]
        y0 = x0 * w[CONV_WIDTH - 1:CONV_WIDTH, :]
        for sft in range(1, CONV_WIDTH):
            y0 = y0 + jnp.where(row8 >= sft, pltpu.roll(x0, sft, axis=0), 0.0) * w[CONV_WIDTH - 1 - sft:CONV_WIDTH - sft, :]
        y = jnp.concatenate([y0, y[8:]], axis=0)
        return y * jax.nn.sigmoid(y)

    def l2n(x):
        return x * lax.rsqrt(jnp.sum(x * x, axis=-1, keepdims=True) + NORM_EPS)

    q = l2n(conv_silu(q_ref, cwq_ref)) * (dk ** -0.5)
    k = l2n(conv_silu(k_ref, cwk_ref))
    v = conv_silu(v_ref, cwv_ref)
    eg = jnp.exp(gcol)
    kb = k * bcol
    qg_out[...] = (q * eg).astype(BF16)
    kk_s[...] = k.astype(BF16)
    kb_s[...] = kb.astype(BF16)
    qq_s[...] = q.astype(BF16)
    vk_s[:, :dk] = (v * bcol).astype(BF16)
    vk_s[:, dk:] = (kb * eg).astype(BF16)
    g3 = gcol.reshape(s // c, c, dk)
    g_last = g3[:, c - 1:c, :]
    kd_s[...] = (k * jnp.exp(jnp.broadcast_to(g_last, g3.shape).reshape(s, dk) - gcol)).astype(BF16)
    egl_out[...] = jnp.broadcast_to(jnp.exp(g_last), egl_out.shape)

    r_i = lax.broadcasted_iota(jnp.int32, (pr, pr), 0)
    l_i = lax.broadcasted_iota(jnp.int32, (pr, pr), 1)
    tri = ((r_i // c) == (l_i // c)) & (r_i >= l_i)
    eye = r_i == l_i
    eye_f = jnp.where(eye, 1.0, 0.0)
    eye_b = eye_f.astype(BF16)

    def pair_group(ps):
        rows = [slice(p * pr, (p + 1) * pr) for p in ps]
        n = len(ps)
        x = [lax.dot_general(jnp.concatenate([kb_s[r, :], qq_s[r, :]], axis=0), kk_s[r, :], nt,
                             preferred_element_type=F32) for r in rows]
        decay = []
        for p, r in zip(ps, rows):
            diff = gcol_s[r, :] - grow_s[p, pl.ds(h, 1), :]
            decay.append(jnp.where(tri, jnp.exp(jnp.where(tri, diff, 0.0)), 0.0))
        for a in range(n):
            qk_out[rows[a], :] = (x[a][pr:] * decay[a]).astype(BF16)
            kdt_out[ps[a]] = lax.dot_general(eye_b, kd_s[rows[a], :], nt, preferred_element_type=F32).astype(BF16)
        qm = [jnp.where(eye, 0.0, -(x[a][:pr] * decay[a])) for a in range(n)]
        tinv = [eye_f + qm[a] for a in range(n)]
        qb = [qm[a].astype(BF16) for a in range(n)]
        qm = [jnp.dot(qb[a], qb[a], preferred_element_type=F32) for a in range(n)]
        steps = int(math.log2(c)) - 1
        for it in range(steps):
            qb = [qm[a].astype(BF16) for a in range(n)]
            if it < steps - 1:
                r = [jnp.dot(jnp.concatenate([qb[a], tinv[a].astype(BF16)], axis=0), qb[a],
                             preferred_element_type=F32) for a in range(n)]
                qm = [r[a][:pr] for a in range(n)]
                tinv = [tinv[a] + r[a][pr:] for a in range(n)]
            else:
                tinv = [tinv[a] + jnp.dot(tinv[a].astype(BF16), qb[a], preferred_element_type=F32) for a in range(n)]
        for a in range(n):
            uw = jnp.dot(tinv[a].astype(BF16), vk_s[rows[a], :], preferred_element_type=F32)
            u_out[rows[a], :] = uw[:, :dk]
            w_out[rows[a], :] = uw[:, dk:].astype(BF16)

    npair = s // pr
    for p0 in range(0, npair, GDN_GROUP):
        pair_group(list(range(p0, min(p0 + GDN_GROUP, npair))))


def _gdn_scan_kernel(w_ref, qg_ref, u_ref, qk_ref, kdt_ref, egl_ref, z_ref, nw_ref, o_ref, state_ref, os_ref):
    c = GDN_CHUNK
    pr = GDN_PAIR
    dk = GDN_HEAD_DIM
    nh = w_ref.shape[0]
    rb = w_ref.shape[1]

    @pl.when(pl.program_id(1) == 0)
    def _():
        state_ref[...] = jnp.zeros(state_ref.shape, F32)

    zeros = jnp.zeros((c, dk), BF16)

    def pair(p, carry):
        r0 = pl.multiple_of(p * pr, pr)
        for half in range(2):
            rows = pl.ds(r0 + half * c, c)
            state = [state_ref[h] for h in range(nh)]
            r1 = [jnp.dot(jnp.concatenate([w_ref[h, rows, :], qg_ref[h, rows, :]], axis=0), state[h].astype(BF16),
                          preferred_element_type=F32) for h in range(nh)]
            vnb = [(u_ref[h, rows, :] - r1[h][:c]).astype(BF16) for h in range(nh)]
            rhs = [jnp.concatenate([vnb[h], zeros] if half == 0 else [zeros, vnb[h]], axis=0) for h in range(nh)]
            r2 = [jnp.dot(jnp.concatenate([qk_ref[h, rows, :], kdt_ref[h, p]], axis=0), rhs[h],
                          preferred_element_type=F32) for h in range(nh)]
            for h in range(nh):
                os_ref[rows, h * dk:(h + 1) * dk] = r1[h][c:] + r2[h][:c]
                state_ref[h] = state[h] * egl_ref[h, 2 * p + half, 0:1, :] + r2[h][c:]
        return carry

    lax.fori_loop(0, rb // pr, pair, 0)

    for h in range(nh):
        o = os_ref[:, h * dk:(h + 1) * dk]
        o = o * lax.rsqrt(jnp.mean(o * o, axis=-1, keepdims=True) + NORM_EPS) * nw_ref[...]
        z = z_ref[:, h * dk:(h + 1) * dk]
        o_ref[:, h * dk:(h + 1) * dk] = (o * (z * jax.nn.sigmoid(z))).astype(o_ref.dtype)


def gated_deltanet(r2, conv_w, a_log, dt_bias, norm_w):
    b, s, _ = r2.shape
    dk = GDN_HEAD_DIM
    nh = GDN_HEADS
    npair = s // GDN_PAIR
    nchunk = s // GDN_CHUNK
    rb = GDN_ROWS
    assert s % GDN_PAIR == 0 and s % rb == 0 and (R2_Z * LANES) % GDN_WIDTH == 0
    a0 = R2_SMALL * LANES + SMALL_A
    smt = jnp.swapaxes(r2[:, :, a0:a0 + 2 * nh], 1, 2)
    col = lambda blk: pl.BlockSpec((None, s, dk), lambda bi, h, blk=blk: (bi, 0, blk + h))
    cw = lambda blk: pl.BlockSpec((CONV_WIDTH, dk), lambda bi, h, blk=blk: (0, blk + h))
    vec = pl.BlockSpec((nh, 1), lambda bi, h: (0, 0))
    per_head = lambda *shape: pl.BlockSpec((None, None) + shape, lambda bi, h: (bi, h) + (0,) * len(shape))
    w, qg, u, qk, kdt, egl = pl.pallas_call(
        _gdn_prep_kernel,
        grid=(b, nh),
        in_specs=[col(R2_GQ), col(R2_GK), col(R2_GV),
                  pl.BlockSpec((None, 2 * nh, s), lambda bi, h: (bi, 0, 0)), vec, vec,
                  cw(0), cw(nh), cw(2 * nh)],
        out_specs=[per_head(s, dk), per_head(s, dk), per_head(s, dk), per_head(s, dk),
                   per_head(npair, GDN_PAIR, GDN_PAIR), per_head(nchunk, 8, dk)],
        out_shape=[jax.ShapeDtypeStruct((b, nh, s, dk), BF16), jax.ShapeDtypeStruct((b, nh, s, dk), BF16),
                   jax.ShapeDtypeStruct((b, nh, s, dk), F32), jax.ShapeDtypeStruct((b, nh, s, dk), BF16),
                   jax.ShapeDtypeStruct((b, nh, npair, GDN_PAIR, GDN_PAIR), BF16),
                   jax.ShapeDtypeStruct((b, nh, nchunk, 8, dk), F32)],
        scratch_shapes=[pltpu.VMEM((s, dk), BF16), pltpu.VMEM((s, dk), BF16), pltpu.VMEM((s, dk), BF16),
                        pltpu.VMEM((s, 2 * dk), BF16), pltpu.VMEM((s, dk), BF16), pltpu.VMEM((s, dk), F32),
                        pltpu.VMEM((npair, nh, GDN_PAIR), F32)],
        compiler_params=_cparams("parallel", "parallel"),
        name="gdn_prep",
    )(r2, r2, r2, smt, a_log.reshape(nh, 1), dt_bias.reshape(nh, 1), conv_w, conv_w, conv_w)

    seq = lambda *shape: pl.BlockSpec((None, nh) + shape, lambda bi, j: (bi, 0, j) + (0,) * (len(shape) - 1))
    return pl.pallas_call(
        _gdn_scan_kernel,
        grid=(b, s // rb),
        in_specs=[seq(rb, dk), seq(rb, dk), seq(rb, dk), seq(rb, dk),
                  seq(rb // GDN_PAIR, GDN_PAIR, GDN_PAIR), seq(rb // GDN_CHUNK, 8, dk),
                  pl.BlockSpec((None, rb, GDN_WIDTH), lambda bi, j: (bi, j, R2_Z * LANES // GDN_WIDTH)),
                  pl.BlockSpec((1, dk), lambda bi, j: (0, 0))],
        out_specs=pl.BlockSpec((None, rb, GDN_WIDTH), lambda bi, j: (bi, j, 0)),
        out_shape=jax.ShapeDtypeStruct((b, s, GDN_WIDTH), BF16),
        scratch_shapes=[pltpu.VMEM((nh, dk, dk), F32), pltpu.VMEM((rb, GDN_WIDTH), F32)],
        compiler_params=_cparams("parallel", "arbitrary"),
        name="gdn_scan",
    )(w, qg, u, qk, kdt, egl, r2, norm_w.reshape(1, dk))


def _split_w_in(w_in):
    o_q = 0
    o_kv = NSA_WIDTH
    o_gate = o_kv + 6 * NSA_KV_WIDTH
    o_qkv = o_gate + 3 * NSA_HEADS
    o_z = o_qkv + 3 * GDN_WIDTH
    o_a = o_z + GDN_WIDTH
    o_b = o_a + GDN_HEADS
    o_m = o_b + GDN_HEADS
    kvw = NSA_KV_WIDTH
    w1 = jnp.concatenate([w_in[:, o_q:o_kv], w_in[:, o_kv + 2 * kvw:o_kv + 6 * kvw]], axis=1)
    small = jnp.concatenate([w_in[:, o_gate:o_qkv], w_in[:, o_a:o_m]], axis=1)
    used = (R2_SMALL * LANES) + small.shape[1]
    w2 = jnp.concatenate([w_in[:, o_z:o_a], w_in[:, o_kv:o_kv + 2 * kvw], w_in[:, o_qkv:o_z], w_in[:, o_m:], small,
                          jnp.zeros((w_in.shape[0], R2_COLS - used), w_in.dtype)], axis=1)
    return w1.astype(BF16), w2.astype(BF16)


def _layer(x, bias_c, bias_d, ln1_w, w_in, pe_k, pe_v, w1_k, w2_k, w1_v, w2_v, conv_w, a_log, dt_bias,
           gdn_norm_w, w_pa, w_pb, w_o, ln2_w, w_up, w_down, final_norm_w, b, s):
    t = b * s
    w_r1, w_r2 = _split_w_in(w_in)
    h = rmsnorm(x, ln1_w, BF16)
    r1 = matmul(h, w_r1, BF16, tm=1024, tn=1024, tk=D_MODEL)
    r2 = matmul(h, w_r2, F32, tm=1024, tn=1024, tk=D_MODEL)
    r1 = r1.reshape(b, s, R1_COLS)
    r2 = r2.reshape(b, s, R2_COLS)
    kcvc = nsa_compress(r2, jnp.stack([pe_k, pe_v]), jnp.stack([w1_k, w1_v]).astype(BF16),
                        w2_k.astype(BF16), w2_v.T.astype(BF16))
    o_a = nsa_attention(r1, r2, kcvc, bias_c, bias_d).reshape(t, NSA_WIDTH)
    o_b = gated_deltanet(r2, conv_w, a_log, dt_bias, gdn_norm_w).reshape(t, GDN_WIDTH)
    m = merge_branches(o_a, o_b, w_pa.astype(BF16), w_pb.astype(BF16), r2.reshape(t, R2_COLS))
    x = matmul(m, w_o.astype(BF16), F32, tm=1024, tn=1024, tk=D_MODEL, epilogue="residual", res=x)
    h2 = rmsnorm(x, ln2_w, BF16)
    u = matmul(h2, w_up.astype(BF16), BF16, tm=1024, tn=1024, tk=D_MODEL, epilogue="relu2")
    if final_norm_w is None:
        return matmul(u, w_down.astype(BF16), F32, tm=512, tn=D_MODEL, tk=1024, epilogue="residual", res=x)
    return matmul(u, w_down.astype(BF16), F32, tm=512, tn=D_MODEL, tk=1024, epilogue="residual_norm", res=x,
                  norm_w=final_norm_w)


def kernel(x, rel_table, ln1_w, w_in, cmp_pe_k, cmp_pe_v, cmp_w1_k, cmp_w2_k, cmp_w1_v, cmp_w2_v, conv_w, a_log,
           dt_bias, gdn_norm_w, w_pa, w_pb, w_o, ln2_w, w_up, w_down, ln_f_w):
    b, s, d = x.shape
    depth = w_in.shape[0]
    bias_c, bias_d = bias_tables(rel_table, s)
    xt = x.reshape(b * s, d)
    for l in range(depth):
        xt = _layer(xt, bias_c, bias_d, ln1_w[l], w_in[l], cmp_pe_k[l], cmp_pe_v[l], cmp_w1_k[l], cmp_w2_k[l],
                    cmp_w1_v[l], cmp_w2_v[l], conv_w[l], a_log[l], dt_bias[l], gdn_norm_w[l], w_pa[l], w_pb[l],
                    w_o[l], ln2_w[l], w_up[l], w_down[l], ln_f_w if l == depth - 1 else None, b, s)
    return xt.reshape(b, s, d)
```

```python
import functools
import math

import numpy as np
import jax
import jax.numpy as jnp
from jax import lax
from jax.experimental import pallas as pl
from jax.experimental.pallas import tpu as pltpu

F32 = jnp.float32
BF16 = jnp.bfloat16
HIGHEST = lax.Precision.HIGHEST

D_MODEL = 2048
DEPTH = 2
NSA_HEADS = 8
NSA_KV_GROUPS = 2
NSA_HPG = NSA_HEADS // NSA_KV_GROUPS
NSA_HEAD_DIM = 128
NSA_WIDTH = NSA_HEADS * NSA_HEAD_DIM
NSA_KV_WIDTH = NSA_KV_GROUPS * NSA_HEAD_DIM
CMP_BLOCK = 32
CMP_STRIDE = 16
CMP_HIDDEN = 256
SEL_BLOCK = 64
SEL_TOP_N = 16
SEL_FORCE = 1000.0
WINDOW = 512
GDN_HEADS = 8
GDN_HEAD_DIM = 128
GDN_WIDTH = GDN_HEADS * GDN_HEAD_DIM
CONV_WIDTH = 4
GDN_CHUNK = 64
REL_BUCKETS = 32
REL_MAX_DIST = 128
D_FF = 4 * D_MODEL
NORM_EPS = 1e-6

LANES = 128
VMEM_LIMIT = 56 * 1024 * 1024
TQ = 256
TK = 256
NEG_INF = float("-inf")

R1_COLS = 2048
R1_KS, R1_VS, R1_KW, R1_VW = 8, 10, 12, 14
R2_Z, R2_KC, R2_VC, R2_GQ, R2_GK, R2_GV, R2_MA, R2_MB, R2_SMALL = 0, 8, 10, 12, 20, 28, 36, 52, 68
R2_COLS = 72 * LANES
SMALL_GATE, SMALL_A, SMALL_B = 0, 24, 32


def _t5_thresholds():
    exact = REL_BUCKETS // 2
    n = np.arange(exact, REL_MAX_DIST * 2, dtype=np.int64)
    nf = n.astype(np.float32)
    large = exact + (np.log(nf / np.float32(exact)) / np.float32(math.log(REL_MAX_DIST / exact))
                     * np.float32(REL_BUCKETS - exact)).astype(np.int32)
    large = np.minimum(large, REL_BUCKETS - 1)
    thr = []
    for b in range(exact + 1, REL_BUCKETS):
        thr.append(int(n[np.argmax(large >= b)]))
    return tuple(thr)


T5_THRESHOLDS = _t5_thresholds()


def _cparams(*sem):
    return pltpu.CompilerParams(dimension_semantics=sem, vmem_limit_bytes=VMEM_LIMIT)


def _rmsnorm_kernel(x_ref, w_ref, o_ref):
    x = x_ref[...]
    y = x * lax.rsqrt(jnp.mean(x * x, axis=-1, keepdims=True) + NORM_EPS)
    o_ref[...] = (y * w_ref[...]).astype(o_ref.dtype)


def rmsnorm(x, w, out_dtype, tm=512):
    t, d = x.shape
    return pl.pallas_call(
        _rmsnorm_kernel,
        grid=(t // tm,),
        in_specs=[pl.BlockSpec((tm, d), lambda i: (i, 0)), pl.BlockSpec((1, d), lambda i: (0, 0))],
        out_specs=pl.BlockSpec((tm, d), lambda i: (i, 0)),
        out_shape=jax.ShapeDtypeStruct((t, d), out_dtype),
        compiler_params=_cparams("parallel"),
        name="rmsnorm",
    )(x, w.reshape(1, d))


def _mm_kernel(*refs, nk, epilogue):
    a_ref, w_ref = refs[0], refs[1]
    pos = 2
    res_ref = norm_ref = None
    if epilogue in ("residual", "residual_norm"):
        res_ref = refs[pos]
        pos += 1
    if epilogue == "residual_norm":
        norm_ref = refs[pos]
        pos += 1
    o_ref = refs[pos]
    acc_ref = refs[pos + 1] if nk > 1 else None

    def finish(acc):
        if epilogue == "relu2":
            acc = jnp.square(jnp.maximum(acc, 0.0))
        elif epilogue == "residual":
            acc = res_ref[...] + acc
        elif epilogue == "residual_norm":
            y = res_ref[...] + acc
            acc = y * lax.rsqrt(jnp.mean(y * y, axis=-1, keepdims=True) + NORM_EPS) * norm_ref[...]
        o_ref[...] = acc.astype(o_ref.dtype)

    part = jnp.dot(a_ref[...], w_ref[...], preferred_element_type=F32)
    if nk == 1:
        finish(part)
    else:
        k = pl.program_id(2)

        @pl.when(k == 0)
        def _():
            acc_ref[...] = part

        @pl.when(k > 0)
        def _():
            acc_ref[...] += part

        @pl.when(k == nk - 1)
        def _():
            finish(acc_ref[...])


def matmul(a, w, out_dtype, *, tm, tn, tk, epilogue="none", res=None, norm_w=None):
    m, kdim = a.shape
    n = w.shape[1]
    nk = kdim // tk
    in_specs = [pl.BlockSpec((tm, tk), lambda i, j, k: (i, k)), pl.BlockSpec((tk, tn), lambda i, j, k: (k, j))]
    args = [a, w]
    if epilogue in ("residual", "residual_norm"):
        in_specs.append(pl.BlockSpec((tm, tn), lambda i, j, k: (i, j)))
        args.append(res)
    if epilogue == "residual_norm":
        assert tn == n
        in_specs.append(pl.BlockSpec((1, tn), lambda i, j, k: (0, 0)))
        args.append(norm_w.reshape(1, n))
    scratch = [pltpu.VMEM((tm, tn), F32)] if nk > 1 else []
    return pl.pallas_call(
        functools.partial(_mm_kernel, nk=nk, epilogue=epilogue),
        grid=(m // tm, n // tn, nk),
        in_specs=in_specs,
        out_specs=pl.BlockSpec((tm, tn), lambda i, j, k: (i, j)),
        out_shape=jax.ShapeDtypeStruct((m, n), out_dtype),
        scratch_shapes=scratch,
        compiler_params=_cparams("parallel", "parallel", "arbitrary"),
        name="matmul_" + epilogue,
    )(*args)


def _merge_kernel(oa_ref, ob_ref, wa_ref, wb_ref, ma_ref, mb_ref, o_ref):
    pa = jnp.dot(oa_ref[...], wa_ref[...], preferred_element_type=F32)
    pb = jnp.dot(ob_ref[...], wb_ref[...], preferred_element_type=F32)
    o_ref[...] = (jax.nn.sigmoid(ma_ref[...]) * pa + jax.nn.sigmoid(mb_ref[...]) * pb).astype(o_ref.dtype)


def merge_branches(o_a, o_b, w_pa, w_pb, r2, *, tm=1024, tn=512):
    t, ka = o_a.shape
    kb = o_b.shape[1]
    n = w_pa.shape[1]
    ma0 = R2_MA * LANES // tn
    mb0 = R2_MB * LANES // tn
    return pl.pallas_call(
        _merge_kernel,
        grid=(t // tm, n // tn),
        in_specs=[
            pl.BlockSpec((tm, ka), lambda i, j: (i, 0)),
            pl.BlockSpec((tm, kb), lambda i, j: (i, 0)),
            pl.BlockSpec((ka, tn), lambda i, j: (0, j)),
            pl.BlockSpec((kb, tn), lambda i, j: (0, j)),
            pl.BlockSpec((tm, tn), lambda i, j: (i, ma0 + j)),
            pl.BlockSpec((tm, tn), lambda i, j: (i, mb0 + j)),
        ],
        out_specs=pl.BlockSpec((tm, tn), lambda i, j: (i, j)),
        out_shape=jax.ShapeDtypeStruct((t, n), BF16),
        compiler_params=_cparams("parallel", "parallel"),
        name="merge_branches",
    )(o_a, o_b, w_pa, w_pb, r2, r2)


def _t5_lookup(dist, tab_ref, h):
    n = jnp.maximum(dist, 0)
    exact = REL_BUCKETS // 2
    large = jnp.full(n.shape, exact, jnp.int32)
    for thr in T5_THRESHOLDS:
        large = large + (n >= thr).astype(jnp.int32)
    bucket = jnp.where(n < exact, n, large)
    out = jnp.zeros(n.shape, F32)
    for b in range(REL_BUCKETS):
        out = jnp.where(bucket == b, tab_ref[b, h], out)
    return out


def _bias_kernel(tab_ref, bc_ref, bd_ref):
    g = pl.program_id(0)
    i = pl.program_id(1)
    ncr = bc_ref.shape[0]
    blk = lax.broadcasted_iota(jnp.int32, (ncr, TQ), 0)
    qry_c = lax.broadcasted_iota(jnp.int32, (ncr, TQ), 1)
    for j in range(NSA_HPG):
        h = g * NSA_HPG + j
        bc_ref[:, j * TQ:(j + 1) * TQ] = _t5_lookup(i * TQ + qry_c - (blk * CMP_STRIDE + CMP_BLOCK - 1), tab_ref, h)

    @pl.when(i == 0)
    def _():
        key = lax.broadcasted_iota(jnp.int32, (TK, TQ), 0)
        qry = lax.broadcasted_iota(jnp.int32, (TK, TQ), 1)
        for v in range(3):
            for j in range(NSA_HPG):
                bd_ref[v, :, j * TQ:(j + 1) * TQ] = _t5_lookup(v * TK + qry - key, tab_ref, g * NSA_HPG + j)


def bias_tables(rel_table, s):
    nq = s // TQ
    cols = NSA_HPG * TQ
    return pl.pallas_call(
        _bias_kernel,
        grid=(NSA_KV_GROUPS, nq),
        in_specs=[pl.BlockSpec(memory_space=pltpu.SMEM)],
        out_specs=[pl.BlockSpec((None, None, s // CMP_STRIDE, cols), lambda g, i: (g, i, 0, 0)),
                   pl.BlockSpec((None, 3, TK, cols), lambda g, i: (g, 0, 0, 0))],
        out_shape=[jax.ShapeDtypeStruct((NSA_KV_GROUPS, nq, s // CMP_STRIDE, cols), F32),
                   jax.ShapeDtypeStruct((NSA_KV_GROUPS, 3, TK, cols), F32)],
        compiler_params=_cparams("arbitrary", "arbitrary"),
        name="t5_bias_tables",
    )(rel_table)


def _gelu_tanh(x):
    return 0.5 * x * (1.0 + jnp.tanh(math.sqrt(2.0 / math.pi) * (x + 0.044715 * (x * x * x))))


def _compress_kernel(x_ref, pe_ref, w1_ref, w2k_ref, w2vt_ref, o_ref, *, nblk):
    pe = pe_ref[...]
    acc_a = jnp.zeros((nblk, CMP_HIDDEN), F32)
    acc_b = jnp.zeros((nblk, CMP_HIDDEN), F32)
    for r in range(CMP_STRIDE):
        xr = x_ref[pl.ds(r, nblk, stride=CMP_STRIDE), :]
        xa = (xr + pe[r:r + 1, :]).astype(BF16)
        xb = (xr + pe[CMP_STRIDE + r:CMP_STRIDE + r + 1, :]).astype(BF16)
        acc_a = acc_a + jnp.dot(xa, w1_ref[r * NSA_HEAD_DIM:(r + 1) * NSA_HEAD_DIM, :], preferred_element_type=F32)
        acc_b = acc_b + jnp.dot(xb, w1_ref[(CMP_STRIDE + r) * NSA_HEAD_DIM:(CMP_STRIDE + r + 1) * NSA_HEAD_DIM, :],
                                preferred_element_type=F32)
    hidden = acc_a + jnp.concatenate([acc_b[1:], acc_b[:1]], axis=0)
    act = _gelu_tanh(hidden).astype(BF16)

    @pl.when(pl.program_id(2) == 0)
    def _():
        o_ref[...] = jnp.dot(act, w2k_ref[...], preferred_element_type=F32).astype(o_ref.dtype)

    @pl.when(pl.program_id(2) == 1)
    def _():
        o_ref[...] = lax.dot_general(w2vt_ref[...], act, (((1,), (1,)), ((), ())),
                                     preferred_element_type=F32).astype(o_ref.dtype)


def nsa_compress(r2, pe, w1, w2k, w2vt):
    b, s, _ = r2.shape
    nblk = s // CMP_STRIDE
    assert CMP_BLOCK == 2 * CMP_STRIDE and nblk == NSA_HEAD_DIM
    return pl.pallas_call(
        functools.partial(_compress_kernel, nblk=nblk),
        grid=(b, NSA_KV_GROUPS, 2),
        in_specs=[
            pl.BlockSpec((None, s, LANES), lambda bi, g, c: (bi, 0, R2_KC + 2 * c + g)),
            pl.BlockSpec((None, CMP_BLOCK, NSA_HEAD_DIM), lambda bi, g, c: (c, 0, 0)),
            pl.BlockSpec((None, CMP_BLOCK * NSA_HEAD_DIM, CMP_HIDDEN), lambda bi, g, c: (c, 0, 0)),
            pl.BlockSpec((CMP_HIDDEN, NSA_HEAD_DIM), lambda bi, g, c: (0, 0)),
            pl.BlockSpec((NSA_HEAD_DIM, CMP_HIDDEN), lambda bi, g, c: (0, 0)),
        ],
        out_specs=pl.BlockSpec((None, None, None, nblk, NSA_HEAD_DIM), lambda bi, g, c: (bi, g, c, 0, 0)),
        out_shape=jax.ShapeDtypeStruct((b, NSA_KV_GROUPS, 2, nblk, NSA_HEAD_DIM), BF16),
        compiler_params=_cparams("parallel", "parallel", "arbitrary"),
        name="nsa_compress",
    )(r2, pe, w1, w2k, w2vt)


def _nsa_attn_kernel(q_ref, ks_ref, kw_ref, vst_ref, vwt_ref, kcvc_ref, gate_ref, biasc_ref, biasd_ref,
                     covert_ref, expandt_ref, o_ref, sel_ref, m_ref, l_ref, acc_ref, res_ref, *, n_sel, top_n):
    g = pl.program_id(1)
    i = pl.program_id(2)
    cols = NSA_HPG * TQ
    scale = NSA_HEAD_DIM ** -0.5
    nt = (((1,), (1,)), ((), ()))

    qf = q_ref[...]
    qs = jnp.concatenate([qf[:, j * NSA_HEAD_DIM:(j + 1) * NSA_HEAD_DIM] for j in range(NSA_HPG)], axis=0)
    key = lax.broadcasted_iota(jnp.int32, (TK, cols), 0)
    t_loc = lax.broadcasted_iota(jnp.int32, (TK, cols), 1) % TQ

    def gate_row(c):
        return jnp.concatenate(
            [jax.nn.sigmoid(gate_ref[pl.ds((g * NSA_HPG + j) * 3 + c, 1), :]) for j in range(NSA_HPG)], axis=1)

    def lanes4(x):
        return jnp.concatenate([x] * NSA_HPG, axis=1)

    s = lax.dot_general(kcvc_ref[0], qs, nt, preferred_element_type=F32) * scale + biasc_ref[...]
    cmp_end = lax.broadcasted_iota(jnp.int32, s.shape, 0) * CMP_STRIDE + (CMP_BLOCK - 1)
    s = jnp.where(cmp_end <= i * TQ + lax.broadcasted_iota(jnp.int32, s.shape, 1) % TQ, s, NEG_INF)
    mx = jnp.max(s, axis=0, keepdims=True)
    mx = jnp.where(mx == NEG_INF, 0.0, mx)
    e = jnp.exp(s - mx)
    p = e / jnp.maximum(jnp.sum(e, axis=0, keepdims=True), 1e-30)
    o_c = jnp.dot(kcvc_ref[1], p.astype(BF16), preferred_element_type=F32)
    res_ref[...] = gate_row(0) * o_c

    psum = p[:, 0:TQ]
    for j in range(1, NSA_HPG):
        psum = psum + p[:, j * TQ:(j + 1) * TQ]
    imp = jnp.dot(covert_ref[...], psum, preferred_element_type=F32, precision=HIGHEST)[:n_sel]
    blk = lax.broadcasted_iota(jnp.int32, (n_sel, TQ), 0)
    cur = (i * TQ + lax.broadcasted_iota(jnp.int32, (n_sel, TQ), 1)) // SEL_BLOCK
    causal_blk = blk <= cur
    forced = (blk == 0) | (blk == cur) | (blk == cur - 1)
    score = jnp.where(causal_blk, imp + jnp.where(forced, SEL_FORCE, 0.0), NEG_INF)
    rank = jnp.zeros((n_sel, TQ), jnp.int32)
    for mp in range(n_sel):
        other = score[mp:mp + 1, :]
        beats = (other > score) | ((other == score) & (blk > mp))
        rank = rank + beats.astype(jnp.int32)
    sel = jnp.where((rank < top_n) & causal_blk, 1.0, 0.0)
    sel = jnp.concatenate([sel, jnp.zeros((LANES - n_sel, TQ), F32)], axis=0).astype(BF16)
    sel_ref[...] = jnp.where(jnp.dot(expandt_ref[...], sel, preferred_element_type=F32) > 0.5, 0.0, NEG_INF)

    def reset():
        m_ref[...] = jnp.full(m_ref.shape, NEG_INF, F32)
        l_ref[...] = jnp.zeros(l_ref.shape, F32)
        acc_ref[...] = jnp.zeros(acc_ref.shape, F32)

    def attend(k_tile, vt_tile, variant, mask_add):
        s = lax.dot_general(k_tile, qs, nt, preferred_element_type=F32) * scale + biasd_ref[variant]
        if mask_add is not None:
            s = s + mask_add
        m_prev = m_ref[...]
        m_new = jnp.maximum(m_prev, jnp.max(s, axis=0, keepdims=True))
        m_safe = jnp.where(m_new == NEG_INF, 0.0, m_new)
        alpha = jnp.exp(m_prev - m_safe)
        p = jnp.exp(s - m_safe)
        l_ref[...] = alpha * l_ref[...] + jnp.sum(p, axis=0, keepdims=True)
        acc_ref[...] = alpha * acc_ref[...] + jnp.dot(vt_tile, p.astype(BF16), preferred_element_type=F32)
        m_ref[...] = m_new

    def finish():
        return acc_ref[...] * (1.0 / jnp.maximum(l_ref[...], 1e-30))

    def tile(ref, k):
        return ref[pl.ds(pl.multiple_of(k * TK, TK), TK), :]

    causal_add = jnp.where(key <= t_loc, 0.0, NEG_INF)

    reset()

    def sel_far(k, carry):
        attend(tile(ks_ref, k), vst_ref[k], 2, lanes4(tile(sel_ref, k)))
        return carry

    lax.fori_loop(0, jnp.maximum(i - 1, 0), sel_far, 0)

    @pl.when(i >= 1)
    def _():
        attend(tile(ks_ref, i - 1), vst_ref[i - 1], 1, lanes4(tile(sel_ref, i - 1)))

    attend(tile(ks_ref, i), vst_ref[i], 0, lanes4(tile(sel_ref, i)) + causal_add)
    res_ref[...] += gate_row(1) * finish()

    reset()
    n_win = WINDOW // TK
    for d in range(n_win, 0, -1):
        @pl.when(i >= d)
        def _(d=d):
            edge_add = jnp.where(key > t_loc, 0.0, NEG_INF) if d == n_win else None
            attend(tile(kw_ref, i - d), vwt_ref[i - d], min(d, 2), edge_add)

    attend(tile(kw_ref, i), vwt_ref[i], 0, causal_add)
    res = res_ref[...] + gate_row(2) * finish()
    for j in range(NSA_HPG):
        o_ref[:, j * NSA_HEAD_DIM:(j + 1) * NSA_HEAD_DIM] = res[:, j * TQ:(j + 1) * TQ].T.astype(o_ref.dtype)


def nsa_attention(r1, r2, kcvc, bias_c, bias_d):
    b, s, _ = r1.shape
    n_cmp_rows = s // CMP_STRIDE
    n_sel = s // SEL_BLOCK
    assert n_cmp_rows % LANES == 0 and n_sel <= LANES and n_sel % 8 == 0 and WINDOW % TK == 0 and TQ == TK
    assert 2 * TK - (TQ - 1) >= REL_MAX_DIST
    top_n = min(SEL_TOP_N, n_sel)
    nq = s // TQ
    nk = s // TK
    gw = NSA_HPG * NSA_HEAD_DIM
    cols = NSA_HPG * TQ
    dh = NSA_HEAD_DIM

    n_cmp = n_cmp_rows - CMP_BLOCK // CMP_STRIDE + 1
    c_start = np.arange(n_cmp_rows) * CMP_STRIDE
    s_start = np.arange(LANES) * SEL_BLOCK
    cover_t = ((c_start[None, :] <= s_start[:, None] + SEL_BLOCK - 1) & (c_start[None, :] + CMP_BLOCK - 1 >= s_start[:, None])
               & (np.arange(n_cmp_rows)[None, :] < n_cmp) & (np.arange(LANES)[:, None] < n_sel))
    cover_t = jnp.asarray(cover_t.astype(np.float32))
    expand_t = jnp.asarray(((np.arange(s)[:, None] // SEL_BLOCK) == np.arange(LANES)[None, :]).astype(np.float32), BF16)

    v0 = R1_VS * LANES
    v1 = R1_VW * LANES
    vt = jnp.stack([r1[:, :, v0:v0 + NSA_KV_WIDTH], r1[:, :, v1:v1 + NSA_KV_WIDTH]], axis=1)
    vt = vt.reshape(b, 2, nk, TK, NSA_KV_GROUPS, dh).transpose(0, 1, 4, 2, 5, 3)
    g0 = R2_SMALL * LANES + SMALL_GATE
    gate_t = jnp.swapaxes(r2[:, :, g0:g0 + 3 * NSA_HEADS], 1, 2)

    k_spec = lambda blk: pl.BlockSpec((None, s, LANES), lambda bi, g, i, blk=blk: (bi, 0, blk + g))
    vt_spec = lambda w: pl.BlockSpec((None, None, None, nk, dh, TK), lambda bi, g, i, w=w: (bi, w, g, 0, 0, 0))
    return pl.pallas_call(
        functools.partial(_nsa_attn_kernel, n_sel=n_sel, top_n=top_n),
        grid=(b, NSA_KV_GROUPS, nq),
        in_specs=[
            pl.BlockSpec((None, TQ, gw), lambda bi, g, i: (bi, i, g)),
            k_spec(R1_KS), k_spec(R1_KW), vt_spec(0), vt_spec(1),
            pl.BlockSpec((None, None, 2, n_cmp_rows, dh), lambda bi, g, i: (bi, g, 0, 0, 0)),
            pl.BlockSpec((None, 3 * NSA_HEADS, TQ), lambda bi, g, i: (bi, 0, i)),
            pl.BlockSpec((None, None, n_cmp_rows, cols), lambda bi, g, i: (g, i, 0, 0)),
            pl.BlockSpec((None, 3, TK, cols), lambda bi, g, i: (g, 0, 0, 0)),
            pl.BlockSpec((LANES, n_cmp_rows), lambda bi, g, i: (0, 0)),
            pl.BlockSpec((s, LANES), lambda bi, g, i: (0, 0)),
        ],
        out_specs=pl.BlockSpec((None, TQ, gw), lambda bi, g, i: (bi, i, g)),
        out_shape=jax.ShapeDtypeStruct((b, s, NSA_WIDTH), BF16),
        scratch_shapes=[
            pltpu.VMEM((s, TQ), F32),
            pltpu.VMEM((1, cols), F32),
            pltpu.VMEM((1, cols), F32),
            pltpu.VMEM((dh, cols), F32),
            pltpu.VMEM((dh, cols), F32),
        ],
        compiler_params=_cparams("parallel", "parallel", "arbitrary"),
        name="nsa_attention",
    )(r1, r1, r1, vt, vt, kcvc, gate_t, bias_c, bias_d, cover_t, expand_t)


GDN_PAIR = 2 * GDN_CHUNK
GDN_GROUP = 4
GDN_ROWS = 512


def _gdn_prep_kernel(q_ref, k_ref, v_ref, smt_ref, alog_ref, dtb_ref, cwq_ref, cwk_ref, cwv_ref,
                     w_out, qg_out, u_out, qk_out, kdt_out, egl_out,
                     kb_s, kk_s, qq_s, vk_s, kd_s, gcol_s, grow_s):
    h = pl.program_id(1)
    s = q_ref.shape[0]
    c = GDN_CHUNK
    pr = GDN_PAIR
    dk = GDN_HEAD_DIM
    nt = (((1,), (1,)), ((), ()))
    tn = (((0,), (0,)), ((), ()))

    xg = smt_ref[0:GDN_HEADS, :] + dtb_ref[...]
    g8 = -jnp.exp(alog_ref[...]) * (jnp.maximum(xg, 0.0) + jnp.log1p(jnp.exp(-jnp.abs(xg))))
    beta8 = jax.nn.sigmoid(smt_ref[GDN_HEADS:2 * GDN_HEADS, :])
    pos = lax.broadcasted_iota(jnp.int32, g8.shape, 1) % c
    gcum8 = g8
    sh = 1
    while sh < c:
        gcum8 = gcum8 + jnp.where(pos >= sh, pltpu.roll(gcum8, sh, axis=1), 0.0)
        sh *= 2
    for p in range(s // pr):
        grow_s[p] = gcum8[:, p * pr:(p + 1) * pr]
    sel_k = lax.broadcasted_iota(jnp.int32, (2 * GDN_HEADS, 2 * dk), 0)
    sel_l = lax.broadcasted_iota(jnp.int32, (2 * GDN_HEADS, 2 * dk), 1)
    pick = jnp.where(((sel_k == h) & (sel_l < dk)) | ((sel_k == GDN_HEADS + h) & (sel_l >= dk)), 1.0, 0.0)
    both = jnp.concatenate([gcum8, beta8], axis=0)
    hi = both.astype(BF16)
    rest = both - hi.astype(F32)
    mid = rest.astype(BF16)
    lo = (rest - mid.astype(F32)).astype(BF16)
    col = lax.dot_general(jnp.concatenate([hi, mid, lo], axis=0), jnp.concatenate([pick] * 3, axis=0).astype(BF16),
                          tn, preferred_element_type=F32)
    gcol = col[:, :dk]
    bcol = col[:, dk:]
    gcol_s[...] = gcol

    row8 = lax.broadcasted_iota(jnp.int32, (8, dk), 0)

    def conv_silu(x_ref, w_ref):
        x = x_ref[...]
        w = w_ref[...]
        y = x * w[CONV_WIDTH - 1:CONV_WIDTH, :]
        for sft in range(1, CONV_WIDTH):
            y = y + pltpu.roll(x, sft, axis=0) * w[CONV_WIDTH - 1 - sft:CONV_WIDTH - sft, :]
        x0 = x[:8]
        y0 = x0 * w[CONV_WIDTH - 1:CONV_WIDTH, :]
        for sft in range(1, CONV_WIDTH):
            y0 = y0 + jnp.where(row8 >= sft, pltpu.roll(x0, sft, axis=0), 0.0) * w[CONV_WIDTH - 1 - sft:CONV_WIDTH - sft, :]
        y = jnp.concatenate([y0, y[8:]], axis=0)
        return y * jax.nn.sigmoid(y)

    def l2n(x):
        return x * lax.rsqrt(jnp.sum(x * x, axis=-1, keepdims=True) + NORM_EPS)

    q = l2n(conv_silu(q_ref, cwq_ref)) * (dk ** -0.5)
    k = l2n(conv_silu(k_ref, cwk_ref))
    v = conv_silu(v_ref, cwv_ref)
    eg = jnp.exp(gcol)
    kb = k * bcol
    qg_out[...] = (q * eg).astype(BF16)
    kk_s[...] = k.astype(BF16)
    kb_s[...] = kb.astype(BF16)
    qq_s[...] = q.astype(BF16)
    vk_s[:, :dk] = (v * bcol).astype(BF16)
    vk_s[:, dk:] = (kb * eg).astype(BF16)
    g3 = gcol.reshape(s // c, c, dk)
    g_last = g3[:, c - 1:c, :]
    kd_s[...] = (k * jnp.exp(jnp.broadcast_to(g_last, g3.shape).reshape(s, dk) - gcol)).astype(BF16)
    egl_out[...] = jnp.broadcast_to(jnp.exp(g_last), egl_out.shape)

    r_i = lax.broadcasted_iota(jnp.int32, (pr, pr), 0)
    l_i = lax.broadcasted_iota(jnp.int32, (pr, pr), 1)
    tri = ((r_i // c) == (l_i // c)) & (r_i >= l_i)
    eye = r_i == l_i
    eye_f = jnp.where(eye, 1.0, 0.0)
    eye_b = eye_f.astype(BF16)

    def pair_group(ps):
        rows = [slice(p * pr, (p + 1) * pr) for p in ps]
        n = len(ps)
        x = [lax.dot_general(jnp.concatenate([kb_s[r, :], qq_s[r, :]], axis=0), kk_s[r, :], nt,
                             preferred_element_type=F32) for r in rows]
        decay = []
        for p, r in zip(ps, rows):
            diff = gcol_s[r, :] - grow_s[p, pl.ds(h, 1), :]
            decay.append(jnp.where(tri, jnp.exp(jnp.where(tri, diff, 0.0)), 0.0))
        for a in range(n):
            qk_out[rows[a], :] = (x[a][pr:] * decay[a]).astype(BF16)
            kdt_out[ps[a]] = lax.dot_general(eye_b, kd_s[rows[a], :], nt, preferred_element_type=F32).astype(BF16)
        qm = [jnp.where(eye, 0.0, -(x[a][:pr] * decay[a])) for a in range(n)]
        tinv = [eye_f + qm[a] for a in range(n)]
        qb = [qm[a].astype(BF16) for a in range(n)]
        qm = [jnp.dot(qb[a], qb[a], preferred_element_type=F32) for a in range(n)]
        steps = int(math.log2(c)) - 1
        for it in range(steps):
            qb = [qm[a].astype(BF16) for a in range(n)]
            if it < steps - 1:
                r = [jnp.dot(jnp.concatenate([qb[a], tinv[a].astype(BF16)], axis=0), qb[a],
                             preferred_element_type=F32) for a in range(n)]
                qm = [r[a][:pr] for a in range(n)]
                tinv = [tinv[a] + r[a][pr:] for a in range(n)]
            else:
                tinv = [tinv[a] + jnp.dot(tinv[a].astype(BF16), qb[a], preferred_element_type=F32) for a in range(n)]
        for a in range(n):
            uw = jnp.dot(tinv[a].astype(BF16), vk_s[rows[a], :], preferred_element_type=F32)
            u_out[rows[a], :] = uw[:, :dk]
            w_out[rows[a], :] = uw[:, dk:].astype(BF16)

    npair = s // pr
    for p0 in range(0, npair, GDN_GROUP):
        pair_group(list(range(p0, min(p0 + GDN_GROUP, npair))))


def _gdn_scan_kernel(w_ref, qg_ref, u_ref, qk_ref, kdt_ref, egl_ref, z_ref, nw_ref, o_ref, state_ref, os_ref):
    c = GDN_CHUNK
    pr = GDN_PAIR
    dk = GDN_HEAD_DIM
    nh = w_ref.shape[0]
    rb = w_ref.shape[1]

    @pl.when(pl.program_id(1) == 0)
    def _():
        state_ref[...] = jnp.zeros(state_ref.shape, F32)

    zeros = jnp.zeros((c, dk), BF16)

    def pair(p, carry):
        r0 = pl.multiple_of(p * pr, pr)
        for half in range(2):
            rows = pl.ds(r0 + half * c, c)
            state = [state_ref[h] for h in range(nh)]
            r1 = [jnp.dot(jnp.concatenate([w_ref[h, rows, :], qg_ref[h, rows, :]], axis=0), state[h].astype(BF16),
                          preferred_element_type=F32) for h in range(nh)]
            vnb = [(u_ref[h, rows, :] - r1[h][:c]).astype(BF16) for h in range(nh)]
            rhs = [jnp.concatenate([vnb[h], zeros] if half == 0 else [zeros, vnb[h]], axis=0) for h in range(nh)]
            r2 = [jnp.dot(jnp.concatenate([qk_ref[h, rows, :], kdt_ref[h, p]], axis=0), rhs[h],
                          preferred_element_type=F32) for h in range(nh)]
            for h in range(nh):
                os_ref[rows, h * dk:(h + 1) * dk] = r1[h][c:] + r2[h][:c]
                state_ref[h] = state[h] * egl_ref[h, 2 * p + half, 0:1, :] + r2[h][c:]
        return carry

    lax.fori_loop(0, rb // pr, pair, 0)

    for h in range(nh):
        o = os_ref[:, h * dk:(h + 1) * dk]
        o = o * lax.rsqrt(jnp.mean(o * o, axis=-1, keepdims=True) + NORM_EPS) * nw_ref[...]
        z = z_ref[:, h * dk:(h + 1) * dk]
        o_ref[:, h * dk:(h + 1) * dk] = (o * (z * jax.nn.sigmoid(z))).astype(o_ref.dtype)


def gated_deltanet(r2, conv_w, a_log, dt_bias, norm_w):
    b, s, _ = r2.shape
    dk = GDN_HEAD_DIM
    nh = GDN_HEADS
    npair = s // GDN_PAIR
    nchunk = s // GDN_CHUNK
    rb = GDN_ROWS
    assert s % GDN_PAIR == 0 and s % rb == 0 and (R2_Z * LANES) % GDN_WIDTH == 0
    a0 = R2_SMALL * LANES + SMALL_A
    smt = jnp.swapaxes(r2[:, :, a0:a0 + 2 * nh], 1, 2)
    col = lambda blk: pl.BlockSpec((None, s, dk), lambda bi, h, blk=blk: (bi, 0, blk + h))
    cw = lambda blk: pl.BlockSpec((CONV_WIDTH, dk), lambda bi, h, blk=blk: (0, blk + h))
    vec = pl.BlockSpec((nh, 1), lambda bi, h: (0, 0))
    per_head = lambda *shape: pl.BlockSpec((None, None) + shape, lambda bi, h: (bi, h) + (0,) * len(shape))
    w, qg, u, qk, kdt, egl = pl.pallas_call(
        _gdn_prep_kernel,
        grid=(b, nh),
        in_specs=[col(R2_GQ), col(R2_GK), col(R2_GV),
                  pl.BlockSpec((None, 2 * nh, s), lambda bi, h: (bi, 0, 0)), vec, vec,
                  cw(0), cw(nh), cw(2 * nh)],
        out_specs=[per_head(s, dk), per_head(s, dk), per_head(s, dk), per_head(s, dk),
                   per_head(npair, GDN_PAIR, GDN_PAIR), per_head(nchunk, 8, dk)],
        out_shape=[jax.ShapeDtypeStruct((b, nh, s, dk), BF16), jax.ShapeDtypeStruct((b, nh, s, dk), BF16),
                   jax.ShapeDtypeStruct((b, nh, s, dk), F32), jax.ShapeDtypeStruct((b, nh, s, dk), BF16),
                   jax.ShapeDtypeStruct((b, nh, npair, GDN_PAIR, GDN_PAIR), BF16),
                   jax.ShapeDtypeStruct((b, nh, nchunk, 8, dk), F32)],
        scratch_shapes=[pltpu.VMEM((s, dk), BF16), pltpu.VMEM((s, dk), BF16), pltpu.VMEM((s, dk), BF16),
                        pltpu.VMEM((s, 2 * dk), BF16), pltpu.VMEM((s, dk), BF16), pltpu.VMEM((s, dk), F32),
                        pltpu.VMEM((npair, nh, GDN_PAIR), F32)],
        compiler_params=_cparams("parallel", "parallel"),
        name="gdn_prep",
    )(r2, r2, r2, smt, a_log.reshape(nh, 1), dt_bias.reshape(nh, 1), conv_w, conv_w, conv_w)

    seq = lambda *shape: pl.BlockSpec((None, nh) + shape, lambda bi, j: (bi, 0, j) + (0,) * (len(shape) - 1))
    return pl.pallas_call(
        _gdn_scan_kernel,
        grid=(b, s // rb),
        in_specs=[seq(rb, dk), seq(rb, dk), seq(rb, dk), seq(rb, dk),
                  seq(rb // GDN_PAIR, GDN_PAIR, GDN_PAIR), seq(rb // GDN_CHUNK, 8, dk),
                  pl.BlockSpec((None, rb, GDN_WIDTH), lambda bi, j: (bi, j, R2_Z * LANES // GDN_WIDTH)),
                  pl.BlockSpec((1, dk), lambda bi, j: (0, 0))],
        out_specs=pl.BlockSpec((None, rb, GDN_WIDTH), lambda bi, j: (bi, j, 0)),
        out_shape=jax.ShapeDtypeStruct((b, s, GDN_WIDTH), BF16),
        scratch_shapes=[pltpu.VMEM((nh, dk, dk), F32), pltpu.VMEM((rb, GDN_WIDTH), F32)],
        compiler_params=_cparams("parallel", "arbitrary"),
        name="gdn_scan",
    )(w, qg, u, qk, kdt, egl, r2, norm_w.reshape(1, dk))


def _split_w_in(w_in):
    o_q = 0
    o_kv = NSA_WIDTH
    o_gate = o_kv + 6 * NSA_KV_WIDTH
    o_qkv = o_gate + 3 * NSA_HEADS
    o_z = o_qkv + 3 * GDN_WIDTH
    o_a = o_z + GDN_WIDTH
    o_b = o_a + GDN_HEADS
    o_m = o_b + GDN_HEADS
    kvw = NSA_KV_WIDTH
    w1 = jnp.concatenate([w_in[:, o_q:o_kv], w_in[:, o_kv + 2 * kvw:o_kv + 6 * kvw]], axis=1)
    small = jnp.concatenate([w_in[:, o_gate:o_qkv], w_in[:, o_a:o_m]], axis=1)
    used = (R2_SMALL * LANES) + small.shape[1]
    w2 = jnp.concatenate([w_in[:, o_z:o_a], w_in[:, o_kv:o_kv + 2 * kvw], w_in[:, o_qkv:o_z], w_in[:, o_m:], small,
                          jnp.zeros((w_in.shape[0], R2_COLS - used), w_in.dtype)], axis=1)
    return w1.astype(BF16), w2.astype(BF16)


def _layer(x, bias_c, bias_d, ln1_w, w_in, pe_k, pe_v, w1_k, w2_k, w1_v, w2_v, conv_w, a_log, dt_bias,
           gdn_norm_w, w_pa, w_pb, w_o, ln2_w, w_up, w_down, final_norm_w, b, s):
    t = b * s
    w_r1, w_r2 = _split_w_in(w_in)
    h = rmsnorm(x, ln1_w, BF16)
    r1 = matmul(h, w_r1, BF16, tm=1024, tn=1024, tk=D_MODEL)
    r2 = matmul(h, w_r2, F32, tm=1024, tn=1024, tk=D_MODEL)
    r1 = r1.reshape(b, s, R1_COLS)
    r2 = r2.reshape(b, s, R2_COLS)
    kcvc = nsa_compress(r2, jnp.stack([pe_k, pe_v]), jnp.stack([w1_k, w1_v]).astype(BF16),
                        w2_k.astype(BF16), w2_v.T.astype(BF16))
    o_a = nsa_attention(r1, r2, kcvc, bias_c, bias_d).reshape(t, NSA_WIDTH)
    o_b = gated_deltanet(r2, conv_w, a_log, dt_bias, gdn_norm_w).reshape(t, GDN_WIDTH)
    m = merge_branches(o_a, o_b, w_pa.astype(BF16), w_pb.astype(BF16), r2.reshape(t, R2_COLS))
    x = matmul(m, w_o.astype(BF16), F32, tm=1024, tn=1024, tk=D_MODEL, epilogue="residual", res=x)
    h2 = rmsnorm(x, ln2_w, BF16)
    u = matmul(h2, w_up.astype(BF16), BF16, tm=1024, tn=1024, tk=D_MODEL, epilogue="relu2")
    if final_norm_w is None:
        return matmul(u, w_down.astype(BF16), F32, tm=512, tn=D_MODEL, tk=2048, epilogue="residual", res=x)
    return matmul(u, w_down.astype(BF16), F32, tm=512, tn=D_MODEL, tk=2048, epilogue="residual_norm", res=x,
                  norm_w=final_norm_w)


def kernel(x, rel_table, ln1_w, w_in, cmp_pe_k, cmp_pe_v, cmp_w1_k, cmp_w2_k, cmp_w1_v, cmp_w2_v, conv_w, a_log,
           dt_bias, gdn_norm_w, w_pa, w_pb, w_o, ln2_w, w_up, w_down, ln_f_w):
    b, s, d = x.shape
    depth = w_in.shape[0]
    bias_c, bias_d = bias_tables(rel_table, s)
    xt = x.reshape(b * s, d)
    for l in range(depth):
        xt = _layer(xt, bias_c, bias_d, ln1_w[l], w_in[l], cmp_pe_k[l], cmp_pe_v[l], cmp_w1_k[l], cmp_w2_k[l],
                    cmp_w1_v[l], cmp_w2_v[l], conv_w[l], a_log[l], dt_bias[l], gdn_norm_w[l], w_pa[l], w_pb[l],
                    w_o[l], ln2_w[l], w_up[l], w_down[l], ln_f_w if l == depth - 1 else None, b, s)
    return xt.reshape(b, s, d)
```

```python
import functools
import math

import numpy as np
import jax
import jax.numpy as jnp
from jax import lax
from jax.experimental import pallas as pl
from jax.experimental.pallas import tpu as pltpu

F32 = jnp.float32
BF16 = jnp.bfloat16
HIGHEST = lax.Precision.HIGHEST

D_MODEL = 2048
DEPTH = 2
NSA_HEADS = 8
NSA_KV_GROUPS = 2
NSA_HPG = NSA_HEADS // NSA_KV_GROUPS
NSA_HEAD_DIM = 128
NSA_WIDTH = NSA_HEADS * NSA_HEAD_DIM
NSA_KV_WIDTH = NSA_KV_GROUPS * NSA_HEAD_DIM
CMP_BLOCK = 32
CMP_STRIDE = 16
CMP_HIDDEN = 256
SEL_BLOCK = 64
SEL_TOP_N = 16
SEL_FORCE = 1000.0
WINDOW = 512
GDN_HEADS = 8
GDN_HEAD_DIM = 128
GDN_WIDTH = GDN_HEADS * GDN_HEAD_DIM
CONV_WIDTH = 4
GDN_CHUNK = 64
REL_BUCKETS = 32
REL_MAX_DIST = 128
D_FF = 4 * D_MODEL
NORM_EPS = 1e-6

LANES = 128
VMEM_LIMIT = 56 * 1024 * 1024
TQ = 256
TK = 256
NEG_INF = float("-inf")
LOG2E = math.log2(math.e)

R1_COLS = 2048
R1_KS, R1_VS, R1_KW, R1_VW = 8, 10, 12, 14
R2_Z, R2_KC, R2_VC, R2_GQ, R2_GK, R2_GV, R2_MA, R2_MB, R2_SMALL = 0, 8, 10, 12, 20, 28, 36, 52, 68
R2_COLS = 72 * LANES
SMALL_GATE, SMALL_A, SMALL_B = 0, 24, 32


def _t5_thresholds():
    exact = REL_BUCKETS // 2
    n = np.arange(exact, REL_MAX_DIST * 2, dtype=np.int64)
    nf = n.astype(np.float32)
    large = exact + (np.log(nf / np.float32(exact)) / np.float32(math.log(REL_MAX_DIST / exact))
                     * np.float32(REL_BUCKETS - exact)).astype(np.int32)
    large = np.minimum(large, REL_BUCKETS - 1)
    thr = []
    for b in range(exact + 1, REL_BUCKETS):
        thr.append(int(n[np.argmax(large >= b)]))
    return tuple(thr)


T5_THRESHOLDS = _t5_thresholds()


def _cparams(*sem):
    return pltpu.CompilerParams(dimension_semantics=sem, vmem_limit_bytes=VMEM_LIMIT)


def _rmsnorm_kernel(x_ref, w_ref, o_ref):
    x = x_ref[...]
    y = x * lax.rsqrt(jnp.mean(x * x, axis=-1, keepdims=True) + NORM_EPS)
    o_ref[...] = (y * w_ref[...]).astype(o_ref.dtype)


def rmsnorm(x, w, out_dtype, tm=512):
    t, d = x.shape
    return pl.pallas_call(
        _rmsnorm_kernel,
        grid=(t // tm,),
        in_specs=[pl.BlockSpec((tm, d), lambda i: (i, 0)), pl.BlockSpec((1, d), lambda i: (0, 0))],
        out_specs=pl.BlockSpec((tm, d), lambda i: (i, 0)),
        out_shape=jax.ShapeDtypeStruct((t, d), out_dtype),
        compiler_params=_cparams("parallel"),
        name="rmsnorm",
    )(x, w.reshape(1, d))


def _mm_kernel(*refs, nk, epilogue):
    a_ref, w_ref = refs[0], refs[1]
    pos = 2
    res_ref = norm_ref = None
    if epilogue in ("residual", "residual_norm"):
        res_ref = refs[pos]
        pos += 1
    if epilogue == "residual_norm":
        norm_ref = refs[pos]
        pos += 1
    o_ref = refs[pos]
    acc_ref = refs[pos + 1] if nk > 1 else None

    def finish(acc):
        if epilogue == "relu2":
            acc = jnp.square(jnp.maximum(acc, 0.0))
        elif epilogue == "residual":
            acc = res_ref[...] + acc
        elif epilogue == "residual_norm":
            y = res_ref[...] + acc
            acc = y * lax.rsqrt(jnp.mean(y * y, axis=-1, keepdims=True) + NORM_EPS) * norm_ref[...]
        o_ref[...] = acc.astype(o_ref.dtype)

    part = jnp.dot(a_ref[...], w_ref[...], preferred_element_type=F32)
    if nk == 1:
        finish(part)
    else:
        k = pl.program_id(2)

        @pl.when(k == 0)
        def _():
            acc_ref[...] = part

        @pl.when(k > 0)
        def _():
            acc_ref[...] += part

        @pl.when(k == nk - 1)
        def _():
            finish(acc_ref[...])


def matmul(a, w, out_dtype, *, tm, tn, tk, epilogue="none", res=None, norm_w=None):
    m, kdim = a.shape
    n = w.shape[1]
    nk = kdim // tk
    in_specs = [pl.BlockSpec((tm, tk), lambda i, j, k: (i, k)), pl.BlockSpec((tk, tn), lambda i, j, k: (k, j))]
    args = [a, w]
    if epilogue in ("residual", "residual_norm"):
        in_specs.append(pl.BlockSpec((tm, tn), lambda i, j, k: (i, j)))
        args.append(res)
    if epilogue == "residual_norm":
        assert tn == n
        in_specs.append(pl.BlockSpec((1, tn), lambda i, j, k: (0, 0)))
        args.append(norm_w.reshape(1, n))
    scratch = [pltpu.VMEM((tm, tn), F32)] if nk > 1 else []
    return pl.pallas_call(
        functools.partial(_mm_kernel, nk=nk, epilogue=epilogue),
        grid=(m // tm, n // tn, nk),
        in_specs=in_specs,
        out_specs=pl.BlockSpec((tm, tn), lambda i, j, k: (i, j)),
        out_shape=jax.ShapeDtypeStruct((m, n), out_dtype),
        scratch_shapes=scratch,
        compiler_params=_cparams("parallel", "parallel", "arbitrary"),
        name="matmul_" + epilogue,
    )(*args)


def _inproj_kernel(x_ref, lnw_ref, w_ref, cs_ref, r1_ref, r2_ref, h_ref, *, n1):
    j = pl.program_id(1)

    @pl.when(j == 0)
    def _():
        x = x_ref[...]
        y = x * lax.rsqrt(jnp.mean(x * x, axis=-1, keepdims=True) + NORM_EPS)
        h_ref[...] = (y * lnw_ref[...]).astype(BF16)

    acc = jnp.dot(h_ref[...], w_ref[...], preferred_element_type=F32)

    @pl.when(j < n1)
    def _():
        r1_ref[...] = (acc * cs_ref[...]).astype(r1_ref.dtype)

    @pl.when(j >= n1)
    def _():
        r2_ref[...] = acc


def input_projection(x, ln_w, w_all, col_scale, *, tm=1024, tn=1024):
    t, d = x.shape
    n1 = R1_COLS // tn
    n2 = R2_COLS // tn
    assert w_all.shape == (d, R1_COLS + R2_COLS) and R1_COLS % tn == 0 and R2_COLS % tn == 0
    return pl.pallas_call(
        functools.partial(_inproj_kernel, n1=n1),
        grid=(t // tm, n1 + n2),
        in_specs=[pl.BlockSpec((tm, d), lambda i, j: (i, 0)),
                  pl.BlockSpec((1, d), lambda i, j: (0, 0)),
                  pl.BlockSpec((d, tn), lambda i, j: (0, j)),
                  pl.BlockSpec((1, tn), lambda i, j: (0, jnp.minimum(j, n1 - 1)))],
        out_specs=[pl.BlockSpec((tm, tn), lambda i, j: (i, jnp.minimum(j, n1 - 1))),
                   pl.BlockSpec((tm, tn), lambda i, j: (i, jnp.maximum(j - n1, 0)))],
        out_shape=[jax.ShapeDtypeStruct((t, R1_COLS), BF16), jax.ShapeDtypeStruct((t, R2_COLS), F32)],
        scratch_shapes=[pltpu.VMEM((tm, d), BF16)],
        compiler_params=_cparams("parallel", "arbitrary"),
        name="input_projection",
    )(x, ln_w.reshape(1, d), w_all, col_scale)


def _merge_kernel(oa_ref, ob_ref, wa_ref, wb_ref, ma_ref, mb_ref, o_ref):
    pa = jnp.dot(oa_ref[...], wa_ref[...], preferred_element_type=F32)
    pb = jnp.dot(ob_ref[...], wb_ref[...], preferred_element_type=F32)
    o_ref[...] = (jax.nn.sigmoid(ma_ref[...]) * pa + jax.nn.sigmoid(mb_ref[...]) * pb).astype(o_ref.dtype)


def merge_branches(o_a, o_b, w_pa, w_pb, r2, *, tm=1024, tn=512):
    t, ka = o_a.shape
    kb = o_b.shape[1]
    n = w_pa.shape[1]
    ma0 = R2_MA * LANES // tn
    mb0 = R2_MB * LANES // tn
    return pl.pallas_call(
        _merge_kernel,
        grid=(t // tm, n // tn),
        in_specs=[
            pl.BlockSpec((tm, ka), lambda i, j: (i, 0)),
            pl.BlockSpec((tm, kb), lambda i, j: (i, 0)),
            pl.BlockSpec((ka, tn), lambda i, j: (0, j)),
            pl.BlockSpec((kb, tn), lambda i, j: (0, j)),
            pl.BlockSpec((tm, tn), lambda i, j: (i, ma0 + j)),
            pl.BlockSpec((tm, tn), lambda i, j: (i, mb0 + j)),
        ],
        out_specs=pl.BlockSpec((tm, tn), lambda i, j: (i, j)),
        out_shape=jax.ShapeDtypeStruct((t, n), BF16),
        compiler_params=_cparams("parallel", "parallel"),
        name="merge_branches",
    )(o_a, o_b, w_pa, w_pb, r2, r2)


def _t5_lookup(dist, tab_ref, h):
    n = jnp.maximum(dist, 0)
    exact = REL_BUCKETS // 2
    large = jnp.full(n.shape, exact, jnp.int32)
    for thr in T5_THRESHOLDS:
        large = large + (n >= thr).astype(jnp.int32)
    bucket = jnp.where(n < exact, n, large)
    out = jnp.zeros(n.shape, F32)
    for b in range(REL_BUCKETS):
        out = jnp.where(bucket == b, tab_ref[b, h] * LOG2E, out)
    return out


def _bias_kernel(tab_ref, bc_ref, bd_ref):
    g = pl.program_id(0)
    i = pl.program_id(1)
    ncr = bc_ref.shape[0]
    blk = lax.broadcasted_iota(jnp.int32, (ncr, TQ), 0)
    qry_c = lax.broadcasted_iota(jnp.int32, (ncr, TQ), 1)
    for j in range(NSA_HPG):
        h = g * NSA_HPG + j
        bc_ref[:, j * TQ:(j + 1) * TQ] = _t5_lookup(i * TQ + qry_c - (blk * CMP_STRIDE + CMP_BLOCK - 1), tab_ref, h)

    @pl.when(i == 0)
    def _():
        key = lax.broadcasted_iota(jnp.int32, (TK, TQ), 0)
        qry = lax.broadcasted_iota(jnp.int32, (TK, TQ), 1)
        for v in range(3):
            for j in range(NSA_HPG):
                bd_ref[v, :, j * TQ:(j + 1) * TQ] = _t5_lookup(v * TK + qry - key, tab_ref, g * NSA_HPG + j)


def bias_tables(rel_table, s):
    nq = s // TQ
    cols = NSA_HPG * TQ
    return pl.pallas_call(
        _bias_kernel,
        grid=(NSA_KV_GROUPS, nq),
        in_specs=[pl.BlockSpec(memory_space=pltpu.SMEM)],
        out_specs=[pl.BlockSpec((None, None, s // CMP_STRIDE, cols), lambda g, i: (g, i, 0, 0)),
                   pl.BlockSpec((None, 3, TK, cols), lambda g, i: (g, 0, 0, 0))],
        out_shape=[jax.ShapeDtypeStruct((NSA_KV_GROUPS, nq, s // CMP_STRIDE, cols), F32),
                   jax.ShapeDtypeStruct((NSA_KV_GROUPS, 3, TK, cols), F32)],
        compiler_params=_cparams("arbitrary", "arbitrary"),
        name="t5_bias_tables",
    )(rel_table)


def _gelu_tanh(x):
    return 0.5 * x * (1.0 + jnp.tanh(math.sqrt(2.0 / math.pi) * (x + 0.044715 * (x * x * x))))


def _compress_kernel(x_ref, pe_ref, w1_ref, w2k_ref, w2vt_ref, o_ref, *, nblk):
    pe = pe_ref[...]
    acc_a = jnp.zeros((nblk, CMP_HIDDEN), F32)
    acc_b = jnp.zeros((nblk, CMP_HIDDEN), F32)
    for r in range(CMP_STRIDE):
        xr = x_ref[pl.ds(r, nblk, stride=CMP_STRIDE), :]
        xa = (xr + pe[r:r + 1, :]).astype(BF16)
        xb = (xr + pe[CMP_STRIDE + r:CMP_STRIDE + r + 1, :]).astype(BF16)
        acc_a = acc_a + jnp.dot(xa, w1_ref[r * NSA_HEAD_DIM:(r + 1) * NSA_HEAD_DIM, :], preferred_element_type=F32)
        acc_b = acc_b + jnp.dot(xb, w1_ref[(CMP_STRIDE + r) * NSA_HEAD_DIM:(CMP_STRIDE + r + 1) * NSA_HEAD_DIM, :],
                                preferred_element_type=F32)
    hidden = acc_a + jnp.concatenate([acc_b[1:], acc_b[:1]], axis=0)
    act = _gelu_tanh(hidden).astype(BF16)

    @pl.when(pl.program_id(2) == 0)
    def _():
        o_ref[...] = jnp.dot(act, w2k_ref[...], preferred_element_type=F32).astype(o_ref.dtype)

    @pl.when(pl.program_id(2) == 1)
    def _():
        o_ref[...] = lax.dot_general(w2vt_ref[...], act, (((1,), (1,)), ((), ())),
                                     preferred_element_type=F32).astype(o_ref.dtype)


def nsa_compress(r2, pe, w1, w2k, w2vt):
    b, s, _ = r2.shape
    nblk = s // CMP_STRIDE
    assert CMP_BLOCK == 2 * CMP_STRIDE and nblk == NSA_HEAD_DIM
    return pl.pallas_call(
        functools.partial(_compress_kernel, nblk=nblk),
        grid=(b, NSA_KV_GROUPS, 2),
        in_specs=[
            pl.BlockSpec((None, s, LANES), lambda bi, g, c: (bi, 0, R2_KC + 2 * c + g)),
            pl.BlockSpec((None, CMP_BLOCK, NSA_HEAD_DIM), lambda bi, g, c: (c, 0, 0)),
            pl.BlockSpec((None, CMP_BLOCK * NSA_HEAD_DIM, CMP_HIDDEN), lambda bi, g, c: (c, 0, 0)),
            pl.BlockSpec((CMP_HIDDEN, NSA_HEAD_DIM), lambda bi, g, c: (0, 0)),
            pl.BlockSpec((NSA_HEAD_DIM, CMP_HIDDEN), lambda bi, g, c: (0, 0)),
        ],
        out_specs=pl.BlockSpec((None, None, None, nblk, NSA_HEAD_DIM), lambda bi, g, c: (bi, g, c, 0, 0)),
        out_shape=jax.ShapeDtypeStruct((b, NSA_KV_GROUPS, 2, nblk, NSA_HEAD_DIM), BF16),
        compiler_params=_cparams("parallel", "parallel", "arbitrary"),
        name="nsa_compress",
    )(r2, pe, w1, w2k, w2vt)


def _nsa_attn_kernel(q_ref, ks_ref, kw_ref, vst_ref, vwt_ref, kcvc_ref, gate_ref, biasc_ref, biasd_ref,
                     covert_ref, expandt_ref, o_ref, sel_ref, m_ref, l_ref, acc_ref, res_ref, *, n_sel, top_n):
    g = pl.program_id(1)
    i = pl.program_id(2)
    cols = NSA_HPG * TQ
    nt = (((1,), (1,)), ((), ()))

    qf = q_ref[...]
    qs = jnp.concatenate([qf[:, j * NSA_HEAD_DIM:(j + 1) * NSA_HEAD_DIM] for j in range(NSA_HPG)], axis=0)
    key = lax.broadcasted_iota(jnp.int32, (TK, cols), 0)
    t_loc = lax.broadcasted_iota(jnp.int32, (TK, cols), 1) % TQ

    def gate_row(c):
        return jnp.concatenate(
            [jax.nn.sigmoid(gate_ref[pl.ds((g * NSA_HPG + j) * 3 + c, 1), :]) for j in range(NSA_HPG)], axis=1)

    def lanes4(x):
        return jnp.concatenate([x] * NSA_HPG, axis=1)

    s = lax.dot_general(kcvc_ref[0], qs, nt, preferred_element_type=F32) + biasc_ref[...]
    cmp_end = lax.broadcasted_iota(jnp.int32, s.shape, 0) * CMP_STRIDE + (CMP_BLOCK - 1)
    s = jnp.where(cmp_end <= i * TQ + lax.broadcasted_iota(jnp.int32, s.shape, 1) % TQ, s, NEG_INF)
    mx = jnp.max(s, axis=0, keepdims=True)
    mx = jnp.where(mx == NEG_INF, 0.0, mx)
    e = jnp.exp2(s - mx)
    p = e / jnp.maximum(jnp.sum(e, axis=0, keepdims=True), 1e-30)
    o_c = jnp.dot(kcvc_ref[1], p.astype(BF16), preferred_element_type=F32)
    res_ref[...] = gate_row(0) * o_c

    psum = p[:, 0:TQ]
    for j in range(1, NSA_HPG):
        psum = psum + p[:, j * TQ:(j + 1) * TQ]
    imp = jnp.dot(covert_ref[...], psum, preferred_element_type=F32, precision=HIGHEST)[:n_sel]
    blk = lax.broadcasted_iota(jnp.int32, (n_sel, TQ), 0)
    cur = (i * TQ + lax.broadcasted_iota(jnp.int32, (n_sel, TQ), 1)) // SEL_BLOCK
    causal_blk = blk <= cur
    forced = (blk == 0) | (blk == cur) | (blk == cur - 1)
    score = jnp.where(causal_blk, imp + jnp.where(forced, SEL_FORCE, 0.0), NEG_INF)
    rank = jnp.zeros((n_sel, TQ), jnp.int32)
    for mp in range(n_sel):
        other = score[mp:mp + 1, :]
        beats = (other > score) | ((other == score) & (blk > mp))
        rank = rank + beats.astype(jnp.int32)
    sel = jnp.where((rank < top_n) & causal_blk, 1.0, 0.0)
    sel = jnp.concatenate([sel, jnp.zeros((LANES - n_sel, TQ), F32)], axis=0).astype(BF16)
    sel_ref[...] = jnp.where(jnp.dot(expandt_ref[...], sel, preferred_element_type=F32) > 0.5, 0.0, NEG_INF)

    def reset():
        m_ref[...] = jnp.full(m_ref.shape, NEG_INF, F32)
        l_ref[...] = jnp.zeros(l_ref.shape, F32)
        acc_ref[...] = jnp.zeros(acc_ref.shape, F32)

    far_bias = biasd_ref[2, 0:1, :]

    def attend(items):
        def scores(item):
            return lax.dot_general(item[1], qs, nt, preferred_element_type=F32)

        def softmax_step(item, s):
            st, _, _, variant, mask_add = item
            if variant < 2:
                s = s + biasd_ref[variant]
            if mask_add is not None:
                s = s + mask_add
            smax = jnp.max(s, axis=0, keepdims=True)
            if variant == 2:
                smax = smax + far_bias
            m_prev = m_ref[st]
            m_new = jnp.maximum(m_prev, smax)
            m_safe = jnp.where(m_new == NEG_INF, 0.0, m_new)
            alpha = jnp.exp2(m_prev - m_safe)
            p = jnp.exp2(s + ((far_bias - m_safe) if variant == 2 else -m_safe))
            l_ref[st] = alpha * l_ref[st] + jnp.sum(p, axis=0, keepdims=True)
            m_ref[st] = m_new
            return alpha, p.astype(BF16)

        def value_step(item, alpha, pb):
            st, vt = item[0], item[2]
            acc_ref[st] = alpha * acc_ref[st] + jnp.dot(vt, pb, preferred_element_type=F32)

        s_next = scores(items[0])
        pending = None
        for n, item in enumerate(items):
            s_cur = s_next
            if n + 1 < len(items):
                s_next = scores(items[n + 1])
            alpha, pb = softmax_step(item, s_cur)
            if pending is not None:
                value_step(*pending)
            pending = (item, alpha, pb)
        value_step(*pending)

    def tile(ref, k):
        return ref[pl.ds(pl.multiple_of(k * TK, TK), TK), :]

    causal_add = jnp.where(key <= t_loc, 0.0, NEG_INF)
    edge_add = jnp.where(key > t_loc, 0.0, NEG_INF)
    n_win = WINDOW // TK
    SEL, WIN, SEL2 = 0, 1, 2
    reset()

    def near_items(depth):
        items = []
        for d in range(depth + 1):
            k = i - d
            sel_mask = lanes4(tile(sel_ref, k))
            items.append((SEL, tile(ks_ref, k), vst_ref[k], min(d, 2), sel_mask + causal_add if d == 0 else sel_mask))
            win_mask = causal_add if d == 0 else (edge_add if d == n_win else None)
            items.append((WIN, tile(kw_ref, k), vwt_ref[k], min(d, 2), win_mask))
        return items

    for depth in range(n_win + 1):
        @pl.when((i == depth) if depth < n_win else (i >= depth))
        def _(depth=depth):
            attend(near_items(depth))

    n_far = jnp.maximum(i - n_win, 0)

    def far_item(stream, k):
        return (stream, tile(ks_ref, k), vst_ref[k], 2, lanes4(tile(sel_ref, k)))

    def far_pair(kk, carry):
        attend([far_item(SEL, 2 * kk), far_item(SEL2, 2 * kk + 1)])
        return carry

    lax.fori_loop(0, n_far // 2, far_pair, 0)

    @pl.when(n_far % 2 == 1)
    def _():
        attend([far_item(SEL, n_far - 1)])

    m_a, m_b = m_ref[SEL], m_ref[SEL2]
    m_ab = jnp.maximum(m_a, m_b)
    m_ab = jnp.where(m_ab == NEG_INF, 0.0, m_ab)
    w_a = jnp.exp2(m_a - m_ab)
    w_b = jnp.exp2(m_b - m_ab)
    l_sel = w_a * l_ref[SEL] + w_b * l_ref[SEL2]
    o_sel = (w_a * acc_ref[SEL] + w_b * acc_ref[SEL2]) * (1.0 / jnp.maximum(l_sel, 1e-30))
    o_win = acc_ref[WIN] * (1.0 / jnp.maximum(l_ref[WIN], 1e-30))
    res = res_ref[...] + gate_row(1) * o_sel + gate_row(2) * o_win
    for j in range(NSA_HPG):
        o_ref[:, j * NSA_HEAD_DIM:(j + 1) * NSA_HEAD_DIM] = res[:, j * TQ:(j + 1) * TQ].T.astype(o_ref.dtype)


def nsa_attention(r1, r2, kcvc, bias_c, bias_d):
    b, s, _ = r1.shape
    n_cmp_rows = s // CMP_STRIDE
    n_sel = s // SEL_BLOCK
    assert n_cmp_rows % LANES == 0 and n_sel <= LANES and n_sel % 8 == 0 and WINDOW % TK == 0 and TQ == TK
    assert 2 * TK - (TQ - 1) >= REL_MAX_DIST
    top_n = min(SEL_TOP_N, n_sel)
    nq = s // TQ
    nk = s // TK
    gw = NSA_HPG * NSA_HEAD_DIM
    cols = NSA_HPG * TQ
    dh = NSA_HEAD_DIM

    n_cmp = n_cmp_rows - CMP_BLOCK // CMP_STRIDE + 1
    c_start = np.arange(n_cmp_rows) * CMP_STRIDE
    s_start = np.arange(LANES) * SEL_BLOCK
    cover_t = ((c_start[None, :] <= s_start[:, None] + SEL_BLOCK - 1) & (c_start[None, :] + CMP_BLOCK - 1 >= s_start[:, None])
               & (np.arange(n_cmp_rows)[None, :] < n_cmp) & (np.arange(LANES)[:, None] < n_sel))
    cover_t = jnp.asarray(cover_t.astype(np.float32))
    expand_t = jnp.asarray(((np.arange(s)[:, None] // SEL_BLOCK) == np.arange(LANES)[None, :]).astype(np.float32), BF16)

    v0 = R1_VS * LANES
    v1 = R1_VW * LANES
    vt = jnp.stack([r1[:, :, v0:v0 + NSA_KV_WIDTH], r1[:, :, v1:v1 + NSA_KV_WIDTH]], axis=1)
    vt = vt.reshape(b, 2, nk, TK, NSA_KV_GROUPS, dh).transpose(0, 1, 4, 2, 5, 3)
    g0 = R2_SMALL * LANES + SMALL_GATE
    gate_t = jnp.swapaxes(r2[:, :, g0:g0 + 3 * NSA_HEADS], 1, 2)

    k_spec = lambda blk: pl.BlockSpec((None, s, LANES), lambda bi, g, i, blk=blk: (bi, 0, blk + g))
    vt_spec = lambda w: pl.BlockSpec((None, None, None, nk, dh, TK), lambda bi, g, i, w=w: (bi, w, g, 0, 0, 0))
    return pl.pallas_call(
        functools.partial(_nsa_attn_kernel, n_sel=n_sel, top_n=top_n),
        grid=(b, NSA_KV_GROUPS, nq),
        in_specs=[
            pl.BlockSpec((None, TQ, gw), lambda bi, g, i: (bi, i, g)),
            k_spec(R1_KS), k_spec(R1_KW), vt_spec(0), vt_spec(1),
            pl.BlockSpec((None, None, 2, n_cmp_rows, dh), lambda bi, g, i: (bi, g, 0, 0, 0)),
            pl.BlockSpec((None, 3 * NSA_HEADS, TQ), lambda bi, g, i: (bi, 0, i)),
            pl.BlockSpec((None, None, n_cmp_rows, cols), lambda bi, g, i: (g, i, 0, 0)),
            pl.BlockSpec((None, 3, TK, cols), lambda bi, g, i: (g, 0, 0, 0)),
            pl.BlockSpec((LANES, n_cmp_rows), lambda bi, g, i: (0, 0)),
            pl.BlockSpec((s, LANES), lambda bi, g, i: (0, 0)),
        ],
        out_specs=pl.BlockSpec((None, TQ, gw), lambda bi, g, i: (bi, i, g)),
        out_shape=jax.ShapeDtypeStruct((b, s, NSA_WIDTH), BF16),
        scratch_shapes=[
            pltpu.VMEM((s, TQ), F32),
            pltpu.VMEM((3, 1, cols), F32),
            pltpu.VMEM((3, 1, cols), F32),
            pltpu.VMEM((3, dh, cols), F32),
            pltpu.VMEM((dh, cols), F32),
        ],
        compiler_params=_cparams("parallel", "parallel", "arbitrary"),
        name="nsa_attention",
    )(r1, r1, r1, vt, vt, kcvc, gate_t, bias_c, bias_d, cover_t, expand_t)


GDN_PAIR = 2 * GDN_CHUNK
GDN_GROUP = 4
GDN_ROWS = 512


def _gdn_prep_kernel(q_ref, k_ref, v_ref, smt_ref, alog_ref, dtb_ref, cwq_ref, cwk_ref, cwv_ref,
                     w_out, qg_out, u_out, qk_out, kdt_out, egl_out,
                     kb_s, kk_s, qq_s, vk_s, kd_s, gcol_s, grow_s, xp_s):
    h = pl.program_id(1)
    s = q_ref.shape[0]
    c = GDN_CHUNK
    pr = GDN_PAIR
    dk = GDN_HEAD_DIM
    nt = (((1,), (1,)), ((), ()))
    tn = (((0,), (0,)), ((), ()))

    xg = smt_ref[0:GDN_HEADS, :] + dtb_ref[...]
    g8 = -jnp.exp(alog_ref[...]) * (jnp.maximum(xg, 0.0) + jnp.log1p(jnp.exp(-jnp.abs(xg))))
    beta8 = jax.nn.sigmoid(smt_ref[GDN_HEADS:2 * GDN_HEADS, :])
    pos = lax.broadcasted_iota(jnp.int32, g8.shape, 1) % c
    gcum8 = g8
    sh = 1
    while sh < c:
        gcum8 = gcum8 + jnp.where(pos >= sh, pltpu.roll(gcum8, sh, axis=1), 0.0)
        sh *= 2
    for p in range(s // pr):
        grow_s[p] = gcum8[:, p * pr:(p + 1) * pr]
    sel_k = lax.broadcasted_iota(jnp.int32, (2 * GDN_HEADS, 2 * dk), 0)
    sel_l = lax.broadcasted_iota(jnp.int32, (2 * GDN_HEADS, 2 * dk), 1)
    pick = jnp.where(((sel_k == h) & (sel_l < dk)) | ((sel_k == GDN_HEADS + h) & (sel_l >= dk)), 1.0, 0.0)
    both = jnp.concatenate([gcum8, beta8], axis=0)
    hi = both.astype(BF16)
    rest = both - hi.astype(F32)
    mid = rest.astype(BF16)
    lo = (rest - mid.astype(F32)).astype(BF16)
    col = lax.dot_general(jnp.concatenate([hi, mid, lo], axis=0), jnp.concatenate([pick] * 3, axis=0).astype(BF16),
                          tn, preferred_element_type=F32)
    gcol = col[:, :dk]
    bcol = col[:, dk:]
    gcol_s[...] = gcol

    hist = 8
    xp_s[0:hist, :] = jnp.zeros((hist, dk), F32)

    def conv_silu(x_ref, w_ref):
        x = x_ref[...]
        w = w_ref[...]
        xp_s[hist:hist + s, :] = x
        y = x * w[CONV_WIDTH - 1:CONV_WIDTH, :]
        for sft in range(1, CONV_WIDTH):
            y = y + xp_s[hist - sft:hist - sft + s, :] * w[CONV_WIDTH - 1 - sft:CONV_WIDTH - sft, :]
        return y * jax.nn.sigmoid(y)

    def l2n(x):
        return x * lax.rsqrt(jnp.sum(x * x, axis=-1, keepdims=True) + NORM_EPS)

    q = l2n(conv_silu(q_ref, cwq_ref)) * (dk ** -0.5)
    k = l2n(conv_silu(k_ref, cwk_ref))
    v = conv_silu(v_ref, cwv_ref)
    eg = jnp.exp(gcol)
    kb = k * bcol
    qg_out[...] = (q * eg).astype(BF16)
    kk_s[...] = k.astype(BF16)
    kb_s[...] = kb.astype(BF16)
    qq_s[...] = q.astype(BF16)
    vk_s[:, :dk] = (v * bcol).astype(BF16)
    vk_s[:, dk:] = (kb * eg).astype(BF16)
    g3 = gcol.reshape(s // c, c, dk)
    g_last = g3[:, c - 1:c, :]
    kd_s[...] = (k * jnp.exp(jnp.broadcast_to(g_last, g3.shape).reshape(s, dk) - gcol)).astype(BF16)
    egl_out[...] = jnp.broadcast_to(jnp.exp(g_last), egl_out.shape)

    r_i = lax.broadcasted_iota(jnp.int32, (pr, pr), 0)
    l_i = lax.broadcasted_iota(jnp.int32, (pr, pr), 1)
    tri = ((r_i // c) == (l_i // c)) & (r_i >= l_i)
    eye = r_i == l_i
    eye_f = jnp.where(eye, 1.0, 0.0)
    eye_b = eye_f.astype(BF16)

    def pair_group(ps):
        rows = [slice(p * pr, (p + 1) * pr) for p in ps]
        n = len(ps)
        x = [lax.dot_general(jnp.concatenate([kb_s[r, :], qq_s[r, :]], axis=0), kk_s[r, :], nt,
                             preferred_element_type=F32) for r in rows]
        decay = []
        for p, r in zip(ps, rows):
            diff = gcol_s[r, :] - grow_s[p, pl.ds(h, 1), :]
            decay.append(jnp.where(tri, jnp.exp(jnp.where(tri, diff, 0.0)), 0.0))
        for a in range(n):
            qk_out[rows[a], :] = (x[a][pr:] * decay[a]).astype(BF16)
            kdt_out[ps[a]] = lax.dot_general(eye_b, kd_s[rows[a], :], nt, preferred_element_type=F32).astype(BF16)
        qm = [jnp.where(eye, 0.0, -(x[a][:pr] * decay[a])) for a in range(n)]
        tinv = [eye_f + qm[a] for a in range(n)]
        qb = [qm[a].astype(BF16) for a in range(n)]
        qm = [jnp.dot(qb[a], qb[a], preferred_element_type=F32) for a in range(n)]
        steps = int(math.log2(c)) - 1
        for it in range(steps):
            qb = [qm[a].astype(BF16) for a in range(n)]
            if it < steps - 1:
                r = [jnp.dot(jnp.concatenate([qb[a], tinv[a].astype(BF16)], axis=0), qb[a],
                             preferred_element_type=F32) for a in range(n)]
                qm = [r[a][:pr] for a in range(n)]
                tinv = [tinv[a] + r[a][pr:] for a in range(n)]
            else:
                tinv = [tinv[a] + jnp.dot(tinv[a].astype(BF16), qb[a], preferred_element_type=F32) for a in range(n)]
        for a in range(n):
            uw = jnp.dot(tinv[a].astype(BF16), vk_s[rows[a], :], preferred_element_type=F32)
            u_out[rows[a], :] = uw[:, :dk]
            w_out[rows[a], :] = uw[:, dk:].astype(BF16)

    npair = s // pr
    for p0 in range(0, npair, GDN_GROUP):
        pair_group(list(range(p0, min(p0 + GDN_GROUP, npair))))


def _gdn_scan_kernel(w_ref, qg_ref, u_ref, qk_ref, kdt_ref, egl_ref, z_ref, nw_ref, o_ref, state_ref, os_ref):
    c = GDN_CHUNK
    pr = GDN_PAIR
    dk = GDN_HEAD_DIM
    nh = w_ref.shape[0]
    rb = w_ref.shape[1]

    @pl.when(pl.program_id(1) == 0)
    def _():
        state_ref[...] = jnp.zeros(state_ref.shape, F32)

    zeros = jnp.zeros((c, dk), BF16)

    def pair(p, carry):
        r0 = pl.multiple_of(p * pr, pr)
        for half in range(2):
            rows = pl.ds(r0 + half * c, c)
            state = [state_ref[h] for h in range(nh)]
            r1 = [jnp.dot(jnp.concatenate([w_ref[h, rows, :], qg_ref[h, rows, :]], axis=0), state[h].astype(BF16),
                          preferred_element_type=F32) for h in range(nh)]
            vnb = [(u_ref[h, rows, :] - r1[h][:c]).astype(BF16) for h in range(nh)]
            rhs = [jnp.concatenate([vnb[h], zeros] if half == 0 else [zeros, vnb[h]], axis=0) for h in range(nh)]
            r2 = [jnp.dot(jnp.concatenate([qk_ref[h, rows, :], kdt_ref[h, p]], axis=0), rhs[h],
                          preferred_element_type=F32) for h in range(nh)]
            for h in range(nh):
                os_ref[rows, h * dk:(h + 1) * dk] = r1[h][c:] + r2[h][:c]
                state_ref[h] = state[h] * egl_ref[h, 2 * p + half, 0:1, :] + r2[h][c:]
        return carry

    lax.fori_loop(0, rb // pr, pair, 0)

    for h in range(nh):
        o = os_ref[:, h * dk:(h + 1) * dk]
        o = o * lax.rsqrt(jnp.mean(o * o, axis=-1, keepdims=True) + NORM_EPS) * nw_ref[...]
        z = z_ref[:, h * dk:(h + 1) * dk]
        o_ref[:, h * dk:(h + 1) * dk] = (o * (z * jax.nn.sigmoid(z))).astype(o_ref.dtype)


def gated_deltanet(r2, conv_w, a_log, dt_bias, norm_w):
    b, s, _ = r2.shape
    dk = GDN_HEAD_DIM
    nh = GDN_HEADS
    npair = s // GDN_PAIR
    nchunk = s // GDN_CHUNK
    rb = GDN_ROWS
    assert s % GDN_PAIR == 0 and s % rb == 0 and (R2_Z * LANES) % GDN_WIDTH == 0
    a0 = R2_SMALL * LANES + SMALL_A
    smt = jnp.swapaxes(r2[:, :, a0:a0 + 2 * nh], 1, 2)
    col = lambda blk: pl.BlockSpec((None, s, dk), lambda bi, h, blk=blk: (bi, 0, blk + h))
    cw = lambda blk: pl.BlockSpec((CONV_WIDTH, dk), lambda bi, h, blk=blk: (0, blk + h))
    vec = pl.BlockSpec((nh, 1), lambda bi, h: (0, 0))
    per_head = lambda *shape: pl.BlockSpec((None, None) + shape, lambda bi, h: (bi, h) + (0,) * len(shape))
    w, qg, u, qk, kdt, egl = pl.pallas_call(
        _gdn_prep_kernel,
        grid=(b, nh),
        in_specs=[col(R2_GQ), col(R2_GK), col(R2_GV),
                  pl.BlockSpec((None, 2 * nh, s), lambda bi, h: (bi, 0, 0)), vec, vec,
                  cw(0), cw(nh), cw(2 * nh)],
        out_specs=[per_head(s, dk), per_head(s, dk), per_head(s, dk), per_head(s, dk),
                   per_head(npair, GDN_PAIR, GDN_PAIR), per_head(nchunk, 8, dk)],
        out_shape=[jax.ShapeDtypeStruct((b, nh, s, dk), BF16), jax.ShapeDtypeStruct((b, nh, s, dk), BF16),
                   jax.ShapeDtypeStruct((b, nh, s, dk), F32), jax.ShapeDtypeStruct((b, nh, s, dk), BF16),
                   jax.ShapeDtypeStruct((b, nh, npair, GDN_PAIR, GDN_PAIR), BF16),
                   jax.ShapeDtypeStruct((b, nh, nchunk, 8, dk), F32)],
        scratch_shapes=[pltpu.VMEM((s, dk), BF16), pltpu.VMEM((s, dk), BF16), pltpu.VMEM((s, dk), BF16),
                        pltpu.VMEM((s, 2 * dk), BF16), pltpu.VMEM((s, dk), BF16), pltpu.VMEM((s, dk), F32),
                        pltpu.VMEM((npair, nh, GDN_PAIR), F32), pltpu.VMEM((s + 8, dk), F32)],
        compiler_params=_cparams("parallel", "parallel"),
        name="gdn_prep",
    )(r2, r2, r2, smt, a_log.reshape(nh, 1), dt_bias.reshape(nh, 1), conv_w, conv_w, conv_w)

    seq = lambda *shape: pl.BlockSpec((None, nh) + shape, lambda bi, j: (bi, 0, j) + (0,) * (len(shape) - 1))
    return pl.pallas_call(
        _gdn_scan_kernel,
        grid=(b, s // rb),
        in_specs=[seq(rb, dk), seq(rb, dk), seq(rb, dk), seq(rb, dk),
                  seq(rb // GDN_PAIR, GDN_PAIR, GDN_PAIR), seq(rb // GDN_CHUNK, 8, dk),
                  pl.BlockSpec((None, rb, GDN_WIDTH), lambda bi, j: (bi, j, R2_Z * LANES // GDN_WIDTH)),
                  pl.BlockSpec((1, dk), lambda bi, j: (0, 0))],
        out_specs=pl.BlockSpec((None, rb, GDN_WIDTH), lambda bi, j: (bi, j, 0)),
        out_shape=jax.ShapeDtypeStruct((b, s, GDN_WIDTH), BF16),
        scratch_shapes=[pltpu.VMEM((nh, dk, dk), F32), pltpu.VMEM((rb, GDN_WIDTH), F32)],
        compiler_params=_cparams("parallel", "arbitrary"),
        name="gdn_scan",
    )(w, qg, u, qk, kdt, egl, r2, norm_w.reshape(1, dk))


def _split_w_in(w_in):
    o_q = 0
    o_kv = NSA_WIDTH
    o_gate = o_kv + 6 * NSA_KV_WIDTH
    o_qkv = o_gate + 3 * NSA_HEADS
    o_z = o_qkv + 3 * GDN_WIDTH
    o_a = o_z + GDN_WIDTH
    o_b = o_a + GDN_HEADS
    o_m = o_b + GDN_HEADS
    kvw = NSA_KV_WIDTH
    w1 = jnp.concatenate([w_in[:, o_q:o_kv], w_in[:, o_kv + 2 * kvw:o_kv + 6 * kvw]], axis=1)
    small = jnp.concatenate([w_in[:, o_gate:o_qkv], w_in[:, o_a:o_m]], axis=1)
    used = (R2_SMALL * LANES) + small.shape[1]
    w2 = jnp.concatenate([w_in[:, o_z:o_a], w_in[:, o_kv:o_kv + 2 * kvw], w_in[:, o_qkv:o_z], w_in[:, o_m:], small,
                          jnp.zeros((w_in.shape[0], R2_COLS - used), w_in.dtype)], axis=1)
    return jnp.concatenate([w1, w2], axis=1).astype(BF16)


def _r1_col_scale():
    c = np.ones((1, R1_COLS), np.float32)
    c[:, :NSA_WIDTH] = NSA_HEAD_DIM ** -0.5 * LOG2E
    return jnp.asarray(c)


def _layer(x, bias_c, bias_d, ln1_w, w_in, pe_k, pe_v, w1_k, w2_k, w1_v, w2_v, conv_w, a_log, dt_bias,
           gdn_norm_w, w_pa, w_pb, w_o, ln2_w, w_up, w_down, final_norm_w, b, s):
    t = b * s
    r1, r2 = input_projection(x, ln1_w, _split_w_in(w_in), _r1_col_scale())
    r1 = r1.reshape(b, s, R1_COLS)
    r2 = r2.reshape(b, s, R2_COLS)
    kcvc = nsa_compress(r2, jnp.stack([pe_k, pe_v]), jnp.stack([w1_k, w1_v]).astype(BF16),
                        w2_k.astype(BF16), w2_v.T.astype(BF16))
    o_a = nsa_attention(r1, r2, kcvc, bias_c, bias_d).reshape(t, NSA_WIDTH)
    o_b = gated_deltanet(r2, conv_w, a_log, dt_bias, gdn_norm_w).reshape(t, GDN_WIDTH)
    m = merge_branches(o_a, o_b, w_pa.astype(BF16), w_pb.astype(BF16), r2.reshape(t, R2_COLS))
    x = matmul(m, w_o.astype(BF16), F32, tm=1024, tn=1024, tk=D_MODEL, epilogue="residual", res=x)
    h2 = rmsnorm(x, ln2_w, BF16)
    u = matmul(h2, w_up.astype(BF16), BF16, tm=1024, tn=1024, tk=D_MODEL, epilogue="relu2")
    if final_norm_w is None:
        return matmul(u, w_down.astype(BF16), F32, tm=512, tn=D_MODEL, tk=2048, epilogue="residual", res=x)
    return matmul(u, w_down.astype(BF16), F32, tm=512, tn=D_MODEL, tk=2048, epilogue="residual_norm", res=x,
                  norm_w=final_norm_w)


def kernel(x, rel_table, ln1_w, w_in, cmp_pe_k, cmp_pe_v, cmp_w1_k, cmp_w2_k, cmp_w1_v, cmp_w2_v, conv_w, a_log,
           dt_bias, gdn_norm_w, w_pa, w_pb, w_o, ln2_w, w_up, w_down, ln_f_w):
    b, s, d = x.shape
    depth = w_in.shape[0]
    bias_c, bias_d = bias_tables(rel_table, s)
    xt = x.reshape(b * s, d)
    for l in range(depth):
        xt = _layer(xt, bias_c, bias_d, ln1_w[l], w_in[l], cmp_pe_k[l], cmp_pe_v[l], cmp_w1_k[l], cmp_w2_k[l],
                    cmp_w1_v[l], cmp_w2_v[l], conv_w[l], a_log[l], dt_bias[l], gdn_norm_w[l], w_pa[l], w_pb[l],
                    w_o[l], ln2_w[l], w_up[l], w_down[l], ln_f_w if l == depth - 1 else None, b, s)
    return xt.reshape(b, s, d)
```

```python
import functools
import math

import numpy as np
import jax
import jax.numpy as jnp
from jax import lax
from jax.experimental import pallas as pl
from jax.experimental.pallas import tpu as pltpu

F32 = jnp.float32
BF16 = jnp.bfloat16
HIGHEST = lax.Precision.HIGHEST

D_MODEL = 2048
DEPTH = 2
NSA_HEADS = 8
NSA_KV_GROUPS = 2
NSA_HPG = NSA_HEADS // NSA_KV_GROUPS
NSA_HEAD_DIM = 128
NSA_WIDTH = NSA_HEADS * NSA_HEAD_DIM
NSA_KV_WIDTH = NSA_KV_GROUPS * NSA_HEAD_DIM
CMP_BLOCK = 32
CMP_STRIDE = 16
CMP_HIDDEN = 256
SEL_BLOCK = 64
SEL_TOP_N = 16
SEL_FORCE = 1000.0
WINDOW = 512
GDN_HEADS = 8
GDN_HEAD_DIM = 128
GDN_WIDTH = GDN_HEADS * GDN_HEAD_DIM
CONV_WIDTH = 4
GDN_CHUNK = 64
REL_BUCKETS = 32
REL_MAX_DIST = 128
D_FF = 4 * D_MODEL
NORM_EPS = 1e-6

LANES = 128
VMEM_LIMIT = 56 * 1024 * 1024
TQ = 256
TK = 256
NEG_INF = float("-inf")
LOG2E = math.log2(math.e)

R1_COLS = 2048
R1_KS, R1_VS, R1_KW, R1_VW = 8, 10, 12, 14
R2_Z, R2_KC, R2_VC, R2_GQ, R2_GK, R2_GV, R2_MA, R2_MB, R2_SMALL = 0, 8, 10, 12, 20, 28, 36, 52, 68
R2_COLS = 72 * LANES
SMALL_GATE, SMALL_A, SMALL_B = 0, 24, 32


def _t5_thresholds():
    exact = REL_BUCKETS // 2
    n = np.arange(exact, REL_MAX_DIST * 2, dtype=np.int64)
    nf = n.astype(np.float32)
    large = exact + (np.log(nf / np.float32(exact)) / np.float32(math.log(REL_MAX_DIST / exact))
                     * np.float32(REL_BUCKETS - exact)).astype(np.int32)
    large = np.minimum(large, REL_BUCKETS - 1)
    thr = []
    for b in range(exact + 1, REL_BUCKETS):
        thr.append(int(n[np.argmax(large >= b)]))
    return tuple(thr)


T5_THRESHOLDS = _t5_thresholds()


def _cparams(*sem):
    return pltpu.CompilerParams(dimension_semantics=sem, vmem_limit_bytes=VMEM_LIMIT)


def _rmsnorm_kernel(x_ref, w_ref, o_ref):
    x = x_ref[...]
    y = x * lax.rsqrt(jnp.mean(x * x, axis=-1, keepdims=True) + NORM_EPS)
    o_ref[...] = (y * w_ref[...]).astype(o_ref.dtype)


def rmsnorm(x, w, out_dtype, tm=512):
    t, d = x.shape
    return pl.pallas_call(
        _rmsnorm_kernel,
        grid=(t // tm,),
        in_specs=[pl.BlockSpec((tm, d), lambda i: (i, 0)), pl.BlockSpec((1, d), lambda i: (0, 0))],
        out_specs=pl.BlockSpec((tm, d), lambda i: (i, 0)),
        out_shape=jax.ShapeDtypeStruct((t, d), out_dtype),
        compiler_params=_cparams("parallel"),
        name="rmsnorm",
    )(x, w.reshape(1, d))


def _mm_kernel(*refs, nk, epilogue):
    a_ref, w_ref = refs[0], refs[1]
    pos = 2
    res_ref = norm_ref = None
    if epilogue in ("residual", "residual_norm"):
        res_ref = refs[pos]
        pos += 1
    if epilogue == "residual_norm":
        norm_ref = refs[pos]
        pos += 1
    o_ref = refs[pos]
    pos += 1
    n_ref = None
    if epilogue == "residual_norm":
        n_ref = refs[pos]
        pos += 1
    acc_ref = refs[pos] if nk > 1 else None

    def finish(acc):
        if epilogue == "relu2":
            acc = jnp.square(jnp.maximum(acc, 0.0))
        elif epilogue == "residual":
            acc = res_ref[...] + acc
        elif epilogue == "residual_norm":
            acc = res_ref[...] + acc
            normed = acc * lax.rsqrt(jnp.mean(acc * acc, axis=-1, keepdims=True) + NORM_EPS) * norm_ref[...]
            n_ref[...] = normed.astype(n_ref.dtype)
        o_ref[...] = acc.astype(o_ref.dtype)

    part = jnp.dot(a_ref[...], w_ref[...], preferred_element_type=F32)
    if nk == 1:
        finish(part)
    else:
        k = pl.program_id(2)

        @pl.when(k == 0)
        def _():
            acc_ref[...] = part

        @pl.when(k > 0)
        def _():
            acc_ref[...] += part

        @pl.when(k == nk - 1)
        def _():
            finish(acc_ref[...])


def _column_tiles(w, tn):
    k, n = w.shape
    return w.reshape(k, n // tn, tn).transpose(1, 0, 2)


def matmul(a, w, out_dtype, *, tm, tn, tk, epilogue="none", res=None, norm_w=None, norm_dtype=None):
    m, kdim = a.shape
    n = w.shape[1]
    nk = kdim // tk
    if nk == 1 and tn < n:
        w = _column_tiles(w, tn)
        w_spec = pl.BlockSpec((None, tk, tn), lambda i, j, k: (j, 0, 0))
    else:
        w_spec = pl.BlockSpec((tk, tn), lambda i, j, k: (k, j))
    in_specs = [pl.BlockSpec((tm, tk), lambda i, j, k: (i, k)), w_spec]
    args = [a, w]
    if epilogue in ("residual", "residual_norm"):
        in_specs.append(pl.BlockSpec((tm, tn), lambda i, j, k: (i, j)))
        args.append(res)
    if epilogue == "residual_norm":
        assert tn == n
        in_specs.append(pl.BlockSpec((1, tn), lambda i, j, k: (0, 0)))
        args.append(norm_w.reshape(1, n))
    scratch = [pltpu.VMEM((tm, tn), F32)] if nk > 1 else []
    out_spec = pl.BlockSpec((tm, tn), lambda i, j, k: (i, j))
    out_specs, out_shape = out_spec, jax.ShapeDtypeStruct((m, n), out_dtype)
    if epilogue == "residual_norm":
        out_specs = [out_spec, out_spec]
        out_shape = [out_shape, jax.ShapeDtypeStruct((m, n), norm_dtype)]
    return pl.pallas_call(
        functools.partial(_mm_kernel, nk=nk, epilogue=epilogue),
        grid=(m // tm, n // tn, nk),
        in_specs=in_specs,
        out_specs=out_specs,
        out_shape=out_shape,
        scratch_shapes=scratch,
        compiler_params=_cparams("parallel", "parallel", "arbitrary"),
        name="matmul_" + epilogue,
    )(*args)


def _inproj_kernel(x_ref, lnw_ref, w_ref, cs_ref, r1_ref, r2_ref, h_ref, *, n1):
    j = pl.program_id(1)

    @pl.when(j == 0)
    def _():
        x = x_ref[...]
        y = x * lax.rsqrt(jnp.mean(x * x, axis=-1, keepdims=True) + NORM_EPS)
        h_ref[...] = (y * lnw_ref[...]).astype(BF16)

    acc = jnp.dot(h_ref[...], w_ref[...], preferred_element_type=F32)

    @pl.when(j < n1)
    def _():
        r1_ref[...] = (acc * cs_ref[...]).astype(r1_ref.dtype)

    @pl.when(j >= n1)
    def _():
        r2_ref[...] = acc


def input_projection(x, ln_w, w_all, col_scale, *, tm=1024, tn=1024):
    t, d = x.shape
    n1 = R1_COLS // tn
    n2 = R2_COLS // tn
    assert w_all.shape == (d, R1_COLS + R2_COLS) and R1_COLS % tn == 0 and R2_COLS % tn == 0
    return pl.pallas_call(
        functools.partial(_inproj_kernel, n1=n1),
        grid=(t // tm, n1 + n2),
        in_specs=[pl.BlockSpec((tm, d), lambda i, j: (i, 0)),
                  pl.BlockSpec((1, d), lambda i, j: (0, 0)),
                  pl.BlockSpec((None, d, tn), lambda i, j: (j, 0, 0)),
                  pl.BlockSpec((1, tn), lambda i, j: (0, jnp.minimum(j, n1 - 1)))],
        out_specs=[pl.BlockSpec((tm, tn), lambda i, j: (i, jnp.minimum(j, n1 - 1))),
                   pl.BlockSpec((tm, tn), lambda i, j: (i, jnp.maximum(j - n1, 0)))],
        out_shape=[jax.ShapeDtypeStruct((t, R1_COLS), BF16), jax.ShapeDtypeStruct((t, R2_COLS), F32)],
        scratch_shapes=[pltpu.VMEM((tm, d), BF16)],
        compiler_params=_cparams("parallel", "arbitrary"),
        name="input_projection",
    )(x, ln_w.reshape(1, d), _column_tiles(w_all, tn), col_scale)


def _merge_kernel(oa_ref, ob_ref, wa_ref, wb_ref, ma_ref, mb_ref, o_ref):
    pa = jnp.dot(oa_ref[...], wa_ref[...], preferred_element_type=F32)
    pb = jnp.dot(ob_ref[...], wb_ref[...], preferred_element_type=F32)
    o_ref[...] = (jax.nn.sigmoid(ma_ref[...]) * pa + jax.nn.sigmoid(mb_ref[...]) * pb).astype(o_ref.dtype)


def merge_branches(o_a, o_b, w_pa, w_pb, r2, *, tm=1024, tn=512):
    t, ka = o_a.shape
    kb = o_b.shape[1]
    n = w_pa.shape[1]
    ma0 = R2_MA * LANES // tn
    mb0 = R2_MB * LANES // tn
    return pl.pallas_call(
        _merge_kernel,
        grid=(t // tm, n // tn),
        in_specs=[
            pl.BlockSpec((tm, ka), lambda i, j: (i, 0)),
            pl.BlockSpec((tm, kb), lambda i, j: (i, 0)),
            pl.BlockSpec((ka, tn), lambda i, j: (0, j)),
            pl.BlockSpec((kb, tn), lambda i, j: (0, j)),
            pl.BlockSpec((tm, tn), lambda i, j: (i, ma0 + j)),
            pl.BlockSpec((tm, tn), lambda i, j: (i, mb0 + j)),
        ],
        out_specs=pl.BlockSpec((tm, tn), lambda i, j: (i, j)),
        out_shape=jax.ShapeDtypeStruct((t, n), BF16),
        compiler_params=_cparams("parallel", "parallel"),
        name="merge_branches",
    )(o_a, o_b, w_pa, w_pb, r2, r2)


def _t5_lookup(dist, tab_ref, h):
    n = jnp.maximum(dist, 0)
    exact = REL_BUCKETS // 2
    large = jnp.full(n.shape, exact, jnp.int32)
    for thr in T5_THRESHOLDS:
        large = large + (n >= thr).astype(jnp.int32)
    bucket = jnp.where(n < exact, n, large)
    out = jnp.zeros(n.shape, F32)
    for b in range(REL_BUCKETS):
        out = jnp.where(bucket == b, tab_ref[b, h] * LOG2E, out)
    return out


def _bias_kernel(tab_ref, bc_ref, bd_ref):
    g = pl.program_id(0)
    i = pl.program_id(1)
    ncr = bc_ref.shape[0]
    blk = lax.broadcasted_iota(jnp.int32, (ncr, TQ), 0)
    qry_c = lax.broadcasted_iota(jnp.int32, (ncr, TQ), 1)
    for j in range(NSA_HPG):
        h = g * NSA_HPG + j
        bc_ref[:, j * TQ:(j + 1) * TQ] = _t5_lookup(i * TQ + qry_c - (blk * CMP_STRIDE + CMP_BLOCK - 1), tab_ref, h)

    @pl.when(i == 0)
    def _():
        key = lax.broadcasted_iota(jnp.int32, (TK, TQ), 0)
        qry = lax.broadcasted_iota(jnp.int32, (TK, TQ), 1)
        for v in range(3):
            for j in range(NSA_HPG):
                bd_ref[v, :, j * TQ:(j + 1) * TQ] = _t5_lookup(v * TK + qry - key, tab_ref, g * NSA_HPG + j)


def bias_tables(rel_table, s):
    nq = s // TQ
    cols = NSA_HPG * TQ
    return pl.pallas_call(
        _bias_kernel,
        grid=(NSA_KV_GROUPS, nq),
        in_specs=[pl.BlockSpec(memory_space=pltpu.SMEM)],
        out_specs=[pl.BlockSpec((None, None, s // CMP_STRIDE, cols), lambda g, i: (g, i, 0, 0)),
                   pl.BlockSpec((None, 3, TK, cols), lambda g, i: (g, 0, 0, 0))],
        out_shape=[jax.ShapeDtypeStruct((NSA_KV_GROUPS, nq, s // CMP_STRIDE, cols), F32),
                   jax.ShapeDtypeStruct((NSA_KV_GROUPS, 3, TK, cols), F32)],
        compiler_params=_cparams("arbitrary", "arbitrary"),
        name="t5_bias_tables",
    )(rel_table)


def _gelu_tanh(x):
    return 0.5 * x * (1.0 + jnp.tanh(math.sqrt(2.0 / math.pi) * (x + 0.044715 * (x * x * x))))


def _compress_kernel(x_ref, pe_ref, w1_ref, w2k_ref, w2vt_ref, o_ref, *, nblk):
    pe = pe_ref[...]
    acc_a = jnp.zeros((nblk, CMP_HIDDEN), F32)
    acc_b = jnp.zeros((nblk, CMP_HIDDEN), F32)
    for r in range(CMP_STRIDE):
        xr = x_ref[pl.ds(r, nblk, stride=CMP_STRIDE), :]
        xa = (xr + pe[r:r + 1, :]).astype(BF16)
        xb = (xr + pe[CMP_STRIDE + r:CMP_STRIDE + r + 1, :]).astype(BF16)
        acc_a = acc_a + jnp.dot(xa, w1_ref[r * NSA_HEAD_DIM:(r + 1) * NSA_HEAD_DIM, :], preferred_element_type=F32)
        acc_b = acc_b + jnp.dot(xb, w1_ref[(CMP_STRIDE + r) * NSA_HEAD_DIM:(CMP_STRIDE + r + 1) * NSA_HEAD_DIM, :],
                                preferred_element_type=F32)
    hidden = acc_a + jnp.concatenate([acc_b[1:], acc_b[:1]], axis=0)
    act = _gelu_tanh(hidden).astype(BF16)

    @pl.when(pl.program_id(2) == 0)
    def _():
        o_ref[...] = jnp.dot(act, w2k_ref[...], preferred_element_type=F32).astype(o_ref.dtype)

    @pl.when(pl.program_id(2) == 1)
    def _():
        o_ref[...] = lax.dot_general(w2vt_ref[...], act, (((1,), (1,)), ((), ())),
                                     preferred_element_type=F32).astype(o_ref.dtype)


def nsa_compress(r2, pe, w1, w2k, w2vt):
    b, s, _ = r2.shape
    nblk = s // CMP_STRIDE
    assert CMP_BLOCK == 2 * CMP_STRIDE and nblk == NSA_HEAD_DIM
    return pl.pallas_call(
        functools.partial(_compress_kernel, nblk=nblk),
        grid=(b, NSA_KV_GROUPS, 2),
        in_specs=[
            pl.BlockSpec((None, s, LANES), lambda bi, g, c: (bi, 0, R2_KC + 2 * c + g)),
            pl.BlockSpec((None, CMP_BLOCK, NSA_HEAD_DIM), lambda bi, g, c: (c, 0, 0)),
            pl.BlockSpec((None, CMP_BLOCK * NSA_HEAD_DIM, CMP_HIDDEN), lambda bi, g, c: (c, 0, 0)),
            pl.BlockSpec((CMP_HIDDEN, NSA_HEAD_DIM), lambda bi, g, c: (0, 0)),
            pl.BlockSpec((NSA_HEAD_DIM, CMP_HIDDEN), lambda bi, g, c: (0, 0)),
        ],
        out_specs=pl.BlockSpec((None, None, None, nblk, NSA_HEAD_DIM), lambda bi, g, c: (bi, g, c, 0, 0)),
        out_shape=jax.ShapeDtypeStruct((b, NSA_KV_GROUPS, 2, nblk, NSA_HEAD_DIM), BF16),
        compiler_params=_cparams("parallel", "parallel", "arbitrary"),
        name="nsa_compress",
    )(r2, pe, w1, w2k, w2vt)


def _nsa_attn_kernel(q_ref, ks_ref, kw_ref, vst_ref, vwt_ref, kcvc_ref, gate_ref, biasc_ref, biasd_ref,
                     covert_ref, expandt_ref, o_ref, sel_ref, m_ref, l_ref, acc_ref, res_ref, *, n_sel, top_n):
    g = pl.program_id(1)
    i = pl.program_id(2)
    cols = NSA_HPG * TQ
    nt = (((1,), (1,)), ((), ()))

    qf = q_ref[...]
    qs = jnp.concatenate([qf[:, j * NSA_HEAD_DIM:(j + 1) * NSA_HEAD_DIM] for j in range(NSA_HPG)], axis=0)
    key = lax.broadcasted_iota(jnp.int32, (TK, cols), 0)
    t_loc = lax.broadcasted_iota(jnp.int32, (TK, cols), 1) % TQ

    def gate_row(c):
        return jnp.concatenate(
            [jax.nn.sigmoid(gate_ref[pl.ds((g * NSA_HPG + j) * 3 + c, 1), :]) for j in range(NSA_HPG)], axis=1)

    def lanes4(x):
        return jnp.concatenate([x] * NSA_HPG, axis=1)

    s = lax.dot_general(kcvc_ref[0], qs, nt, preferred_element_type=F32) + biasc_ref[...]
    cmp_end = lax.broadcasted_iota(jnp.int32, s.shape, 0) * CMP_STRIDE + (CMP_BLOCK - 1)
    s = jnp.where(cmp_end <= i * TQ + lax.broadcasted_iota(jnp.int32, s.shape, 1) % TQ, s, NEG_INF)
    mx = jnp.max(s, axis=0, keepdims=True)
    mx = jnp.where(mx == NEG_INF, 0.0, mx)
    e = jnp.exp2(s - mx)
    p = e / jnp.maximum(jnp.sum(e, axis=0, keepdims=True), 1e-30)
    o_c = jnp.dot(kcvc_ref[1], p.astype(BF16), preferred_element_type=F32)
    res_ref[...] = gate_row(0) * o_c

    psum = p[:, 0:TQ]
    for j in range(1, NSA_HPG):
        psum = psum + p[:, j * TQ:(j + 1) * TQ]
    imp = jnp.dot(covert_ref[...], psum, preferred_element_type=F32, precision=HIGHEST)[:n_sel]
    blk = lax.broadcasted_iota(jnp.int32, (n_sel, TQ), 0)
    cur = (i * TQ + lax.broadcasted_iota(jnp.int32, (n_sel, TQ), 1)) // SEL_BLOCK
    causal_blk = blk <= cur
    forced = (blk == 0) | (blk == cur) | (blk == cur - 1)
    score = jnp.where(causal_blk, imp + jnp.where(forced, SEL_FORCE, 0.0), NEG_INF)
    rank = jnp.zeros((n_sel, TQ), jnp.int32)
    for mp in range(n_sel):
        other = score[mp:mp + 1, :]
        beats = (other > score) | ((other == score) & (blk > mp))
        rank = rank + beats.astype(jnp.int32)
    sel = jnp.where((rank < top_n) & causal_blk, 1.0, 0.0)
    sel = jnp.concatenate([sel, jnp.zeros((LANES - n_sel, TQ), F32)], axis=0).astype(BF16)
    sel_ref[...] = jnp.where(jnp.dot(expandt_ref[...], sel, preferred_element_type=F32) > 0.5, 0.0, NEG_INF)

    def reset():
        m_ref[...] = jnp.full(m_ref.shape, NEG_INF, F32)
        l_ref[...] = jnp.zeros(l_ref.shape, F32)
        acc_ref[...] = jnp.zeros(acc_ref.shape, F32)

    far_bias = biasd_ref[2, 0:1, :]

    def attend(items):
        def scores(item):
            return lax.dot_general(item[1], qs, nt, preferred_element_type=F32)

        def softmax_step(item, s):
            st, _, _, variant, mask_add = item
            if variant < 2:
                s = s + biasd_ref[variant]
            if mask_add is not None:
                s = s + mask_add
            smax = jnp.max(s, axis=0, keepdims=True)
            if variant == 2:
                smax = smax + far_bias
            m_prev = m_ref[st]
            m_new = jnp.maximum(m_prev, smax)
            m_safe = jnp.where(m_new == NEG_INF, 0.0, m_new)
            alpha = jnp.exp2(m_prev - m_safe)
            p = jnp.exp2(s + ((far_bias - m_safe) if variant == 2 else -m_safe))
            l_ref[st] = alpha * l_ref[st] + jnp.sum(p, axis=0, keepdims=True)
            m_ref[st] = m_new
            return alpha, p.astype(BF16)

        def value_step(item, alpha, pb):
            st, vt = item[0], item[2]
            acc_ref[st] = alpha * acc_ref[st] + jnp.dot(vt, pb, preferred_element_type=F32)

        s_next = scores(items[0])
        pending = None
        for n, item in enumerate(items):
            s_cur = s_next
            if n + 1 < len(items):
                s_next = scores(items[n + 1])
            alpha, pb = softmax_step(item, s_cur)
            if pending is not None:
                value_step(*pending)
            pending = (item, alpha, pb)
        value_step(*pending)

    def tile(ref, k):
        return ref[pl.ds(pl.multiple_of(k * TK, TK), TK), :]

    causal_add = jnp.where(key <= t_loc, 0.0, NEG_INF)
    edge_add = jnp.where(key > t_loc, 0.0, NEG_INF)
    n_win = WINDOW // TK
    SEL, WIN, SEL2 = 0, 1, 2
    reset()

    def near_items(depth):
        items = []
        for d in range(depth + 1):
            k = i - d
            sel_mask = lanes4(tile(sel_ref, k))
            items.append((SEL, tile(ks_ref, k), vst_ref[k], min(d, 2), sel_mask + causal_add if d == 0 else sel_mask))
            win_mask = causal_add if d == 0 else (edge_add if d == n_win else None)
            items.append((WIN, tile(kw_ref, k), vwt_ref[k], min(d, 2), win_mask))
        return items

    for depth in range(n_win + 1):
        @pl.when((i == depth) if depth < n_win else (i >= depth))
        def _(depth=depth):
            attend(near_items(depth))

    n_far = jnp.maximum(i - n_win, 0)

    def far_item(stream, k):
        return (stream, tile(ks_ref, k), vst_ref[k], 2, lanes4(tile(sel_ref, k)))

    def far_pair(kk, carry):
        attend([far_item(SEL, 2 * kk), far_item(SEL2, 2 * kk + 1)])
        return carry

    lax.fori_loop(0, n_far // 2, far_pair, 0)

    @pl.when(n_far % 2 == 1)
    def _():
        attend([far_item(SEL, n_far - 1)])

    m_a, m_b = m_ref[SEL], m_ref[SEL2]
    m_ab = jnp.maximum(m_a, m_b)
    m_ab = jnp.where(m_ab == NEG_INF, 0.0, m_ab)
    w_a = jnp.exp2(m_a - m_ab)
    w_b = jnp.exp2(m_b - m_ab)
    l_sel = w_a * l_ref[SEL] + w_b * l_ref[SEL2]
    o_sel = (w_a * acc_ref[SEL] + w_b * acc_ref[SEL2]) * (1.0 / jnp.maximum(l_sel, 1e-30))
    o_win = acc_ref[WIN] * (1.0 / jnp.maximum(l_ref[WIN], 1e-30))
    res = res_ref[...] + gate_row(1) * o_sel + gate_row(2) * o_win
    for j in range(NSA_HPG):
        o_ref[:, j * NSA_HEAD_DIM:(j + 1) * NSA_HEAD_DIM] = res[:, j * TQ:(j + 1) * TQ].T.astype(o_ref.dtype)


def nsa_attention(r1, r2, kcvc, bias_c, bias_d):
    b, s, _ = r1.shape
    n_cmp_rows = s // CMP_STRIDE
    n_sel = s // SEL_BLOCK
    assert n_cmp_rows % LANES == 0 and n_sel <= LANES and n_sel % 8 == 0 and WINDOW % TK == 0 and TQ == TK
    assert 2 * TK - (TQ - 1) >= REL_MAX_DIST
    top_n = min(SEL_TOP_N, n_sel)
    nq = s // TQ
    nk = s // TK
    gw = NSA_HPG * NSA_HEAD_DIM
    cols = NSA_HPG * TQ
    dh = NSA_HEAD_DIM

    n_cmp = n_cmp_rows - CMP_BLOCK // CMP_STRIDE + 1
    c_start = np.arange(n_cmp_rows) * CMP_STRIDE
    s_start = np.arange(LANES) * SEL_BLOCK
    cover_t = ((c_start[None, :] <= s_start[:, None] + SEL_BLOCK - 1) & (c_start[None, :] + CMP_BLOCK - 1 >= s_start[:, None])
               & (np.arange(n_cmp_rows)[None, :] < n_cmp) & (np.arange(LANES)[:, None] < n_sel))
    cover_t = jnp.asarray(cover_t.astype(np.float32))
    expand_t = jnp.asarray(((np.arange(s)[:, None] // SEL_BLOCK) == np.arange(LANES)[None, :]).astype(np.float32), BF16)

    v0 = R1_VS * LANES
    v1 = R1_VW * LANES
    vt = jnp.stack([r1[:, :, v0:v0 + NSA_KV_WIDTH], r1[:, :, v1:v1 + NSA_KV_WIDTH]], axis=1)
    vt = vt.reshape(b, 2, nk, TK, NSA_KV_GROUPS, dh).transpose(0, 1, 4, 2, 5, 3)
    g0 = R2_SMALL * LANES + SMALL_GATE
    gate_t = jnp.swapaxes(r2[:, :, g0:g0 + 3 * NSA_HEADS], 1, 2)

    k_spec = lambda blk: pl.BlockSpec((None, s, LANES), lambda bi, g, i, blk=blk: (bi, 0, blk + g))
    vt_spec = lambda w: pl.BlockSpec((None, None, None, nk, dh, TK), lambda bi, g, i, w=w: (bi, w, g, 0, 0, 0))
    return pl.pallas_call(
        functools.partial(_nsa_attn_kernel, n_sel=n_sel, top_n=top_n),
        grid=(b, NSA_KV_GROUPS, nq),
        in_specs=[
            pl.BlockSpec((None, TQ, gw), lambda bi, g, i: (bi, i, g)),
            k_spec(R1_KS), k_spec(R1_KW), vt_spec(0), vt_spec(1),
            pl.BlockSpec((None, None, 2, n_cmp_rows, dh), lambda bi, g, i: (bi, g, 0, 0, 0)),
            pl.BlockSpec((None, 3 * NSA_HEADS, TQ), lambda bi, g, i: (bi, 0, i)),
            pl.BlockSpec((None, None, n_cmp_rows, cols), lambda bi, g, i: (g, i, 0, 0)),
            pl.BlockSpec((None, 3, TK, cols), lambda bi, g, i: (g, 0, 0, 0)),
            pl.BlockSpec((LANES, n_cmp_rows), lambda bi, g, i: (0, 0)),
            pl.BlockSpec((s, LANES), lambda bi, g, i: (0, 0)),
        ],
        out_specs=pl.BlockSpec((None, TQ, gw), lambda bi, g, i: (bi, i, g)),
        out_shape=jax.ShapeDtypeStruct((b, s, NSA_WIDTH), BF16),
        scratch_shapes=[
            pltpu.VMEM((s, TQ), F32),
            pltpu.VMEM((3, 1, cols), F32),
            pltpu.VMEM((3, 1, cols), F32),
            pltpu.VMEM((3, dh, cols), F32),
            pltpu.VMEM((dh, cols), F32),
        ],
        compiler_params=_cparams("parallel", "parallel", "arbitrary"),
        name="nsa_attention",
    )(r1, r1, r1, vt, vt, kcvc, gate_t, bias_c, bias_d, cover_t, expand_t)


GDN_PAIR = 2 * GDN_CHUNK
GDN_GROUP = 4
GDN_ROWS = 512


def _gdn_prep_kernel(q_ref, k_ref, v_ref, smt_ref, alog_ref, dtb_ref, cwq_ref, cwk_ref, cwv_ref,
                     w_out, qg_out, u_out, qk_out, kdt_out, egl_out,
                     kb_s, kk_s, qq_s, vk_s, kd_s, gcol_s, grow_s, xp_s):
    h = pl.program_id(1)
    s = q_ref.shape[0]
    c = GDN_CHUNK
    pr = GDN_PAIR
    dk = GDN_HEAD_DIM
    nt = (((1,), (1,)), ((), ()))
    tn = (((0,), (0,)), ((), ()))

    xg = smt_ref[0:GDN_HEADS, :] + dtb_ref[...]
    g8 = -jnp.exp(alog_ref[...]) * (jnp.maximum(xg, 0.0) + jnp.log1p(jnp.exp(-jnp.abs(xg))))
    beta8 = jax.nn.sigmoid(smt_ref[GDN_HEADS:2 * GDN_HEADS, :])
    pos = lax.broadcasted_iota(jnp.int32, g8.shape, 1) % c
    gcum8 = g8
    sh = 1
    while sh < c:
        gcum8 = gcum8 + jnp.where(pos >= sh, pltpu.roll(gcum8, sh, axis=1), 0.0)
        sh *= 2
    for p in range(s // pr):
        grow_s[p] = gcum8[:, p * pr:(p + 1) * pr]
    sel_k = lax.broadcasted_iota(jnp.int32, (2 * GDN_HEADS, 2 * dk), 0)
    sel_l = lax.broadcasted_iota(jnp.int32, (2 * GDN_HEADS, 2 * dk), 1)
    pick = jnp.where(((sel_k == h) & (sel_l < dk)) | ((sel_k == GDN_HEADS + h) & (sel_l >= dk)), 1.0, 0.0)
    both = jnp.concatenate([gcum8, beta8], axis=0)
    hi = both.astype(BF16)
    rest = both - hi.astype(F32)
    mid = rest.astype(BF16)
    lo = (rest - mid.astype(F32)).astype(BF16)
    col = lax.dot_general(jnp.concatenate([hi, mid, lo], axis=0), jnp.concatenate([pick] * 3, axis=0).astype(BF16),
                          tn, preferred_element_type=F32)
    gcol = col[:, :dk]
    bcol = col[:, dk:]
    gcol_s[...] = gcol

    hist = 8
    xp_s[0:hist, :] = jnp.zeros((hist, dk), F32)

    def conv_silu(x_ref, w_ref):
        x = x_ref[...]
        w = w_ref[...]
        xp_s[hist:hist + s, :] = x
        y = x * w[CONV_WIDTH - 1:CONV_WIDTH, :]
        for sft in range(1, CONV_WIDTH):
            y = y + xp_s[hist - sft:hist - sft + s, :] * w[CONV_WIDTH - 1 - sft:CONV_WIDTH - sft, :]
        return y * jax.nn.sigmoid(y)

    def l2n(x):
        return x * lax.rsqrt(jnp.sum(x * x, axis=-1, keepdims=True) + NORM_EPS)

    q = l2n(conv_silu(q_ref, cwq_ref)) * (dk ** -0.5)
    k = l2n(conv_silu(k_ref, cwk_ref))
    v = conv_silu(v_ref, cwv_ref)
    eg = jnp.exp(gcol)
    kb = k * bcol
    qg_out[...] = (q * eg).astype(BF16)
    kk_s[...] = k.astype(BF16)
    kb_s[...] = kb.astype(BF16)
    qq_s[...] = q.astype(BF16)
    vk_s[:, :dk] = (v * bcol).astype(BF16)
    vk_s[:, dk:] = (kb * eg).astype(BF16)
    g3 = gcol.reshape(s // c, c, dk)
    g_last = g3[:, c - 1:c, :]
    kd_s[...] = (k * jnp.exp(jnp.broadcast_to(g_last, g3.shape).reshape(s, dk) - gcol)).astype(BF16)
    egl_out[...] = jnp.broadcast_to(jnp.exp(g_last), egl_out.shape)

    r_i = lax.broadcasted_iota(jnp.int32, (pr, pr), 0)
    l_i = lax.broadcasted_iota(jnp.int32, (pr, pr), 1)
    tri = ((r_i // c) == (l_i // c)) & (r_i >= l_i)
    eye = r_i == l_i
    eye_f = jnp.where(eye, 1.0, 0.0)
    eye_b = eye_f.astype(BF16)

    def pair_group(ps):
        rows = [slice(p * pr, (p + 1) * pr) for p in ps]
        n = len(ps)
        x = [lax.dot_general(jnp.concatenate([kb_s[r, :], qq_s[r, :]], axis=0), kk_s[r, :], nt,
                             preferred_element_type=F32) for r in rows]
        decay = []
        for p, r in zip(ps, rows):
            diff = gcol_s[r, :] - grow_s[p, pl.ds(h, 1), :]
            decay.append(jnp.where(tri, jnp.exp(jnp.where(tri, diff, 0.0)), 0.0))
        for a in range(n):
            qk_out[rows[a], :] = (x[a][pr:] * decay[a]).astype(BF16)
            kdt_out[ps[a]] = lax.dot_general(eye_b, kd_s[rows[a], :], nt, preferred_element_type=F32).astype(BF16)
        def split(m):
            hi = m.astype(BF16)
            return hi, (m - hi.astype(F32)).astype(BF16)

        def mm3(lhs, rhs):
            return (jnp.dot(lhs[0], rhs[0], preferred_element_type=F32)
                    + jnp.dot(lhs[1], rhs[0], preferred_element_type=F32)
                    + jnp.dot(lhs[0], rhs[1], preferred_element_type=F32))

        def stack(u, v):
            return (jnp.concatenate([u[0], v[0]], axis=0), jnp.concatenate([u[1], v[1]], axis=0))

        qm = [jnp.where(eye, 0.0, -(x[a][:pr] * decay[a])) for a in range(n)]
        tinv = [eye_f + qm[a] for a in range(n)]
        qs = [split(qm[a]) for a in range(n)]
        qm = [mm3(qs[a], qs[a]) for a in range(n)]
        steps = int(math.log2(c)) - 1
        for it in range(steps):
            qs = [split(qm[a]) for a in range(n)]
            ts = [split(tinv[a]) for a in range(n)]
            if it < steps - 1:
                r = [mm3(stack(qs[a], ts[a]), qs[a]) for a in range(n)]
                qm = [r[a][:pr] for a in range(n)]
                tinv = [tinv[a] + r[a][pr:] for a in range(n)]
            else:
                tinv = [tinv[a] + mm3(ts[a], qs[a]) for a in range(n)]
        for a in range(n):
            uw = jnp.dot(tinv[a].astype(BF16), vk_s[rows[a], :], preferred_element_type=F32)
            u_out[rows[a], :] = uw[:, :dk]
            w_out[rows[a], :] = uw[:, dk:].astype(BF16)

    npair = s // pr
    for p0 in range(0, npair, GDN_GROUP):
        pair_group(list(range(p0, min(p0 + GDN_GROUP, npair))))


def _gdn_scan_kernel(w_ref, qg_ref, u_ref, qk_ref, kdt_ref, egl_ref, z_ref, nw_ref, o_ref, state_ref, os_ref):
    c = GDN_CHUNK
    pr = GDN_PAIR
    dk = GDN_HEAD_DIM
    nh = w_ref.shape[0]
    rb = w_ref.shape[1]

    @pl.when(pl.program_id(1) == 0)
    def _():
        state_ref[...] = jnp.zeros(state_ref.shape, F32)

    zeros = jnp.zeros((c, dk), BF16)

    def pair(p, carry):
        r0 = pl.multiple_of(p * pr, pr)
        for half in range(2):
            rows = pl.ds(r0 + half * c, c)
            state = [state_ref[h] for h in range(nh)]
            r1 = [jnp.dot(jnp.concatenate([w_ref[h, rows, :], qg_ref[h, rows, :]], axis=0), state[h].astype(BF16),
                          preferred_element_type=F32) for h in range(nh)]
            vnb = [(u_ref[h, rows, :] - r1[h][:c]).astype(BF16) for h in range(nh)]
            rhs = [jnp.concatenate([vnb[h], zeros] if half == 0 else [zeros, vnb[h]], axis=0) for h in range(nh)]
            r2 = [jnp.dot(jnp.concatenate([qk_ref[h, rows, :], kdt_ref[h, p]], axis=0), rhs[h],
                          preferred_element_type=F32) for h in range(nh)]
            for h in range(nh):
                os_ref[rows, h * dk:(h + 1) * dk] = r1[h][c:] + r2[h][:c]
                state_ref[h] = state[h] * egl_ref[h, 2 * p + half, 0:1, :] + r2[h][c:]
        return carry

    lax.fori_loop(0, rb // pr, pair, 0)

    for h in range(nh):
        o = os_ref[:, h * dk:(h + 1) * dk]
        o = o * lax.rsqrt(jnp.mean(o * o, axis=-1, keepdims=True) + NORM_EPS) * nw_ref[...]
        z = z_ref[:, h * dk:(h + 1) * dk]
        o_ref[:, h * dk:(h + 1) * dk] = (o * (z * jax.nn.sigmoid(z))).astype(o_ref.dtype)


def gated_deltanet(r2, conv_w, a_log, dt_bias, norm_w):
    b, s, _ = r2.shape
    dk = GDN_HEAD_DIM
    nh = GDN_HEADS
    npair = s // GDN_PAIR
    nchunk = s // GDN_CHUNK
    rb = GDN_ROWS
    assert s % GDN_PAIR == 0 and s % rb == 0 and (R2_Z * LANES) % GDN_WIDTH == 0
    a0 = R2_SMALL * LANES + SMALL_A
    smt = jnp.swapaxes(r2[:, :, a0:a0 + 2 * nh], 1, 2)
    col = lambda blk: pl.BlockSpec((None, s, dk), lambda bi, h, blk=blk: (bi, 0, blk + h))
    cw = lambda blk: pl.BlockSpec((CONV_WIDTH, dk), lambda bi, h, blk=blk: (0, blk + h))
    vec = pl.BlockSpec((nh, 1), lambda bi, h: (0, 0))
    per_head = lambda *shape: pl.BlockSpec((None, None) + shape, lambda bi, h: (bi, h) + (0,) * len(shape))
    w, qg, u, qk, kdt, egl = pl.pallas_call(
        _gdn_prep_kernel,
        grid=(b, nh),
        in_specs=[col(R2_GQ), col(R2_GK), col(R2_GV),
                  pl.BlockSpec((None, 2 * nh, s), lambda bi, h: (bi, 0, 0)), vec, vec,
                  cw(0), cw(nh), cw(2 * nh)],
        out_specs=[per_head(s, dk), per_head(s, dk), per_head(s, dk), per_head(s, dk),
                   per_head(npair, GDN_PAIR, GDN_PAIR), per_head(nchunk, 8, dk)],
        out_shape=[jax.ShapeDtypeStruct((b, nh, s, dk), BF16), jax.ShapeDtypeStruct((b, nh, s, dk), BF16),
                   jax.ShapeDtypeStruct((b, nh, s, dk), F32), jax.ShapeDtypeStruct((b, nh, s, dk), BF16),
                   jax.ShapeDtypeStruct((b, nh, npair, GDN_PAIR, GDN_PAIR), BF16),
                   jax.ShapeDtypeStruct((b, nh, nchunk, 8, dk), F32)],
        scratch_shapes=[pltpu.VMEM((s, dk), BF16), pltpu.VMEM((s, dk), BF16), pltpu.VMEM((s, dk), BF16),
                        pltpu.VMEM((s, 2 * dk), BF16), pltpu.VMEM((s, dk), BF16), pltpu.VMEM((s, dk), F32),
                        pltpu.VMEM((npair, nh, GDN_PAIR), F32), pltpu.VMEM((s + 8, dk), F32)],
        compiler_params=_cparams("parallel", "parallel"),
        name="gdn_prep",
    )(r2, r2, r2, smt, a_log.reshape(nh, 1), dt_bias.reshape(nh, 1), conv_w, conv_w, conv_w)

    seq = lambda *shape: pl.BlockSpec((None, nh) + shape, lambda bi, j: (bi, 0, j) + (0,) * (len(shape) - 1))
    return pl.pallas_call(
        _gdn_scan_kernel,
        grid=(b, s // rb),
        in_specs=[seq(rb, dk), seq(rb, dk), seq(rb, dk), seq(rb, dk),
                  seq(rb // GDN_PAIR, GDN_PAIR, GDN_PAIR), seq(rb // GDN_CHUNK, 8, dk),
                  pl.BlockSpec((None, rb, GDN_WIDTH), lambda bi, j: (bi, j, R2_Z * LANES // GDN_WIDTH)),
                  pl.BlockSpec((1, dk), lambda bi, j: (0, 0))],
        out_specs=pl.BlockSpec((None, rb, GDN_WIDTH), lambda bi, j: (bi, j, 0)),
        out_shape=jax.ShapeDtypeStruct((b, s, GDN_WIDTH), BF16),
        scratch_shapes=[pltpu.VMEM((nh, dk, dk), F32), pltpu.VMEM((rb, GDN_WIDTH), F32)],
        compiler_params=_cparams("parallel", "arbitrary"),
        name="gdn_scan",
    )(w, qg, u, qk, kdt, egl, r2, norm_w.reshape(1, dk))


def _split_w_in(w_in):
    o_q = 0
    o_kv = NSA_WIDTH
    o_gate = o_kv + 6 * NSA_KV_WIDTH
    o_qkv = o_gate + 3 * NSA_HEADS
    o_z = o_qkv + 3 * GDN_WIDTH
    o_a = o_z + GDN_WIDTH
    o_b = o_a + GDN_HEADS
    o_m = o_b + GDN_HEADS
    kvw = NSA_KV_WIDTH
    w1 = jnp.concatenate([w_in[:, o_q:o_kv], w_in[:, o_kv + 2 * kvw:o_kv + 6 * kvw]], axis=1)
    small = jnp.concatenate([w_in[:, o_gate:o_qkv], w_in[:, o_a:o_m]], axis=1)
    used = (R2_SMALL * LANES) + small.shape[1]
    w2 = jnp.concatenate([w_in[:, o_z:o_a], w_in[:, o_kv:o_kv + 2 * kvw], w_in[:, o_qkv:o_z], w_in[:, o_m:], small,
                          jnp.zeros((w_in.shape[0], R2_COLS - used), w_in.dtype)], axis=1)
    return jnp.concatenate([w1, w2], axis=1).astype(BF16)


def _r1_col_scale():
    c = np.ones((1, R1_COLS), np.float32)
    c[:, :NSA_WIDTH] = NSA_HEAD_DIM ** -0.5 * LOG2E
    return jnp.asarray(c)


def _layer(x, bias_c, bias_d, ln1_w, w_in, pe_k, pe_v, w1_k, w2_k, w1_v, w2_v, conv_w, a_log, dt_bias,
           gdn_norm_w, w_pa, w_pb, w_o, ln2_w, w_up, w_down, final_norm_w, b, s):
    t = b * s
    r1, r2 = input_projection(x, ln1_w, _split_w_in(w_in), _r1_col_scale())
    r1 = r1.reshape(b, s, R1_COLS)
    r2 = r2.reshape(b, s, R2_COLS)
    kcvc = nsa_compress(r2, jnp.stack([pe_k, pe_v]), jnp.stack([w1_k, w1_v]).astype(BF16),
                        w2_k.astype(BF16), w2_v.T.astype(BF16))
    o_a = nsa_attention(r1, r2, kcvc, bias_c, bias_d).reshape(t, NSA_WIDTH)
    o_b = gated_deltanet(r2, conv_w, a_log, dt_bias, gdn_norm_w).reshape(t, GDN_WIDTH)
    m = merge_branches(o_a, o_b, w_pa.astype(BF16), w_pb.astype(BF16), r2.reshape(t, R2_COLS))
    x, h2 = matmul(m, w_o.astype(BF16), F32, tm=512, tn=D_MODEL, tk=D_MODEL, epilogue="residual_norm", res=x,
                   norm_w=ln2_w, norm_dtype=BF16)
    u = matmul(h2, w_up.astype(BF16), BF16, tm=1024, tn=1024, tk=D_MODEL, epilogue="relu2")
    x = matmul(u, w_down.astype(BF16), F32, tm=512, tn=512, tk=D_FF, epilogue="residual", res=x)
    if final_norm_w is None:
        return x
    return rmsnorm(x, final_norm_w, F32)


def kernel(x, rel_table, ln1_w, w_in, cmp_pe_k, cmp_pe_v, cmp_w1_k, cmp_w2_k, cmp_w1_v, cmp_w2_v, conv_w, a_log,
           dt_bias, gdn_norm_w, w_pa, w_pb, w_o, ln2_w, w_up, w_down, ln_f_w):
    b, s, d = x.shape
    depth = w_in.shape[0]
    bias_c, bias_d = bias_tables(rel_table, s)
    xt = x.reshape(b * s, d)
    for l in range(depth):
        xt = _layer(xt, bias_c, bias_d, ln1_w[l], w_in[l], cmp_pe_k[l], cmp_pe_v[l], cmp_w1_k[l], cmp_w2_k[l],
                    cmp_w1_v[l], cmp_w2_v[l], conv_w[l], a_log[l], dt_bias[l], gdn_norm_w[l], w_pa[l], w_pb[l],
                    w_o[l], ln2_w[l], w_up[l], w_down[l], ln_f_w if l == depth - 1 else None, b, s)
    return xt.reshape(b, s, d)
```

```python
import functools
import math

import numpy as np
import jax
import jax.numpy as jnp
from jax import lax
from jax.experimental import pallas as pl
from jax.experimental.pallas import tpu as pltpu

F32 = jnp.float32
BF16 = jnp.bfloat16
HIGHEST = lax.Precision.HIGHEST

D_MODEL = 2048
DEPTH = 2
NSA_HEADS = 8
NSA_KV_GROUPS = 2
NSA_HPG = NSA_HEADS // NSA_KV_GROUPS
NSA_HEAD_DIM = 128
NSA_WIDTH = NSA_HEADS * NSA_HEAD_DIM
NSA_KV_WIDTH = NSA_KV_GROUPS * NSA_HEAD_DIM
CMP_BLOCK = 32
CMP_STRIDE = 16
CMP_HIDDEN = 256
SEL_BLOCK = 64
SEL_TOP_N = 16
SEL_FORCE = 1000.0
WINDOW = 512
GDN_HEADS = 8
GDN_HEAD_DIM = 128
GDN_WIDTH = GDN_HEADS * GDN_HEAD_DIM
CONV_WIDTH = 4
GDN_CHUNK = 64
REL_BUCKETS = 32
REL_MAX_DIST = 128
D_FF = 4 * D_MODEL
NORM_EPS = 1e-6

LANES = 128
VMEM_LIMIT = 56 * 1024 * 1024
TQ = 256
TK = 256
NEG_INF = float("-inf")
LOG2E = math.log2(math.e)

R1_KS, R1_VS, R1_KW, R1_VW, R1_MA, R1_MB = 8, 10, 12, 14, 16, 32
R1_COLS = 48 * LANES
R2_Z, R2_KC, R2_VC, R2_GQ, R2_GK, R2_GV, R2_SMALL = 0, 8, 10, 12, 20, 28, 36
R2_COLS = 40 * LANES
SMALL_GATE, SMALL_A, SMALL_B = 0, 24, 32


def _t5_thresholds():
    exact = REL_BUCKETS // 2
    n = np.arange(exact, REL_MAX_DIST * 2, dtype=np.int64)
    nf = n.astype(np.float32)
    large = exact + (np.log(nf / np.float32(exact)) / np.float32(math.log(REL_MAX_DIST / exact))
                     * np.float32(REL_BUCKETS - exact)).astype(np.int32)
    large = np.minimum(large, REL_BUCKETS - 1)
    thr = []
    for b in range(exact + 1, REL_BUCKETS):
        thr.append(int(n[np.argmax(large >= b)]))
    return tuple(thr)


T5_THRESHOLDS = _t5_thresholds()


def _cparams(*sem):
    return pltpu.CompilerParams(dimension_semantics=sem, vmem_limit_bytes=VMEM_LIMIT)


def _rmsnorm_kernel(x_ref, w_ref, o_ref):
    x = x_ref[...]
    y = x * lax.rsqrt(jnp.mean(x * x, axis=-1, keepdims=True) + NORM_EPS)
    o_ref[...] = (y * w_ref[...]).astype(o_ref.dtype)


def rmsnorm(x, w, out_dtype, tm=512):
    t, d = x.shape
    return pl.pallas_call(
        _rmsnorm_kernel,
        grid=(t // tm,),
        in_specs=[pl.BlockSpec((tm, d), lambda i: (i, 0)), pl.BlockSpec((1, d), lambda i: (0, 0))],
        out_specs=pl.BlockSpec((tm, d), lambda i: (i, 0)),
        out_shape=jax.ShapeDtypeStruct((t, d), out_dtype),
        compiler_params=_cparams("parallel"),
        name="rmsnorm",
    )(x, w.reshape(1, d))


def _mm_kernel(*refs, nk, epilogue):
    a_ref, w_ref = refs[0], refs[1]
    pos = 2
    res_ref = norm_ref = None
    if epilogue in ("residual", "residual_norm"):
        res_ref = refs[pos]
        pos += 1
    if epilogue == "residual_norm":
        norm_ref = refs[pos]
        pos += 1
    o_ref = refs[pos]
    pos += 1
    n_ref = None
    if epilogue == "residual_norm":
        n_ref = refs[pos]
        pos += 1
    acc_ref = refs[pos] if nk > 1 else None

    def finish(acc):
        if epilogue == "relu2":
            acc = jnp.square(jnp.maximum(acc, 0.0))
        elif epilogue == "residual":
            acc = res_ref[...] + acc
        elif epilogue == "residual_norm":
            acc = res_ref[...] + acc
            normed = acc * lax.rsqrt(jnp.mean(acc * acc, axis=-1, keepdims=True) + NORM_EPS) * norm_ref[...]
            n_ref[...] = normed.astype(n_ref.dtype)
        o_ref[...] = acc.astype(o_ref.dtype)

    part = jnp.dot(a_ref[...], w_ref[...], preferred_element_type=F32)
    if nk == 1:
        finish(part)
    else:
        k = pl.program_id(2)

        @pl.when(k == 0)
        def _():
            acc_ref[...] = part

        @pl.when(k > 0)
        def _():
            acc_ref[...] += part

        @pl.when(k == nk - 1)
        def _():
            finish(acc_ref[...])


def _column_tiles(w, tn):
    k, n = w.shape
    return w.reshape(k, n // tn, tn).transpose(1, 0, 2)


def matmul(a, w, out_dtype, *, tm, tn, tk, epilogue="none", res=None, norm_w=None, norm_dtype=None):
    m, kdim = a.shape
    n = w.shape[1]
    nk = kdim // tk
    if nk == 1 and tn < n:
        w = _column_tiles(w, tn)
        w_spec = pl.BlockSpec((None, tk, tn), lambda i, j, k: (j, 0, 0))
    else:
        w_spec = pl.BlockSpec((tk, tn), lambda i, j, k: (k, j))
    in_specs = [pl.BlockSpec((tm, tk), lambda i, j, k: (i, k)), w_spec]
    args = [a, w]
    if epilogue in ("residual", "residual_norm"):
        in_specs.append(pl.BlockSpec((tm, tn), lambda i, j, k: (i, j)))
        args.append(res)
    if epilogue == "residual_norm":
        assert tn == n
        in_specs.append(pl.BlockSpec((1, tn), lambda i, j, k: (0, 0)))
        args.append(norm_w.reshape(1, n))
    scratch = [pltpu.VMEM((tm, tn), F32)] if nk > 1 else []
    out_spec = pl.BlockSpec((tm, tn), lambda i, j, k: (i, j))
    out_specs, out_shape = out_spec, jax.ShapeDtypeStruct((m, n), out_dtype)
    if epilogue == "residual_norm":
        out_specs = [out_spec, out_spec]
        out_shape = [out_shape, jax.ShapeDtypeStruct((m, n), norm_dtype)]
    return pl.pallas_call(
        functools.partial(_mm_kernel, nk=nk, epilogue=epilogue),
        grid=(m // tm, n // tn, nk),
        in_specs=in_specs,
        out_specs=out_specs,
        out_shape=out_shape,
        scratch_shapes=scratch,
        compiler_params=_cparams("parallel", "parallel", "arbitrary"),
        name="matmul_" + epilogue,
    )(*args)


def _inproj_kernel(x_ref, lnw_ref, w_ref, cs_ref, r1_ref, r2_ref, h_ref, *, n1):
    j = pl.program_id(1)

    @pl.when(j == 0)
    def _():
        x = x_ref[...]
        y = x * lax.rsqrt(jnp.mean(x * x, axis=-1, keepdims=True) + NORM_EPS)
        h_ref[...] = (y * lnw_ref[...]).astype(BF16)

    acc = jnp.dot(h_ref[...], w_ref[...], preferred_element_type=F32)

    @pl.when(j < n1)
    def _():
        r1_ref[...] = (acc * cs_ref[...]).astype(r1_ref.dtype)

    @pl.when(j >= n1)
    def _():
        r2_ref[...] = acc


def input_projection(x, ln_w, w_all, col_scale, *, tm=1024, tn=1024):
    t, d = x.shape
    n1 = R1_COLS // tn
    n2 = R2_COLS // tn
    assert w_all.shape == (d, R1_COLS + R2_COLS) and R1_COLS % tn == 0 and R2_COLS % tn == 0
    return pl.pallas_call(
        functools.partial(_inproj_kernel, n1=n1),
        grid=(t // tm, n1 + n2),
        in_specs=[pl.BlockSpec((tm, d), lambda i, j: (i, 0)),
                  pl.BlockSpec((1, d), lambda i, j: (0, 0)),
                  pl.BlockSpec((None, d, tn), lambda i, j: (j, 0, 0)),
                  pl.BlockSpec((1, tn), lambda i, j: (0, jnp.minimum(j, n1 - 1)))],
        out_specs=[pl.BlockSpec((tm, tn), lambda i, j: (i, jnp.minimum(j, n1 - 1))),
                   pl.BlockSpec((tm, tn), lambda i, j: (i, jnp.maximum(j - n1, 0)))],
        out_shape=[jax.ShapeDtypeStruct((t, R1_COLS), BF16), jax.ShapeDtypeStruct((t, R2_COLS), F32)],
        scratch_shapes=[pltpu.VMEM((tm, d), BF16)],
        compiler_params=_cparams("parallel", "arbitrary"),
        name="input_projection",
    )(x, ln_w.reshape(1, d), _column_tiles(w_all, tn), col_scale)


def _merge_kernel(oa_ref, ob_ref, wa_ref, wb_ref, ma_ref, mb_ref, o_ref):
    pa = jnp.dot(oa_ref[...], wa_ref[...], preferred_element_type=F32)
    pb = jnp.dot(ob_ref[...], wb_ref[...], preferred_element_type=F32)
    ga = jax.nn.sigmoid(ma_ref[...].astype(F32))
    gb = jax.nn.sigmoid(mb_ref[...].astype(F32))
    o_ref[...] = (ga * pa + gb * pb).astype(o_ref.dtype)


def merge_branches(o_a, o_b, w_pa, w_pb, r1, *, tm=1024, tn=512):
    t, ka = o_a.shape
    kb = o_b.shape[1]
    n = w_pa.shape[1]
    ma0 = R1_MA * LANES // tn
    mb0 = R1_MB * LANES // tn
    return pl.pallas_call(
        _merge_kernel,
        grid=(t // tm, n // tn),
        in_specs=[
            pl.BlockSpec((tm, ka), lambda i, j: (i, 0)),
            pl.BlockSpec((tm, kb), lambda i, j: (i, 0)),
            pl.BlockSpec((ka, tn), lambda i, j: (0, j)),
            pl.BlockSpec((kb, tn), lambda i, j: (0, j)),
            pl.BlockSpec((tm, tn), lambda i, j: (i, ma0 + j)),
            pl.BlockSpec((tm, tn), lambda i, j: (i, mb0 + j)),
        ],
        out_specs=pl.BlockSpec((tm, tn), lambda i, j: (i, j)),
        out_shape=jax.ShapeDtypeStruct((t, n), BF16),
        compiler_params=_cparams("parallel", "parallel"),
        name="merge_branches",
    )(o_a, o_b, w_pa, w_pb, r1, r1)


def _t5_lookup(dist, tab_ref, h):
    n = jnp.maximum(dist, 0)
    exact = REL_BUCKETS // 2
    large = jnp.full(n.shape, exact, jnp.int32)
    for thr in T5_THRESHOLDS:
        large = large + (n >= thr).astype(jnp.int32)
    bucket = jnp.where(n < exact, n, large)
    out = jnp.zeros(n.shape, F32)
    for b in range(REL_BUCKETS):
        out = jnp.where(bucket == b, tab_ref[b, h] * LOG2E, out)
    return out


def _bias_kernel(tab_ref, bc_ref, bd_ref):
    g = pl.program_id(0)
    i = pl.program_id(1)
    ncr = bc_ref.shape[0]
    blk = lax.broadcasted_iota(jnp.int32, (ncr, TQ), 0)
    qry_c = lax.broadcasted_iota(jnp.int32, (ncr, TQ), 1)
    for j in range(NSA_HPG):
        h = g * NSA_HPG + j
        bc_ref[:, j * TQ:(j + 1) * TQ] = _t5_lookup(i * TQ + qry_c - (blk * CMP_STRIDE + CMP_BLOCK - 1), tab_ref, h)

    @pl.when(i == 0)
    def _():
        key = lax.broadcasted_iota(jnp.int32, (TK, TQ), 0)
        qry = lax.broadcasted_iota(jnp.int32, (TK, TQ), 1)
        for v in range(3):
            for j in range(NSA_HPG):
                bd_ref[v, :, j * TQ:(j + 1) * TQ] = _t5_lookup(v * TK + qry - key, tab_ref, g * NSA_HPG + j)


def bias_tables(rel_table, s):
    nq = s // TQ
    cols = NSA_HPG * TQ
    return pl.pallas_call(
        _bias_kernel,
        grid=(NSA_KV_GROUPS, nq),
        in_specs=[pl.BlockSpec(memory_space=pltpu.SMEM)],
        out_specs=[pl.BlockSpec((None, None, s // CMP_STRIDE, cols), lambda g, i: (g, i, 0, 0)),
                   pl.BlockSpec((None, 3, TK, cols), lambda g, i: (g, 0, 0, 0))],
        out_shape=[jax.ShapeDtypeStruct((NSA_KV_GROUPS, nq, s // CMP_STRIDE, cols), F32),
                   jax.ShapeDtypeStruct((NSA_KV_GROUPS, 3, TK, cols), F32)],
        compiler_params=_cparams("arbitrary", "arbitrary"),
        name="t5_bias_tables",
    )(rel_table)


def _gelu_tanh(x):
    return 0.5 * x * (1.0 + jnp.tanh(math.sqrt(2.0 / math.pi) * (x + 0.044715 * (x * x * x))))


def _compress_kernel(x_ref, pe_ref, w1_ref, w2k_ref, w2vt_ref, o_ref, *, nblk):
    pe = pe_ref[...]
    acc_a = jnp.zeros((nblk, CMP_HIDDEN), F32)
    acc_b = jnp.zeros((nblk, CMP_HIDDEN), F32)
    for r in range(CMP_STRIDE):
        xr = x_ref[pl.ds(r, nblk, stride=CMP_STRIDE), :]
        xa = (xr + pe[r:r + 1, :]).astype(BF16)
        xb = (xr + pe[CMP_STRIDE + r:CMP_STRIDE + r + 1, :]).astype(BF16)
        acc_a = acc_a + jnp.dot(xa, w1_ref[r * NSA_HEAD_DIM:(r + 1) * NSA_HEAD_DIM, :], preferred_element_type=F32)
        acc_b = acc_b + jnp.dot(xb, w1_ref[(CMP_STRIDE + r) * NSA_HEAD_DIM:(CMP_STRIDE + r + 1) * NSA_HEAD_DIM, :],
                                preferred_element_type=F32)
    hidden = acc_a + jnp.concatenate([acc_b[1:], acc_b[:1]], axis=0)
    act = _gelu_tanh(hidden).astype(BF16)

    @pl.when(pl.program_id(2) == 0)
    def _():
        o_ref[...] = jnp.dot(act, w2k_ref[...], preferred_element_type=F32).astype(o_ref.dtype)

    @pl.when(pl.program_id(2) == 1)
    def _():
        o_ref[...] = lax.dot_general(w2vt_ref[...], act, (((1,), (1,)), ((), ())),
                                     preferred_element_type=F32).astype(o_ref.dtype)


def nsa_compress(r2, pe, w1, w2k, w2vt):
    b, s, _ = r2.shape
    nblk = s // CMP_STRIDE
    assert CMP_BLOCK == 2 * CMP_STRIDE and nblk == NSA_HEAD_DIM
    return pl.pallas_call(
        functools.partial(_compress_kernel, nblk=nblk),
        grid=(b, NSA_KV_GROUPS, 2),
        in_specs=[
            pl.BlockSpec((None, s, LANES), lambda bi, g, c: (bi, 0, R2_KC + 2 * c + g)),
            pl.BlockSpec((None, CMP_BLOCK, NSA_HEAD_DIM), lambda bi, g, c: (c, 0, 0)),
            pl.BlockSpec((None, CMP_BLOCK * NSA_HEAD_DIM, CMP_HIDDEN), lambda bi, g, c: (c, 0, 0)),
            pl.BlockSpec((CMP_HIDDEN, NSA_HEAD_DIM), lambda bi, g, c: (0, 0)),
            pl.BlockSpec((NSA_HEAD_DIM, CMP_HIDDEN), lambda bi, g, c: (0, 0)),
        ],
        out_specs=pl.BlockSpec((None, None, None, nblk, NSA_HEAD_DIM), lambda bi, g, c: (bi, g, c, 0, 0)),
        out_shape=jax.ShapeDtypeStruct((b, NSA_KV_GROUPS, 2, nblk, NSA_HEAD_DIM), BF16),
        compiler_params=_cparams("parallel", "parallel", "arbitrary"),
        name="nsa_compress",
    )(r2, pe, w1, w2k, w2vt)


def _nsa_attn_kernel(q_ref, ks_ref, kw_ref, vst_ref, vwt_ref, kcvc_ref, gate_ref, biasc_ref, biasd_ref,
                     covert_ref, expandt_ref, o_ref, sel_ref, m_ref, l_ref, acc_ref, res_ref, *, n_sel, top_n):
    g = pl.program_id(1)
    i = pl.program_id(2)
    cols = NSA_HPG * TQ
    nt = (((1,), (1,)), ((), ()))

    qf = q_ref[...]
    qs = jnp.concatenate([qf[:, j * NSA_HEAD_DIM:(j + 1) * NSA_HEAD_DIM] for j in range(NSA_HPG)], axis=0)
    key = lax.broadcasted_iota(jnp.int32, (TK, cols), 0)
    t_loc = lax.broadcasted_iota(jnp.int32, (TK, cols), 1) % TQ

    def gate_row(c):
        return jnp.concatenate(
            [jax.nn.sigmoid(gate_ref[pl.ds((g * NSA_HPG + j) * 3 + c, 1), :]) for j in range(NSA_HPG)], axis=1)

    def lanes4(x):
        return jnp.concatenate([x] * NSA_HPG, axis=1)

    s = lax.dot_general(kcvc_ref[0], qs, nt, preferred_element_type=F32) + biasc_ref[...]
    cmp_end = lax.broadcasted_iota(jnp.int32, s.shape, 0) * CMP_STRIDE + (CMP_BLOCK - 1)
    s = jnp.where(cmp_end <= i * TQ + lax.broadcasted_iota(jnp.int32, s.shape, 1) % TQ, s, NEG_INF)
    mx = jnp.max(s, axis=0, keepdims=True)
    mx = jnp.where(mx == NEG_INF, 0.0, mx)
    e = jnp.exp2(s - mx)
    p = e / jnp.maximum(jnp.sum(e, axis=0, keepdims=True), 1e-30)
    o_c = jnp.dot(kcvc_ref[1], p.astype(BF16), preferred_element_type=F32)
    res_ref[...] = gate_row(0) * o_c

    psum = p[:, 0:TQ]
    for j in range(1, NSA_HPG):
        psum = psum + p[:, j * TQ:(j + 1) * TQ]
    imp = jnp.dot(covert_ref[...], psum, preferred_element_type=F32, precision=HIGHEST)[:n_sel]
    blk = lax.broadcasted_iota(jnp.int32, (n_sel, TQ), 0)
    cur = (i * TQ + lax.broadcasted_iota(jnp.int32, (n_sel, TQ), 1)) // SEL_BLOCK
    causal_blk = blk <= cur
    forced = (blk == 0) | (blk == cur) | (blk == cur - 1)
    score = jnp.where(causal_blk, imp + jnp.where(forced, SEL_FORCE, 0.0), NEG_INF)
    rank = jnp.zeros((n_sel, TQ), jnp.int32)
    for mp in range(n_sel):
        other = score[mp:mp + 1, :]
        beats = (other > score) | ((other == score) & (blk > mp))
        rank = rank + beats.astype(jnp.int32)
    sel = jnp.where((rank < top_n) & causal_blk, 1.0, 0.0)
    sel = jnp.concatenate([sel, jnp.zeros((LANES - n_sel, TQ), F32)], axis=0).astype(BF16)
    sel_ref[...] = jnp.where(jnp.dot(expandt_ref[...], sel, preferred_element_type=F32) > 0.5, 0.0, NEG_INF)

    def reset():
        m_ref[...] = jnp.full(m_ref.shape, NEG_INF, F32)
        l_ref[...] = jnp.zeros(l_ref.shape, F32)
        acc_ref[...] = jnp.zeros(acc_ref.shape, F32)

    far_bias = biasd_ref[2, 0:1, :]

    def attend(items):
        def scores(item):
            return lax.dot_general(item[1], qs, nt, preferred_element_type=F32)

        def softmax_step(item, s):
            st, _, _, variant, mask_add = item
            if variant < 2:
                s = s + biasd_ref[variant]
            if mask_add is not None:
                s = s + mask_add
            smax = jnp.max(s, axis=0, keepdims=True)
            if variant == 2:
                smax = smax + far_bias
            m_prev = m_ref[st]
            m_new = jnp.maximum(m_prev, smax)
            m_safe = jnp.where(m_new == NEG_INF, 0.0, m_new)
            alpha = jnp.exp2(m_prev - m_safe)
            p = jnp.exp2(s + ((far_bias - m_safe) if variant == 2 else -m_safe))
            l_ref[st] = alpha * l_ref[st] + jnp.sum(p, axis=0, keepdims=True)
            m_ref[st] = m_new
            return alpha, p.astype(BF16)

        def value_step(item, alpha, pb):
            st, vt = item[0], item[2]
            acc_ref[st] = alpha * acc_ref[st] + jnp.dot(vt, pb, preferred_element_type=F32)

        s_next = scores(items[0])
        pending = None
        for n, item in enumerate(items):
            s_cur = s_next
            if n + 1 < len(items):
                s_next = scores(items[n + 1])
            alpha, pb = softmax_step(item, s_cur)
            if pending is not None:
                value_step(*pending)
            pending = (item, alpha, pb)
        value_step(*pending)

    def tile(ref, k):
        return ref[pl.ds(pl.multiple_of(k * TK, TK), TK), :]

    causal_add = jnp.where(key <= t_loc, 0.0, NEG_INF)
    edge_add = jnp.where(key > t_loc, 0.0, NEG_INF)
    n_win = WINDOW // TK
    SEL, WIN, SEL2 = 0, 1, 2
    reset()

    def near_items(depth):
        items = []
        for d in range(depth + 1):
            k = i - d
            sel_mask = lanes4(tile(sel_ref, k))
            items.append((SEL, tile(ks_ref, k), vst_ref[k], min(d, 2), sel_mask + causal_add if d == 0 else sel_mask))
            win_mask = causal_add if d == 0 else (edge_add if d == n_win else None)
            items.append((WIN, tile(kw_ref, k), vwt_ref[k], min(d, 2), win_mask))
        return items

    for depth in range(n_win + 1):
        @pl.when((i == depth) if depth < n_win else (i >= depth))
        def _(depth=depth):
            attend(near_items(depth))

    n_far = jnp.maximum(i - n_win, 0)

    def far_item(stream, k):
        return (stream, tile(ks_ref, k), vst_ref[k], 2, lanes4(tile(sel_ref, k)))

    def far_pair(kk, carry):
        attend([far_item(SEL, 2 * kk), far_item(SEL2, 2 * kk + 1)])
        return carry

    lax.fori_loop(0, n_far // 2, far_pair, 0)

    @pl.when(n_far % 2 == 1)
    def _():
        attend([far_item(SEL, n_far - 1)])

    m_a, m_b = m_ref[SEL], m_ref[SEL2]
    m_ab = jnp.maximum(m_a, m_b)
    m_ab = jnp.where(m_ab == NEG_INF, 0.0, m_ab)
    w_a = jnp.exp2(m_a - m_ab)
    w_b = jnp.exp2(m_b - m_ab)
    l_sel = w_a * l_ref[SEL] + w_b * l_ref[SEL2]
    o_sel = (w_a * acc_ref[SEL] + w_b * acc_ref[SEL2]) * (1.0 / jnp.maximum(l_sel, 1e-30))
    o_win = acc_ref[WIN] * (1.0 / jnp.maximum(l_ref[WIN], 1e-30))
    res = res_ref[...] + gate_row(1) * o_sel + gate_row(2) * o_win
    for j in range(NSA_HPG):
        o_ref[:, j * NSA_HEAD_DIM:(j + 1) * NSA_HEAD_DIM] = res[:, j * TQ:(j + 1) * TQ].T.astype(o_ref.dtype)


def nsa_attention(r1, r2, kcvc, bias_c, bias_d):
    b, s, _ = r1.shape
    n_cmp_rows = s // CMP_STRIDE
    n_sel = s // SEL_BLOCK
    assert n_cmp_rows % LANES == 0 and n_sel <= LANES and n_sel % 8 == 0 and WINDOW % TK == 0 and TQ == TK
    assert 2 * TK - (TQ - 1) >= REL_MAX_DIST
    top_n = min(SEL_TOP_N, n_sel)
    nq = s // TQ
    nk = s // TK
    gw = NSA_HPG * NSA_HEAD_DIM
    cols = NSA_HPG * TQ
    dh = NSA_HEAD_DIM

    n_cmp = n_cmp_rows - CMP_BLOCK // CMP_STRIDE + 1
    c_start = np.arange(n_cmp_rows) * CMP_STRIDE
    s_start = np.arange(LANES) * SEL_BLOCK
    cover_t = ((c_start[None, :] <= s_start[:, None] + SEL_BLOCK - 1) & (c_start[None, :] + CMP_BLOCK - 1 >= s_start[:, None])
               & (np.arange(n_cmp_rows)[None, :] < n_cmp) & (np.arange(LANES)[:, None] < n_sel))
    cover_t = jnp.asarray(cover_t.astype(np.float32))
    expand_t = jnp.asarray(((np.arange(s)[:, None] // SEL_BLOCK) == np.arange(LANES)[None, :]).astype(np.float32), BF16)

    v0 = R1_VS * LANES
    v1 = R1_VW * LANES
    vt = jnp.stack([r1[:, :, v0:v0 + NSA_KV_WIDTH], r1[:, :, v1:v1 + NSA_KV_WIDTH]], axis=1)
    vt = vt.reshape(b, 2, nk, TK, NSA_KV_GROUPS, dh).transpose(0, 1, 4, 2, 5, 3)
    g0 = R2_SMALL * LANES + SMALL_GATE
    gate_t = jnp.swapaxes(r2[:, :, g0:g0 + 3 * NSA_HEADS], 1, 2)

    k_spec = lambda blk: pl.BlockSpec((None, s, LANES), lambda bi, g, i, blk=blk: (bi, 0, blk + g))
    vt_spec = lambda w: pl.BlockSpec((None, None, None, nk, dh, TK), lambda bi, g, i, w=w: (bi, w, g, 0, 0, 0))
    return pl.pallas_call(
        functools.partial(_nsa_attn_kernel, n_sel=n_sel, top_n=top_n),
        grid=(b, NSA_KV_GROUPS, nq),
        in_specs=[
            pl.BlockSpec((None, TQ, gw), lambda bi, g, i: (bi, i, g)),
            k_spec(R1_KS), k_spec(R1_KW), vt_spec(0), vt_spec(1),
            pl.BlockSpec((None, None, 2, n_cmp_rows, dh), lambda bi, g, i: (bi, g, 0, 0, 0)),
            pl.BlockSpec((None, 3 * NSA_HEADS, TQ), lambda bi, g, i: (bi, 0, i)),
            pl.BlockSpec((None, None, n_cmp_rows, cols), lambda bi, g, i: (g, i, 0, 0)),
            pl.BlockSpec((None, 3, TK, cols), lambda bi, g, i: (g, 0, 0, 0)),
            pl.BlockSpec((LANES, n_cmp_rows), lambda bi, g, i: (0, 0)),
            pl.BlockSpec((s, LANES), lambda bi, g, i: (0, 0)),
        ],
        out_specs=pl.BlockSpec((None, TQ, gw), lambda bi, g, i: (bi, i, g)),
        out_shape=jax.ShapeDtypeStruct((b, s, NSA_WIDTH), BF16),
        scratch_shapes=[
            pltpu.VMEM((s, TQ), F32),
            pltpu.VMEM((3, 1, cols), F32),
            pltpu.VMEM((3, 1, cols), F32),
            pltpu.VMEM((3, dh, cols), F32),
            pltpu.VMEM((dh, cols), F32),
        ],
        compiler_params=_cparams("parallel", "parallel", "arbitrary"),
        name="nsa_attention",
    )(r1, r1, r1, vt, vt, kcvc, gate_t, bias_c, bias_d, cover_t, expand_t)


GDN_PAIR = 2 * GDN_CHUNK
GDN_GROUP = 4
GDN_ROWS = 512


def _gdn_prep_kernel(q_ref, k_ref, v_ref, smt_ref, alog_ref, dtb_ref, cwq_ref, cwk_ref, cwv_ref,
                     w_out, qg_out, u_out, qk_out, kdt_out, egl_out,
                     kb_s, kk_s, qq_s, vk_s, kd_s, gcol_s, grow_s, xp_s):
    h = pl.program_id(1)
    s = q_ref.shape[0]
    c = GDN_CHUNK
    pr = GDN_PAIR
    dk = GDN_HEAD_DIM
    nt = (((1,), (1,)), ((), ()))
    tn = (((0,), (0,)), ((), ()))

    xg = smt_ref[0:GDN_HEADS, :] + dtb_ref[...]
    g8 = -jnp.exp(alog_ref[...]) * (jnp.maximum(xg, 0.0) + jnp.log1p(jnp.exp(-jnp.abs(xg))))
    beta8 = jax.nn.sigmoid(smt_ref[GDN_HEADS:2 * GDN_HEADS, :])
    pos = lax.broadcasted_iota(jnp.int32, g8.shape, 1) % c
    gcum8 = g8
    sh = 1
    while sh < c:
        gcum8 = gcum8 + jnp.where(pos >= sh, pltpu.roll(gcum8, sh, axis=1), 0.0)
        sh *= 2
    for p in range(s // pr):
        grow_s[p] = gcum8[:, p * pr:(p + 1) * pr]
    sel_k = lax.broadcasted_iota(jnp.int32, (2 * GDN_HEADS, 2 * dk), 0)
    sel_l = lax.broadcasted_iota(jnp.int32, (2 * GDN_HEADS, 2 * dk), 1)
    pick = jnp.where(((sel_k == h) & (sel_l < dk)) | ((sel_k == GDN_HEADS + h) & (sel_l >= dk)), 1.0, 0.0)
    both = jnp.concatenate([gcum8, beta8], axis=0)
    hi = both.astype(BF16)
    rest = both - hi.astype(F32)
    mid = rest.astype(BF16)
    lo = (rest - mid.astype(F32)).astype(BF16)
    col = lax.dot_general(jnp.concatenate([hi, mid, lo], axis=0), jnp.concatenate([pick] * 3, axis=0).astype(BF16),
                          tn, preferred_element_type=F32)
    gcol = col[:, :dk]
    bcol = col[:, dk:]
    gcol_s[...] = gcol

    hist = 8
    xp_s[0:hist, :] = jnp.zeros((hist, dk), F32)

    def conv_silu(x_ref, w_ref):
        x = x_ref[...]
        w = w_ref[...]
        xp_s[hist:hist + s, :] = x
        y = x * w[CONV_WIDTH - 1:CONV_WIDTH, :]
        for sft in range(1, CONV_WIDTH):
            y = y + xp_s[hist - sft:hist - sft + s, :] * w[CONV_WIDTH - 1 - sft:CONV_WIDTH - sft, :]
        return y * jax.nn.sigmoid(y)

    def l2n(x):
        return x * lax.rsqrt(jnp.sum(x * x, axis=-1, keepdims=True) + NORM_EPS)

    q = l2n(conv_silu(q_ref, cwq_ref)) * (dk ** -0.5)
    k = l2n(conv_silu(k_ref, cwk_ref))
    v = conv_silu(v_ref, cwv_ref)
    eg = jnp.exp(gcol)
    kb = k * bcol
    qg_out[...] = (q * eg).astype(BF16)
    kk_s[...] = k.astype(BF16)
    kb_s[...] = kb.astype(BF16)
    qq_s[...] = q.astype(BF16)
    vk_s[:, :dk] = (v * bcol).astype(BF16)
    vk_s[:, dk:] = (kb * eg).astype(BF16)
    g3 = gcol.reshape(s // c, c, dk)
    g_last = g3[:, c - 1:c, :]
    kd_s[...] = (k * jnp.exp(jnp.broadcast_to(g_last, g3.shape).reshape(s, dk) - gcol)).astype(BF16)
    egl_out[...] = jnp.broadcast_to(jnp.exp(g_last), egl_out.shape)

    r_i = lax.broadcasted_iota(jnp.int32, (pr, pr), 0)
    l_i = lax.broadcasted_iota(jnp.int32, (pr, pr), 1)
    tri = ((r_i // c) == (l_i // c)) & (r_i >= l_i)
    eye = r_i == l_i
    eye_f = jnp.where(eye, 1.0, 0.0)
    eye_b = eye_f.astype(BF16)

    def pair_group(ps):
        rows = [slice(p * pr, (p + 1) * pr) for p in ps]
        n = len(ps)
        x = [lax.dot_general(jnp.concatenate([kb_s[r, :], qq_s[r, :]], axis=0), kk_s[r, :], nt,
                             preferred_element_type=F32) for r in rows]
        decay = []
        for p, r in zip(ps, rows):
            diff = gcol_s[r, :] - grow_s[p, pl.ds(h, 1), :]
            decay.append(jnp.where(tri, jnp.exp(jnp.where(tri, diff, 0.0)), 0.0))
        for a in range(n):
            qk_out[rows[a], :] = (x[a][pr:] * decay[a]).astype(BF16)
            kdt_out[ps[a]] = lax.dot_general(eye_b, kd_s[rows[a], :], nt, preferred_element_type=F32).astype(BF16)
        def split(m):
            hi = m.astype(BF16)
            return hi, (m - hi.astype(F32)).astype(BF16)

        def mm3(lhs, rhs):
            return (jnp.dot(lhs[0], rhs[0], preferred_element_type=F32)
                    + jnp.dot(lhs[1], rhs[0], preferred_element_type=F32)
                    + jnp.dot(lhs[0], rhs[1], preferred_element_type=F32))

        def stack(u, v):
            return (jnp.concatenate([u[0], v[0]], axis=0), jnp.concatenate([u[1], v[1]], axis=0))

        qm = [jnp.where(eye, 0.0, -(x[a][:pr] * decay[a])) for a in range(n)]
        tinv = [eye_f + qm[a] for a in range(n)]
        qs = [split(qm[a]) for a in range(n)]
        qm = [mm3(qs[a], qs[a]) for a in range(n)]
        steps = int(math.log2(c)) - 1
        for it in range(steps):
            qs = [split(qm[a]) for a in range(n)]
            ts = [split(tinv[a]) for a in range(n)]
            if it < steps - 1:
                r = [mm3(stack(qs[a], ts[a]), qs[a]) for a in range(n)]
                qm = [r[a][:pr] for a in range(n)]
                tinv = [tinv[a] + r[a][pr:] for a in range(n)]
            else:
                tinv = [tinv[a] + mm3(ts[a], qs[a]) for a in range(n)]
        for a in range(n):
            uw = jnp.dot(tinv[a].astype(BF16), vk_s[rows[a], :], preferred_element_type=F32)
            u_out[rows[a], :] = uw[:, :dk]
            w_out[rows[a], :] = uw[:, dk:].astype(BF16)

    npair = s // pr
    for p0 in range(0, npair, GDN_GROUP):
        pair_group(list(range(p0, min(p0 + GDN_GROUP, npair))))


def _gdn_scan_kernel(w_ref, qg_ref, u_ref, qk_ref, kdt_ref, egl_ref, z_ref, nw_ref, o_ref, state_ref, os_ref):
    c = GDN_CHUNK
    pr = GDN_PAIR
    dk = GDN_HEAD_DIM
    nh = w_ref.shape[0]
    rb = w_ref.shape[1]

    @pl.when(pl.program_id(1) == 0)
    def _():
        state_ref[...] = jnp.zeros(state_ref.shape, F32)

    zeros = jnp.zeros((c, dk), BF16)

    def pair(p, carry):
        r0 = pl.multiple_of(p * pr, pr)
        for half in range(2):
            rows = pl.ds(r0 + half * c, c)
            state = [state_ref[h] for h in range(nh)]
            r1 = [jnp.dot(jnp.concatenate([w_ref[h, rows, :], qg_ref[h, rows, :]], axis=0), state[h].astype(BF16),
                          preferred_element_type=F32) for h in range(nh)]
            vnb = [(u_ref[h, rows, :] - r1[h][:c]).astype(BF16) for h in range(nh)]
            rhs = [jnp.concatenate([vnb[h], zeros] if half == 0 else [zeros, vnb[h]], axis=0) for h in range(nh)]
            r2 = [jnp.dot(jnp.concatenate([qk_ref[h, rows, :], kdt_ref[h, p]], axis=0), rhs[h],
                          preferred_element_type=F32) for h in range(nh)]
            for h in range(nh):
                os_ref[rows, h * dk:(h + 1) * dk] = r1[h][c:] + r2[h][:c]
                state_ref[h] = state[h] * egl_ref[h, 2 * p + half, 0:1, :] + r2[h][c:]
        return carry

    lax.fori_loop(0, rb // pr, pair, 0)

    for h in range(nh):
        o = os_ref[:, h * dk:(h + 1) * dk]
        o = o * lax.rsqrt(jnp.mean(o * o, axis=-1, keepdims=True) + NORM_EPS) * nw_ref[...]
        z = z_ref[:, h * dk:(h + 1) * dk]
        o_ref[:, h * dk:(h + 1) * dk] = (o * (z * jax.nn.sigmoid(z))).astype(o_ref.dtype)


def gated_deltanet(r2, conv_w, a_log, dt_bias, norm_w):
    b, s, _ = r2.shape
    dk = GDN_HEAD_DIM
    nh = GDN_HEADS
    npair = s // GDN_PAIR
    nchunk = s // GDN_CHUNK
    rb = GDN_ROWS
    assert s % GDN_PAIR == 0 and s % rb == 0 and (R2_Z * LANES) % GDN_WIDTH == 0
    a0 = R2_SMALL * LANES + SMALL_A
    smt = jnp.swapaxes(r2[:, :, a0:a0 + 2 * nh], 1, 2)
    col = lambda blk: pl.BlockSpec((None, s, dk), lambda bi, h, blk=blk: (bi, 0, blk + h))
    cw = lambda blk: pl.BlockSpec((CONV_WIDTH, dk), lambda bi, h, blk=blk: (0, blk + h))
    vec = pl.BlockSpec((nh, 1), lambda bi, h: (0, 0))
    per_head = lambda *shape: pl.BlockSpec((None, None) + shape, lambda bi, h: (bi, h) + (0,) * len(shape))
    w, qg, u, qk, kdt, egl = pl.pallas_call(
        _gdn_prep_kernel,
        grid=(b, nh),
        in_specs=[col(R2_GQ), col(R2_GK), col(R2_GV),
                  pl.BlockSpec((None, 2 * nh, s), lambda bi, h: (bi, 0, 0)), vec, vec,
                  cw(0), cw(nh), cw(2 * nh)],
        out_specs=[per_head(s, dk), per_head(s, dk), per_head(s, dk), per_head(s, dk),
                   per_head(npair, GDN_PAIR, GDN_PAIR), per_head(nchunk, 8, dk)],
        out_shape=[jax.ShapeDtypeStruct((b, nh, s, dk), BF16), jax.ShapeDtypeStruct((b, nh, s, dk), BF16),
                   jax.ShapeDtypeStruct((b, nh, s, dk), F32), jax.ShapeDtypeStruct((b, nh, s, dk), BF16),
                   jax.ShapeDtypeStruct((b, nh, npair, GDN_PAIR, GDN_PAIR), BF16),
                   jax.ShapeDtypeStruct((b, nh, nchunk, 8, dk), F32)],
        scratch_shapes=[pltpu.VMEM((s, dk), BF16), pltpu.VMEM((s, dk), BF16), pltpu.VMEM((s, dk), BF16),
                        pltpu.VMEM((s, 2 * dk), BF16), pltpu.VMEM((s, dk), BF16), pltpu.VMEM((s, dk), F32),
                        pltpu.VMEM((npair, nh, GDN_PAIR), F32), pltpu.VMEM((s + 8, dk), F32)],
        compiler_params=_cparams("parallel", "parallel"),
        name="gdn_prep",
    )(r2, r2, r2, smt, a_log.reshape(nh, 1), dt_bias.reshape(nh, 1), conv_w, conv_w, conv_w)

    seq = lambda *shape: pl.BlockSpec((None, nh) + shape, lambda bi, j: (bi, 0, j) + (0,) * (len(shape) - 1))
    return pl.pallas_call(
        _gdn_scan_kernel,
        grid=(b, s // rb),
        in_specs=[seq(rb, dk), seq(rb, dk), seq(rb, dk), seq(rb, dk),
                  seq(rb // GDN_PAIR, GDN_PAIR, GDN_PAIR), seq(rb // GDN_CHUNK, 8, dk),
                  pl.BlockSpec((None, rb, GDN_WIDTH), lambda bi, j: (bi, j, R2_Z * LANES // GDN_WIDTH)),
                  pl.BlockSpec((1, dk), lambda bi, j: (0, 0))],
        out_specs=pl.BlockSpec((None, rb, GDN_WIDTH), lambda bi, j: (bi, j, 0)),
        out_shape=jax.ShapeDtypeStruct((b, s, GDN_WIDTH), BF16),
        scratch_shapes=[pltpu.VMEM((nh, dk, dk), F32), pltpu.VMEM((rb, GDN_WIDTH), F32)],
        compiler_params=_cparams("parallel", "arbitrary"),
        name="gdn_scan",
    )(w, qg, u, qk, kdt, egl, r2, norm_w.reshape(1, dk))


def _split_w_in(w_in):
    o_q = 0
    o_kv = NSA_WIDTH
    o_gate = o_kv + 6 * NSA_KV_WIDTH
    o_qkv = o_gate + 3 * NSA_HEADS
    o_z = o_qkv + 3 * GDN_WIDTH
    o_a = o_z + GDN_WIDTH
    o_b = o_a + GDN_HEADS
    o_m = o_b + GDN_HEADS
    kvw = NSA_KV_WIDTH
    w1 = jnp.concatenate([w_in[:, o_q:o_kv], w_in[:, o_kv + 2 * kvw:o_kv + 6 * kvw], w_in[:, o_m:]], axis=1)
    small = jnp.concatenate([w_in[:, o_gate:o_qkv], w_in[:, o_a:o_m]], axis=1)
    used = (R2_SMALL * LANES) + small.shape[1]
    w2 = jnp.concatenate([w_in[:, o_z:o_a], w_in[:, o_kv:o_kv + 2 * kvw], w_in[:, o_qkv:o_z], small,
                          jnp.zeros((w_in.shape[0], R2_COLS - used), w_in.dtype)], axis=1)
    return jnp.concatenate([w1, w2], axis=1).astype(BF16)


def _r1_col_scale():
    c = np.ones((1, R1_COLS), np.float32)
    c[:, :NSA_WIDTH] = NSA_HEAD_DIM ** -0.5 * LOG2E
    return jnp.asarray(c)


def _layer(x, bias_c, bias_d, ln1_w, w_in, pe_k, pe_v, w1_k, w2_k, w1_v, w2_v, conv_w, a_log, dt_bias,
           gdn_norm_w, w_pa, w_pb, w_o, ln2_w, w_up, w_down, final_norm_w, b, s):
    t = b * s
    r1, r2 = input_projection(x, ln1_w, _split_w_in(w_in), _r1_col_scale())
    r1 = r1.reshape(b, s, R1_COLS)
    r2 = r2.reshape(b, s, R2_COLS)
    kcvc = nsa_compress(r2, jnp.stack([pe_k, pe_v]), jnp.stack([w1_k, w1_v]).astype(BF16),
                        w2_k.astype(BF16), w2_v.T.astype(BF16))
    o_a = nsa_attention(r1, r2, kcvc, bias_c, bias_d).reshape(t, NSA_WIDTH)
    o_b = gated_deltanet(r2, conv_w, a_log, dt_bias, gdn_norm_w).reshape(t, GDN_WIDTH)
    m = merge_branches(o_a, o_b, w_pa.astype(BF16), w_pb.astype(BF16), r1.reshape(t, R1_COLS))
    x, h2 = matmul(m, w_o.astype(BF16), F32, tm=512, tn=D_MODEL, tk=D_MODEL, epilogue="residual_norm", res=x,
                   norm_w=ln2_w, norm_dtype=BF16)
    u = matmul(h2, w_up.astype(BF16), BF16, tm=1024, tn=1024, tk=D_MODEL, epilogue="relu2")
    x = matmul(u, w_down.astype(BF16), F32, tm=1024, tn=256, tk=D_FF, epilogue="residual", res=x)
    if final_norm_w is None:
        return x
    return rmsnorm(x, final_norm_w, F32)


def kernel(x, rel_table, ln1_w, w_in, cmp_pe_k, cmp_pe_v, cmp_w1_k, cmp_w2_k, cmp_w1_v, cmp_w2_v, conv_w, a_log,
           dt_bias, gdn_norm_w, w_pa, w_pb, w_o, ln2_w, w_up, w_down, ln_f_w):
    b, s, d = x.shape
    depth = w_in.shape[0]
    bias_c, bias_d = bias_tables(rel_table, s)
    xt = x.reshape(b * s, d)
    for l in range(depth):
        xt = _layer(xt, bias_c, bias_d, ln1_w[l], w_in[l], cmp_pe_k[l], cmp_pe_v[l], cmp_w1_k[l], cmp_w2_k[l],
                    cmp_w1_v[l], cmp_w2_v[l], conv_w[l], a_log[l], dt_bias[l], gdn_norm_w[l], w_pa[l], w_pb[l],
                    w_o[l], ln2_w[l], w_up[l], w_down[l], ln_f_w if l == depth - 1 else None, b, s)
    return xt.reshape(b, s, d)
```

```python
import functools
import math

import numpy as np
import jax
import jax.numpy as jnp
from jax import lax
from jax.experimental import pallas as pl
from jax.experimental.pallas import tpu as pltpu

F32 = jnp.float32
BF16 = jnp.bfloat16
HIGHEST = lax.Precision.HIGHEST

D_MODEL = 2048
DEPTH = 2
NSA_HEADS = 8
NSA_KV_GROUPS = 2
NSA_HPG = NSA_HEADS // NSA_KV_GROUPS
NSA_HEAD_DIM = 128
NSA_WIDTH = NSA_HEADS * NSA_HEAD_DIM
NSA_KV_WIDTH = NSA_KV_GROUPS * NSA_HEAD_DIM
CMP_BLOCK = 32
CMP_STRIDE = 16
CMP_HIDDEN = 256
SEL_BLOCK = 64
SEL_TOP_N = 16
SEL_FORCE = 1000.0
WINDOW = 512
GDN_HEADS = 8
GDN_HEAD_DIM = 128
GDN_WIDTH = GDN_HEADS * GDN_HEAD_DIM
CONV_WIDTH = 4
GDN_CHUNK = 64
REL_BUCKETS = 32
REL_MAX_DIST = 128
D_FF = 4 * D_MODEL
NORM_EPS = 1e-6

LANES = 128
VMEM_LIMIT = 56 * 1024 * 1024
TQ = 256
TK = 256
NEG_INF = float("-inf")
LOG2E = math.log2(math.e)

R1_KS, R1_VS, R1_KW, R1_VW, R1_MA, R1_MB = 8, 10, 12, 14, 16, 32
R1_COLS = 48 * LANES
R2_Z, R2_KC, R2_VC, R2_GQ, R2_GK, R2_GV, R2_SMALL = 0, 8, 10, 12, 20, 28, 36
R2_COLS = 40 * LANES
SMALL_GATE, SMALL_A, SMALL_B = 0, 24, 32


def _t5_thresholds():
    exact = REL_BUCKETS // 2
    n = np.arange(exact, REL_MAX_DIST * 2, dtype=np.int64)
    nf = n.astype(np.float32)
    large = exact + (np.log(nf / np.float32(exact)) / np.float32(math.log(REL_MAX_DIST / exact))
                     * np.float32(REL_BUCKETS - exact)).astype(np.int32)
    large = np.minimum(large, REL_BUCKETS - 1)
    thr = []
    for b in range(exact + 1, REL_BUCKETS):
        thr.append(int(n[np.argmax(large >= b)]))
    return tuple(thr)


T5_THRESHOLDS = _t5_thresholds()


def _cparams(*sem):
    return pltpu.CompilerParams(dimension_semantics=sem, vmem_limit_bytes=VMEM_LIMIT)


def _rmsnorm_kernel(x_ref, w_ref, o_ref):
    x = x_ref[...]
    y = x * lax.rsqrt(jnp.mean(x * x, axis=-1, keepdims=True) + NORM_EPS)
    o_ref[...] = (y * w_ref[...]).astype(o_ref.dtype)


def rmsnorm(x, w, out_dtype, tm=512):
    t, d = x.shape
    return pl.pallas_call(
        _rmsnorm_kernel,
        grid=(t // tm,),
        in_specs=[pl.BlockSpec((tm, d), lambda i: (i, 0)), pl.BlockSpec((1, d), lambda i: (0, 0))],
        out_specs=pl.BlockSpec((tm, d), lambda i: (i, 0)),
        out_shape=jax.ShapeDtypeStruct((t, d), out_dtype),
        compiler_params=_cparams("parallel"),
        name="rmsnorm",
    )(x, w.reshape(1, d))


def _mm_kernel(*refs, nk, epilogue):
    a_ref, w_ref = refs[0], refs[1]
    pos = 2
    res_ref = norm_ref = None
    if epilogue in ("residual", "residual_norm"):
        res_ref = refs[pos]
        pos += 1
    if epilogue == "residual_norm":
        norm_ref = refs[pos]
        pos += 1
    o_ref = refs[pos]
    pos += 1
    n_ref = None
    if epilogue == "residual_norm":
        n_ref = refs[pos]
        pos += 1
    acc_ref = refs[pos] if nk > 1 else None

    def finish(acc):
        if epilogue == "relu2":
            acc = jnp.square(jnp.maximum(acc, 0.0))
        elif epilogue == "residual":
            acc = res_ref[...] + acc
        elif epilogue == "residual_norm":
            acc = res_ref[...] + acc
            normed = acc * lax.rsqrt(jnp.mean(acc * acc, axis=-1, keepdims=True) + NORM_EPS) * norm_ref[...]
            n_ref[...] = normed.astype(n_ref.dtype)
        o_ref[...] = acc.astype(o_ref.dtype)

    part = jnp.dot(a_ref[...], w_ref[...], preferred_element_type=F32)
    if nk == 1:
        finish(part)
    else:
        k = pl.program_id(2)

        @pl.when(k == 0)
        def _():
            acc_ref[...] = part

        @pl.when(k > 0)
        def _():
            acc_ref[...] += part

        @pl.when(k == nk - 1)
        def _():
            finish(acc_ref[...])


def _column_tiles(w, tn):
    k, n = w.shape
    return w.reshape(k, n // tn, tn).transpose(1, 0, 2)


def matmul(a, w, out_dtype, *, tm, tn, tk, epilogue="none", res=None, norm_w=None, norm_dtype=None):
    m, kdim = a.shape
    n = w.shape[1]
    nk = kdim // tk
    if nk == 1 and tn < n:
        w = _column_tiles(w, tn)
        w_spec = pl.BlockSpec((None, tk, tn), lambda i, j, k: (j, 0, 0))
    else:
        w_spec = pl.BlockSpec((tk, tn), lambda i, j, k: (k, j))
    in_specs = [pl.BlockSpec((tm, tk), lambda i, j, k: (i, k)), w_spec]
    args = [a, w]
    if epilogue in ("residual", "residual_norm"):
        in_specs.append(pl.BlockSpec((tm, tn), lambda i, j, k: (i, j)))
        args.append(res)
    if epilogue == "residual_norm":
        assert tn == n
        in_specs.append(pl.BlockSpec((1, tn), lambda i, j, k: (0, 0)))
        args.append(norm_w.reshape(1, n))
    scratch = [pltpu.VMEM((tm, tn), F32)] if nk > 1 else []
    out_spec = pl.BlockSpec((tm, tn), lambda i, j, k: (i, j))
    out_specs, out_shape = out_spec, jax.ShapeDtypeStruct((m, n), out_dtype)
    if epilogue == "residual_norm":
        out_specs = [out_spec, out_spec]
        out_shape = [out_shape, jax.ShapeDtypeStruct((m, n), norm_dtype)]
    return pl.pallas_call(
        functools.partial(_mm_kernel, nk=nk, epilogue=epilogue),
        grid=(m // tm, n // tn, nk),
        in_specs=in_specs,
        out_specs=out_specs,
        out_shape=out_shape,
        scratch_shapes=scratch,
        compiler_params=_cparams("parallel", "parallel", "arbitrary"),
        name="matmul_" + epilogue,
    )(*args)


def _row_tile_copy(src_hbm, buf, sem, tile, slot):
    tm = buf.shape[1]
    return pltpu.make_async_copy(src_hbm.at[pl.ds(tile * tm, tm), :], buf.at[slot], sem.at[slot])


def _prefetch_row_tile(src_hbm, buf, sem):
    i = pl.program_id(0)
    slot = i % 2

    @pl.when(pl.program_id(1) == 0)
    def _():
        @pl.when(i == 0)
        def _():
            _row_tile_copy(src_hbm, buf, sem, 0, 0).start()

        _row_tile_copy(src_hbm, buf, sem, i, slot).wait()

        @pl.when(i + 1 < pl.num_programs(0))
        def _():
            _row_tile_copy(src_hbm, buf, sem, i + 1, 1 - slot).start()

    return slot


def _mm_rows_kernel(*refs, epilogue):
    a_hbm, w_ref = refs[0], refs[1]
    res_ref = refs[2] if epilogue == "residual" else None
    o_ref, a_buf, sem = refs[-3], refs[-2], refs[-1]
    slot = _prefetch_row_tile(a_hbm, a_buf, sem)
    acc = jnp.dot(a_buf[slot], w_ref[...], preferred_element_type=F32)
    if epilogue == "relu2":
        acc = jnp.square(jnp.maximum(acc, 0.0))
    elif epilogue == "residual":
        acc = res_ref[...] + acc
    o_ref[...] = acc.astype(o_ref.dtype)


def matmul_rows(a, w, out_dtype, *, tm, tn, epilogue, res=None):
    m, kdim = a.shape
    n = w.shape[1]
    in_specs = [pl.BlockSpec(memory_space=pl.ANY), pl.BlockSpec((None, kdim, tn), lambda i, j: (j, 0, 0))]
    args = [a, _column_tiles(w, tn)]
    if epilogue == "residual":
        in_specs.append(pl.BlockSpec((tm, tn), lambda i, j: (i, j)))
        args.append(res)
    return pl.pallas_call(
        functools.partial(_mm_rows_kernel, epilogue=epilogue),
        grid=(m // tm, n // tn),
        in_specs=in_specs,
        out_specs=pl.BlockSpec((tm, tn), lambda i, j: (i, j)),
        out_shape=jax.ShapeDtypeStruct((m, n), out_dtype),
        scratch_shapes=[pltpu.VMEM((2, tm, kdim), a.dtype), pltpu.SemaphoreType.DMA((2,))],
        compiler_params=_cparams("arbitrary", "arbitrary"),
        name="matmul_rows_" + epilogue,
    )(*args)


def _inproj_kernel(x_hbm, lnw_ref, w_ref, cs_ref, r1_ref, r2_ref, h_ref, x_buf, sem, *, n1):
    j = pl.program_id(1)
    slot = _prefetch_row_tile(x_hbm, x_buf, sem)

    @pl.when(j == 0)
    def _():
        x = x_buf[slot]
        y = x * lax.rsqrt(jnp.mean(x * x, axis=-1, keepdims=True) + NORM_EPS)
        h_ref[...] = (y * lnw_ref[...]).astype(BF16)

    acc = jnp.dot(h_ref[...], w_ref[...], preferred_element_type=F32)

    @pl.when(j < n1)
    def _():
        r1_ref[...] = (acc * cs_ref[...]).astype(r1_ref.dtype)

    @pl.when(j >= n1)
    def _():
        r2_ref[...] = acc


def input_projection(x, ln_w, w_all, col_scale, *, tm=1024, tn=1024):
    t, d = x.shape
    n1 = R1_COLS // tn
    n2 = R2_COLS // tn
    assert w_all.shape == (d, R1_COLS + R2_COLS) and R1_COLS % tn == 0 and R2_COLS % tn == 0
    return pl.pallas_call(
        functools.partial(_inproj_kernel, n1=n1),
        grid=(t // tm, n1 + n2),
        in_specs=[pl.BlockSpec(memory_space=pl.ANY),
                  pl.BlockSpec((1, d), lambda i, j: (0, 0)),
                  pl.BlockSpec((None, d, tn), lambda i, j: (j, 0, 0)),
                  pl.BlockSpec((1, tn), lambda i, j: (0, jnp.minimum(j, n1 - 1)))],
        out_specs=[pl.BlockSpec((tm, tn), lambda i, j: (i, jnp.minimum(j, n1 - 1))),
                   pl.BlockSpec((tm, tn), lambda i, j: (i, jnp.maximum(j - n1, 0)))],
        out_shape=[jax.ShapeDtypeStruct((t, R1_COLS), BF16), jax.ShapeDtypeStruct((t, R2_COLS), F32)],
        scratch_shapes=[pltpu.VMEM((tm, d), BF16), pltpu.VMEM((2, tm, d), x.dtype), pltpu.SemaphoreType.DMA((2,))],
        compiler_params=_cparams("arbitrary", "arbitrary"),
        name="input_projection",
    )(x, ln_w.reshape(1, d), _column_tiles(w_all, tn), col_scale)


def _merge_kernel(oa_ref, ob_ref, wa_ref, wb_ref, ma_ref, mb_ref, o_ref):
    pa = jnp.dot(oa_ref[...], wa_ref[...], preferred_element_type=F32)
    pb = jnp.dot(ob_ref[...], wb_ref[...], preferred_element_type=F32)
    ga = jax.nn.sigmoid(ma_ref[...].astype(F32))
    gb = jax.nn.sigmoid(mb_ref[...].astype(F32))
    o_ref[...] = (ga * pa + gb * pb).astype(o_ref.dtype)


def merge_branches(o_a, o_b, w_pa, w_pb, r1, *, tm=1024, tn=512):
    t, ka = o_a.shape
    kb = o_b.shape[1]
    n = w_pa.shape[1]
    ma0 = R1_MA * LANES // tn
    mb0 = R1_MB * LANES // tn
    return pl.pallas_call(
        _merge_kernel,
        grid=(t // tm, n // tn),
        in_specs=[
            pl.BlockSpec((tm, ka), lambda i, j: (i, 0)),
            pl.BlockSpec((tm, kb), lambda i, j: (i, 0)),
            pl.BlockSpec((ka, tn), lambda i, j: (0, j)),
            pl.BlockSpec((kb, tn), lambda i, j: (0, j)),
            pl.BlockSpec((tm, tn), lambda i, j: (i, ma0 + j)),
            pl.BlockSpec((tm, tn), lambda i, j: (i, mb0 + j)),
        ],
        out_specs=pl.BlockSpec((tm, tn), lambda i, j: (i, j)),
        out_shape=jax.ShapeDtypeStruct((t, n), BF16),
        compiler_params=_cparams("parallel", "parallel"),
        name="merge_branches",
    )(o_a, o_b, w_pa, w_pb, r1, r1)


def _t5_lookup(dist, tab_ref, h):
    n = jnp.maximum(dist, 0)
    exact = REL_BUCKETS // 2
    large = jnp.full(n.shape, exact, jnp.int32)
    for thr in T5_THRESHOLDS:
        large = large + (n >= thr).astype(jnp.int32)
    bucket = jnp.where(n < exact, n, large)
    out = jnp.zeros(n.shape, F32)
    for b in range(REL_BUCKETS):
        out = jnp.where(bucket == b, tab_ref[b, h] * LOG2E, out)
    return out


def _bias_kernel(tab_ref, bc_ref, bd_ref):
    g = pl.program_id(0)
    i = pl.program_id(1)
    ncr = bc_ref.shape[0]
    blk = lax.broadcasted_iota(jnp.int32, (ncr, TQ), 0)
    qry_c = lax.broadcasted_iota(jnp.int32, (ncr, TQ), 1)
    for j in range(NSA_HPG):
        h = g * NSA_HPG + j
        bc_ref[:, j * TQ:(j + 1) * TQ] = _t5_lookup(i * TQ + qry_c - (blk * CMP_STRIDE + CMP_BLOCK - 1), tab_ref, h)

    @pl.when(i == 0)
    def _():
        key = lax.broadcasted_iota(jnp.int32, (TK, TQ), 0)
        qry = lax.broadcasted_iota(jnp.int32, (TK, TQ), 1)
        for v in range(3):
            for j in range(NSA_HPG):
                bd_ref[v, :, j * TQ:(j + 1) * TQ] = _t5_lookup(v * TK + qry - key, tab_ref, g * NSA_HPG + j)


def bias_tables(rel_table, s):
    nq = s // TQ
    cols = NSA_HPG * TQ
    return pl.pallas_call(
        _bias_kernel,
        grid=(NSA_KV_GROUPS, nq),
        in_specs=[pl.BlockSpec(memory_space=pltpu.SMEM)],
        out_specs=[pl.BlockSpec((None, None, s // CMP_STRIDE, cols), lambda g, i: (g, i, 0, 0)),
                   pl.BlockSpec((None, 3, TK, cols), lambda g, i: (g, 0, 0, 0))],
        out_shape=[jax.ShapeDtypeStruct((NSA_KV_GROUPS, nq, s // CMP_STRIDE, cols), F32),
                   jax.ShapeDtypeStruct((NSA_KV_GROUPS, 3, TK, cols), F32)],
        compiler_params=_cparams("arbitrary", "arbitrary"),
        name="t5_bias_tables",
    )(rel_table)


def _gelu_tanh(x):
    return 0.5 * x * (1.0 + jnp.tanh(math.sqrt(2.0 / math.pi) * (x + 0.044715 * (x * x * x))))


def _compress_kernel(x_ref, pe_ref, w1_ref, w2k_ref, w2vt_ref, o_ref, *, nblk):
    pe = pe_ref[...]
    acc_a = jnp.zeros((nblk, CMP_HIDDEN), F32)
    acc_b = jnp.zeros((nblk, CMP_HIDDEN), F32)
    for r in range(CMP_STRIDE):
        xr = x_ref[pl.ds(r, nblk, stride=CMP_STRIDE), :]
        xa = (xr + pe[r:r + 1, :]).astype(BF16)
        xb = (xr + pe[CMP_STRIDE + r:CMP_STRIDE + r + 1, :]).astype(BF16)
        acc_a = acc_a + jnp.dot(xa, w1_ref[r * NSA_HEAD_DIM:(r + 1) * NSA_HEAD_DIM, :], preferred_element_type=F32)
        acc_b = acc_b + jnp.dot(xb, w1_ref[(CMP_STRIDE + r) * NSA_HEAD_DIM:(CMP_STRIDE + r + 1) * NSA_HEAD_DIM, :],
                                preferred_element_type=F32)
    hidden = acc_a + jnp.concatenate([acc_b[1:], acc_b[:1]], axis=0)
    act = _gelu_tanh(hidden).astype(BF16)

    @pl.when(pl.program_id(2) == 0)
    def _():
        o_ref[...] = jnp.dot(act, w2k_ref[...], preferred_element_type=F32).astype(o_ref.dtype)

    @pl.when(pl.program_id(2) == 1)
    def _():
        o_ref[...] = lax.dot_general(w2vt_ref[...], act, (((1,), (1,)), ((), ())),
                                     preferred_element_type=F32).astype(o_ref.dtype)


def nsa_compress(r2, pe, w1, w2k, w2vt):
    b, s, _ = r2.shape
    nblk = s // CMP_STRIDE
    assert CMP_BLOCK == 2 * CMP_STRIDE and nblk == NSA_HEAD_DIM
    return pl.pallas_call(
        functools.partial(_compress_kernel, nblk=nblk),
        grid=(b, NSA_KV_GROUPS, 2),
        in_specs=[
            pl.BlockSpec((None, s, LANES), lambda bi, g, c: (bi, 0, R2_KC + 2 * c + g)),
            pl.BlockSpec((None, CMP_BLOCK, NSA_HEAD_DIM), lambda bi, g, c: (c, 0, 0)),
            pl.BlockSpec((None, CMP_BLOCK * NSA_HEAD_DIM, CMP_HIDDEN), lambda bi, g, c: (c, 0, 0)),
            pl.BlockSpec((CMP_HIDDEN, NSA_HEAD_DIM), lambda bi, g, c: (0, 0)),
            pl.BlockSpec((NSA_HEAD_DIM, CMP_HIDDEN), lambda bi, g, c: (0, 0)),
        ],
        out_specs=pl.BlockSpec((None, None, None, nblk, NSA_HEAD_DIM), lambda bi, g, c: (bi, g, c, 0, 0)),
        out_shape=jax.ShapeDtypeStruct((b, NSA_KV_GROUPS, 2, nblk, NSA_HEAD_DIM), BF16),
        compiler_params=_cparams("parallel", "parallel", "arbitrary"),
        name="nsa_compress",
    )(r2, pe, w1, w2k, w2vt)


def _nsa_attn_kernel(q_ref, ks_ref, kw_ref, vst_ref, vwt_ref, kcvc_ref, gate_ref, biasc_ref, biasd_ref,
                     covert_ref, expandt_ref, o_ref, sel_ref, m_ref, l_ref, acc_ref, res_ref, *, n_sel, top_n):
    g = pl.program_id(1)
    i = pl.program_id(2)
    cols = NSA_HPG * TQ
    nt = (((1,), (1,)), ((), ()))

    qf = q_ref[...]
    qs = jnp.concatenate([qf[:, j * NSA_HEAD_DIM:(j + 1) * NSA_HEAD_DIM] for j in range(NSA_HPG)], axis=0)
    key = lax.broadcasted_iota(jnp.int32, (TK, cols), 0)
    t_loc = lax.broadcasted_iota(jnp.int32, (TK, cols), 1) % TQ

    def gate_row(c):
        return jnp.concatenate(
            [jax.nn.sigmoid(gate_ref[pl.ds((g * NSA_HPG + j) * 3 + c, 1), :]) for j in range(NSA_HPG)], axis=1)

    def lanes4(x):
        return jnp.concatenate([x] * NSA_HPG, axis=1)

    s = lax.dot_general(kcvc_ref[0], qs, nt, preferred_element_type=F32) + biasc_ref[...]
    cmp_end = lax.broadcasted_iota(jnp.int32, s.shape, 0) * CMP_STRIDE + (CMP_BLOCK - 1)
    s = jnp.where(cmp_end <= i * TQ + lax.broadcasted_iota(jnp.int32, s.shape, 1) % TQ, s, NEG_INF)
    mx = jnp.max(s, axis=0, keepdims=True)
    mx = jnp.where(mx == NEG_INF, 0.0, mx)
    e = jnp.exp2(s - mx)
    p = e / jnp.maximum(jnp.sum(e, axis=0, keepdims=True), 1e-30)
    o_c = jnp.dot(kcvc_ref[1], p.astype(BF16), preferred_element_type=F32)
    res_ref[...] = gate_row(0) * o_c

    psum = p[:, 0:TQ]
    for j in range(1, NSA_HPG):
        psum = psum + p[:, j * TQ:(j + 1) * TQ]
    imp = jnp.dot(covert_ref[...], psum, preferred_element_type=F32, precision=HIGHEST)[:n_sel]
    blk = lax.broadcasted_iota(jnp.int32, (n_sel, TQ), 0)
    cur = (i * TQ + lax.broadcasted_iota(jnp.int32, (n_sel, TQ), 1)) // SEL_BLOCK
    causal_blk = blk <= cur
    forced = (blk == 0) | (blk == cur) | (blk == cur - 1)
    score = jnp.where(causal_blk, imp + jnp.where(forced, SEL_FORCE, 0.0), NEG_INF)
    rank = jnp.zeros((n_sel, TQ), jnp.int32)
    for mp in range(n_sel):
        other = score[mp:mp + 1, :]
        beats = (other > score) | ((other == score) & (blk > mp))
        rank = rank + beats.astype(jnp.int32)
    sel = jnp.where((rank < top_n) & causal_blk, 1.0, 0.0)
    sel = jnp.concatenate([sel, jnp.zeros((LANES - n_sel, TQ), F32)], axis=0).astype(BF16)
    sel_ref[...] = jnp.where(jnp.dot(expandt_ref[...], sel, preferred_element_type=F32) > 0.5, 0.0, NEG_INF)

    def reset():
        m_ref[...] = jnp.full(m_ref.shape, NEG_INF, F32)
        l_ref[...] = jnp.zeros(l_ref.shape, F32)
        acc_ref[...] = jnp.zeros(acc_ref.shape, F32)

    far_bias = biasd_ref[2, 0:1, :]

    def attend(items):
        def scores(item):
            return lax.dot_general(item[1], qs, nt, preferred_element_type=F32)

        def softmax_step(item, s):
            st, _, _, variant, mask_add = item
            if variant < 2:
                s = s + biasd_ref[variant]
            if mask_add is not None:
                s = s + mask_add
            smax = jnp.max(s, axis=0, keepdims=True)
            if variant == 2:
                smax = smax + far_bias
            m_prev = m_ref[st]
            m_new = jnp.maximum(m_prev, smax)
            m_safe = jnp.where(m_new == NEG_INF, 0.0, m_new)
            alpha = jnp.exp2(m_prev - m_safe)
            p = jnp.exp2(s + ((far_bias - m_safe) if variant == 2 else -m_safe))
            l_ref[st] = alpha * l_ref[st] + jnp.sum(p, axis=0, keepdims=True)
            m_ref[st] = m_new
            return alpha, p.astype(BF16)

        def value_step(item, alpha, pb):
            st, vt = item[0], item[2]
            acc_ref[st] = alpha * acc_ref[st] + jnp.dot(vt, pb, preferred_element_type=F32)

        s_next = scores(items[0])
        pending = None
        for n, item in enumerate(items):
            s_cur = s_next
            if n + 1 < len(items):
                s_next = scores(items[n + 1])
            alpha, pb = softmax_step(item, s_cur)
            if pending is not None:
                value_step(*pending)
            pending = (item, alpha, pb)
        value_step(*pending)

    def tile(ref, k):
        return ref[pl.ds(pl.multiple_of(k * TK, TK), TK), :]

    causal_add = jnp.where(key <= t_loc, 0.0, NEG_INF)
    edge_add = jnp.where(key > t_loc, 0.0, NEG_INF)
    n_win = WINDOW // TK
    SEL, WIN, SEL2 = 0, 1, 2
    reset()

    def near_items(depth):
        items = []
        for d in range(depth + 1):
            k = i - d
            sel_mask = lanes4(tile(sel_ref, k))
            items.append((SEL, tile(ks_ref, k), vst_ref[k], min(d, 2), sel_mask + causal_add if d == 0 else sel_mask))
            win_mask = causal_add if d == 0 else (edge_add if d == n_win else None)
            items.append((WIN, tile(kw_ref, k), vwt_ref[k], min(d, 2), win_mask))
        return items

    for depth in range(n_win + 1):
        @pl.when((i == depth) if depth < n_win else (i >= depth))
        def _(depth=depth):
            attend(near_items(depth))

    n_far = jnp.maximum(i - n_win, 0)

    def far_item(stream, k):
        return (stream, tile(ks_ref, k), vst_ref[k], 2, lanes4(tile(sel_ref, k)))

    def far_pair(kk, carry):
        attend([far_item(SEL, 2 * kk), far_item(SEL2, 2 * kk + 1)])
        return carry

    lax.fori_loop(0, n_far // 2, far_pair, 0)

    @pl.when(n_far % 2 == 1)
    def _():
        attend([far_item(SEL, n_far - 1)])

    m_a, m_b = m_ref[SEL], m_ref[SEL2]
    m_ab = jnp.maximum(m_a, m_b)
    m_ab = jnp.where(m_ab == NEG_INF, 0.0, m_ab)
    w_a = jnp.exp2(m_a - m_ab)
    w_b = jnp.exp2(m_b - m_ab)
    l_sel = w_a * l_ref[SEL] + w_b * l_ref[SEL2]
    o_sel = (w_a * acc_ref[SEL] + w_b * acc_ref[SEL2]) * (1.0 / jnp.maximum(l_sel, 1e-30))
    o_win = acc_ref[WIN] * (1.0 / jnp.maximum(l_ref[WIN], 1e-30))
    res = res_ref[...] + gate_row(1) * o_sel + gate_row(2) * o_win
    for j in range(NSA_HPG):
        o_ref[:, j * NSA_HEAD_DIM:(j + 1) * NSA_HEAD_DIM] = res[:, j * TQ:(j + 1) * TQ].T.astype(o_ref.dtype)


def nsa_attention(r1, r2, kcvc, bias_c, bias_d):
    b, s, _ = r1.shape
    n_cmp_rows = s // CMP_STRIDE
    n_sel = s // SEL_BLOCK
    assert n_cmp_rows % LANES == 0 and n_sel <= LANES and n_sel % 8 == 0 and WINDOW % TK == 0 and TQ == TK
    assert 2 * TK - (TQ - 1) >= REL_MAX_DIST
    top_n = min(SEL_TOP_N, n_sel)
    nq = s // TQ
    nk = s // TK
    gw = NSA_HPG * NSA_HEAD_DIM
    cols = NSA_HPG * TQ
    dh = NSA_HEAD_DIM

    n_cmp = n_cmp_rows - CMP_BLOCK // CMP_STRIDE + 1
    c_start = np.arange(n_cmp_rows) * CMP_STRIDE
    s_start = np.arange(LANES) * SEL_BLOCK
    cover_t = ((c_start[None, :] <= s_start[:, None] + SEL_BLOCK - 1) & (c_start[None, :] + CMP_BLOCK - 1 >= s_start[:, None])
               & (np.arange(n_cmp_rows)[None, :] < n_cmp) & (np.arange(LANES)[:, None] < n_sel))
    cover_t = jnp.asarray(cover_t.astype(np.float32))
    expand_t = jnp.asarray(((np.arange(s)[:, None] // SEL_BLOCK) == np.arange(LANES)[None, :]).astype(np.float32), BF16)

    v0 = R1_VS * LANES
    v1 = R1_VW * LANES
    vt = jnp.stack([r1[:, :, v0:v0 + NSA_KV_WIDTH], r1[:, :, v1:v1 + NSA_KV_WIDTH]], axis=1)
    vt = vt.reshape(b, 2, nk, TK, NSA_KV_GROUPS, dh).transpose(0, 1, 4, 2, 5, 3)
    g0 = R2_SMALL * LANES + SMALL_GATE
    gate_t = jnp.swapaxes(r2[:, :, g0:g0 + 3 * NSA_HEADS], 1, 2)

    k_spec = lambda blk: pl.BlockSpec((None, s, LANES), lambda bi, g, i, blk=blk: (bi, 0, blk + g))
    vt_spec = lambda w: pl.BlockSpec((None, None, None, nk, dh, TK), lambda bi, g, i, w=w: (bi, w, g, 0, 0, 0))
    return pl.pallas_call(
        functools.partial(_nsa_attn_kernel, n_sel=n_sel, top_n=top_n),
        grid=(b, NSA_KV_GROUPS, nq),
        in_specs=[
            pl.BlockSpec((None, TQ, gw), lambda bi, g, i: (bi, i, g)),
            k_spec(R1_KS), k_spec(R1_KW), vt_spec(0), vt_spec(1),
            pl.BlockSpec((None, None, 2, n_cmp_rows, dh), lambda bi, g, i: (bi, g, 0, 0, 0)),
            pl.BlockSpec((None, 3 * NSA_HEADS, TQ), lambda bi, g, i: (bi, 0, i)),
            pl.BlockSpec((None, None, n_cmp_rows, cols), lambda bi, g, i: (g, i, 0, 0)),
            pl.BlockSpec((None, 3, TK, cols), lambda bi, g, i: (g, 0, 0, 0)),
            pl.BlockSpec((LANES, n_cmp_rows), lambda bi, g, i: (0, 0)),
            pl.BlockSpec((s, LANES), lambda bi, g, i: (0, 0)),
        ],
        out_specs=pl.BlockSpec((None, TQ, gw), lambda bi, g, i: (bi, i, g)),
        out_shape=jax.ShapeDtypeStruct((b, s, NSA_WIDTH), BF16),
        scratch_shapes=[
            pltpu.VMEM((s, TQ), F32),
            pltpu.VMEM((3, 1, cols), F32),
            pltpu.VMEM((3, 1, cols), F32),
            pltpu.VMEM((3, dh, cols), F32),
            pltpu.VMEM((dh, cols), F32),
        ],
        compiler_params=_cparams("parallel", "parallel", "arbitrary"),
        name="nsa_attention",
    )(r1, r1, r1, vt, vt, kcvc, gate_t, bias_c, bias_d, cover_t, expand_t)


GDN_PAIR = 2 * GDN_CHUNK
GDN_GROUP = 4
GDN_ROWS = 512


def _gdn_prep_kernel(q_ref, k_ref, v_ref, smt_ref, alog_ref, dtb_ref, cwq_ref, cwk_ref, cwv_ref,
                     w_out, qg_out, u_out, qk_out, kdt_out, egl_out,
                     kb_s, kk_s, qq_s, vk_s, kd_s, gcol_s, grow_s, xp_s):
    h = pl.program_id(1)
    s = q_ref.shape[0]
    c = GDN_CHUNK
    pr = GDN_PAIR
    dk = GDN_HEAD_DIM
    nt = (((1,), (1,)), ((), ()))
    tn = (((0,), (0,)), ((), ()))

    xg = smt_ref[0:GDN_HEADS, :] + dtb_ref[...]
    g8 = -jnp.exp(alog_ref[...]) * (jnp.maximum(xg, 0.0) + jnp.log1p(jnp.exp(-jnp.abs(xg))))
    beta8 = jax.nn.sigmoid(smt_ref[GDN_HEADS:2 * GDN_HEADS, :])
    pos = lax.broadcasted_iota(jnp.int32, g8.shape, 1) % c
    gcum8 = g8
    sh = 1
    while sh < c:
        gcum8 = gcum8 + jnp.where(pos >= sh, pltpu.roll(gcum8, sh, axis=1), 0.0)
        sh *= 2
    for p in range(s // pr):
        grow_s[p] = gcum8[:, p * pr:(p + 1) * pr]
    sel_k = lax.broadcasted_iota(jnp.int32, (2 * GDN_HEADS, 2 * dk), 0)
    sel_l = lax.broadcasted_iota(jnp.int32, (2 * GDN_HEADS, 2 * dk), 1)
    pick = jnp.where(((sel_k == h) & (sel_l < dk)) | ((sel_k == GDN_HEADS + h) & (sel_l >= dk)), 1.0, 0.0)
    both = jnp.concatenate([gcum8, beta8], axis=0)
    hi = both.astype(BF16)
    rest = both - hi.astype(F32)
    mid = rest.astype(BF16)
    lo = (rest - mid.astype(F32)).astype(BF16)
    col = lax.dot_general(jnp.concatenate([hi, mid, lo], axis=0), jnp.concatenate([pick] * 3, axis=0).astype(BF16),
                          tn, preferred_element_type=F32)
    gcol = col[:, :dk]
    bcol = col[:, dk:]
    gcol_s[...] = gcol

    hist = 8
    xp_s[0:hist, :] = jnp.zeros((hist, dk), F32)

    def conv_silu(x_ref, w_ref):
        x = x_ref[...]
        w = w_ref[...]
        xp_s[hist:hist + s, :] = x
        y = x * w[CONV_WIDTH - 1:CONV_WIDTH, :]
        for sft in range(1, CONV_WIDTH):
            y = y + xp_s[hist - sft:hist - sft + s, :] * w[CONV_WIDTH - 1 - sft:CONV_WIDTH - sft, :]
        return y * jax.nn.sigmoid(y)

    def l2n(x):
        return x * lax.rsqrt(jnp.sum(x * x, axis=-1, keepdims=True) + NORM_EPS)

    q = l2n(conv_silu(q_ref, cwq_ref)) * (dk ** -0.5)
    k = l2n(conv_silu(k_ref, cwk_ref))
    v = conv_silu(v_ref, cwv_ref)
    eg = jnp.exp(gcol)
    kb = k * bcol
    qg_out[...] = (q * eg).astype(BF16)
    kk_s[...] = k.astype(BF16)
    kb_s[...] = kb.astype(BF16)
    qq_s[...] = q.astype(BF16)
    vk_s[:, :dk] = (v * bcol).astype(BF16)
    vk_s[:, dk:] = (kb * eg).astype(BF16)
    g3 = gcol.reshape(s // c, c, dk)
    g_last = g3[:, c - 1:c, :]
    kd_s[...] = (k * jnp.exp(jnp.broadcast_to(g_last, g3.shape).reshape(s, dk) - gcol)).astype(BF16)
    egl_out[...] = jnp.broadcast_to(jnp.exp(g_last), egl_out.shape)

    r_i = lax.broadcasted_iota(jnp.int32, (pr, pr), 0)
    l_i = lax.broadcasted_iota(jnp.int32, (pr, pr), 1)
    tri = ((r_i // c) == (l_i // c)) & (r_i >= l_i)
    eye = r_i == l_i
    eye_f = jnp.where(eye, 1.0, 0.0)
    eye_b = eye_f.astype(BF16)

    def pair_group(ps):
        rows = [slice(p * pr, (p + 1) * pr) for p in ps]
        n = len(ps)
        x = [lax.dot_general(jnp.concatenate([kb_s[r, :], qq_s[r, :]], axis=0), kk_s[r, :], nt,
                             preferred_element_type=F32) for r in rows]
        decay = []
        for p, r in zip(ps, rows):
            diff = gcol_s[r, :] - grow_s[p, pl.ds(h, 1), :]
            decay.append(jnp.where(tri, jnp.exp(jnp.where(tri, diff, 0.0)), 0.0))
        for a in range(n):
            qk_out[rows[a], :] = (x[a][pr:] * decay[a]).astype(BF16)
            kdt_out[ps[a]] = lax.dot_general(eye_b, kd_s[rows[a], :], nt, preferred_element_type=F32).astype(BF16)
        def split(m):
            hi = m.astype(BF16)
            return hi, (m - hi.astype(F32)).astype(BF16)

        def mm3(lhs, rhs):
            return (jnp.dot(lhs[0], rhs[0], preferred_element_type=F32)
                    + jnp.dot(lhs[1], rhs[0], preferred_element_type=F32)
                    + jnp.dot(lhs[0], rhs[1], preferred_element_type=F32))

        def stack(u, v):
            return (jnp.concatenate([u[0], v[0]], axis=0), jnp.concatenate([u[1], v[1]], axis=0))

        qm = [jnp.where(eye, 0.0, -(x[a][:pr] * decay[a])) for a in range(n)]
        tinv = [eye_f + qm[a] for a in range(n)]
        qs = [split(qm[a]) for a in range(n)]
        qm = [mm3(qs[a], qs[a]) for a in range(n)]
        steps = int(math.log2(c)) - 1
        for it in range(steps):
            qs = [split(qm[a]) for a in range(n)]
            ts = [split(tinv[a]) for a in range(n)]
            if it < steps - 1:
                r = [mm3(stack(qs[a], ts[a]), qs[a]) for a in range(n)]
                qm = [r[a][:pr] for a in range(n)]
                tinv = [tinv[a] + r[a][pr:] for a in range(n)]
            else:
                tinv = [tinv[a] + mm3(ts[a], qs[a]) for a in range(n)]
        for a in range(n):
            uw = jnp.dot(tinv[a].astype(BF16), vk_s[rows[a], :], preferred_element_type=F32)
            u_out[rows[a], :] = uw[:, :dk]
            w_out[rows[a], :] = uw[:, dk:].astype(BF16)

    npair = s // pr
    for p0 in range(0, npair, GDN_GROUP):
        pair_group(list(range(p0, min(p0 + GDN_GROUP, npair))))


def _gdn_scan_kernel(w_ref, qg_ref, u_ref, qk_ref, kdt_ref, egl_ref, z_ref, nw_ref, o_ref, state_ref, os_ref):
    c = GDN_CHUNK
    pr = GDN_PAIR
    dk = GDN_HEAD_DIM
    nh = w_ref.shape[0]
    rb = w_ref.shape[1]

    @pl.when(pl.program_id(1) == 0)
    def _():
        state_ref[...] = jnp.zeros(state_ref.shape, F32)

    zeros = jnp.zeros((c, dk), BF16)

    def pair(p, carry):
        r0 = pl.multiple_of(p * pr, pr)
        for half in range(2):
            rows = pl.ds(r0 + half * c, c)
            state = [state_ref[h] for h in range(nh)]
            r1 = [jnp.dot(jnp.concatenate([w_ref[h, rows, :], qg_ref[h, rows, :]], axis=0), state[h].astype(BF16),
                          preferred_element_type=F32) for h in range(nh)]
            vnb = [(u_ref[h, rows, :] - r1[h][:c]).astype(BF16) for h in range(nh)]
            rhs = [jnp.concatenate([vnb[h], zeros] if half == 0 else [zeros, vnb[h]], axis=0) for h in range(nh)]
            r2 = [jnp.dot(jnp.concatenate([qk_ref[h, rows, :], kdt_ref[h, p]], axis=0), rhs[h],
                          preferred_element_type=F32) for h in range(nh)]
            for h in range(nh):
                os_ref[rows, h * dk:(h + 1) * dk] = r1[h][c:] + r2[h][:c]
                state_ref[h] = state[h] * egl_ref[h, 2 * p + half, 0:1, :] + r2[h][c:]
        return carry

    lax.fori_loop(0, rb // pr, pair, 0)

    for h in range(nh):
        o = os_ref[:, h * dk:(h + 1) * dk]
        o = o * lax.rsqrt(jnp.mean(o * o, axis=-1, keepdims=True) + NORM_EPS) * nw_ref[...]
        z = z_ref[:, h * dk:(h + 1) * dk]
        o_ref[:, h * dk:(h + 1) * dk] = (o * (z * jax.nn.sigmoid(z))).astype(o_ref.dtype)


def gated_deltanet(r2, conv_w, a_log, dt_bias, norm_w):
    b, s, _ = r2.shape
    dk = GDN_HEAD_DIM
    nh = GDN_HEADS
    npair = s // GDN_PAIR
    nchunk = s // GDN_CHUNK
    rb = GDN_ROWS
    assert s % GDN_PAIR == 0 and s % rb == 0 and (R2_Z * LANES) % GDN_WIDTH == 0
    a0 = R2_SMALL * LANES + SMALL_A
    smt = jnp.swapaxes(r2[:, :, a0:a0 + 2 * nh], 1, 2)
    col = lambda blk: pl.BlockSpec((None, s, dk), lambda bi, h, blk=blk: (bi, 0, blk + h))
    cw = lambda blk: pl.BlockSpec((CONV_WIDTH, dk), lambda bi, h, blk=blk: (0, blk + h))
    vec = pl.BlockSpec((nh, 1), lambda bi, h: (0, 0))
    per_head = lambda *shape: pl.BlockSpec((None, None) + shape, lambda bi, h: (bi, h) + (0,) * len(shape))
    w, qg, u, qk, kdt, egl = pl.pallas_call(
        _gdn_prep_kernel,
        grid=(b, nh),
        in_specs=[col(R2_GQ), col(R2_GK), col(R2_GV),
                  pl.BlockSpec((None, 2 * nh, s), lambda bi, h: (bi, 0, 0)), vec, vec,
                  cw(0), cw(nh), cw(2 * nh)],
        out_specs=[per_head(s, dk), per_head(s, dk), per_head(s, dk), per_head(s, dk),
                   per_head(npair, GDN_PAIR, GDN_PAIR), per_head(nchunk, 8, dk)],
        out_shape=[jax.ShapeDtypeStruct((b, nh, s, dk), BF16), jax.ShapeDtypeStruct((b, nh, s, dk), BF16),
                   jax.ShapeDtypeStruct((b, nh, s, dk), F32), jax.ShapeDtypeStruct((b, nh, s, dk), BF16),
                   jax.ShapeDtypeStruct((b, nh, npair, GDN_PAIR, GDN_PAIR), BF16),
                   jax.ShapeDtypeStruct((b, nh, nchunk, 8, dk), F32)],
        scratch_shapes=[pltpu.VMEM((s, dk), BF16), pltpu.VMEM((s, dk), BF16), pltpu.VMEM((s, dk), BF16),
                        pltpu.VMEM((s, 2 * dk), BF16), pltpu.VMEM((s, dk), BF16), pltpu.VMEM((s, dk), F32),
                        pltpu.VMEM((npair, nh, GDN_PAIR), F32), pltpu.VMEM((s + 8, dk), F32)],
        compiler_params=_cparams("parallel", "parallel"),
        name="gdn_prep",
    )(r2, r2, r2, smt, a_log.reshape(nh, 1), dt_bias.reshape(nh, 1), conv_w, conv_w, conv_w)

    seq = lambda *shape: pl.BlockSpec((None, nh) + shape, lambda bi, j: (bi, 0, j) + (0,) * (len(shape) - 1))
    return pl.pallas_call(
        _gdn_scan_kernel,
        grid=(b, s // rb),
        in_specs=[seq(rb, dk), seq(rb, dk), seq(rb, dk), seq(rb, dk),
                  seq(rb // GDN_PAIR, GDN_PAIR, GDN_PAIR), seq(rb // GDN_CHUNK, 8, dk),
                  pl.BlockSpec((None, rb, GDN_WIDTH), lambda bi, j: (bi, j, R2_Z * LANES // GDN_WIDTH)),
                  pl.BlockSpec((1, dk), lambda bi, j: (0, 0))],
        out_specs=pl.BlockSpec((None, rb, GDN_WIDTH), lambda bi, j: (bi, j, 0)),
        out_shape=jax.ShapeDtypeStruct((b, s, GDN_WIDTH), BF16),
        scratch_shapes=[pltpu.VMEM((nh, dk, dk), F32), pltpu.VMEM((rb, GDN_WIDTH), F32)],
        compiler_params=_cparams("parallel", "arbitrary"),
        name="gdn_scan",
    )(w, qg, u, qk, kdt, egl, r2, norm_w.reshape(1, dk))


def _split_w_in(w_in):
    o_q = 0
    o_kv = NSA_WIDTH
    o_gate = o_kv + 6 * NSA_KV_WIDTH
    o_qkv = o_gate + 3 * NSA_HEADS
    o_z = o_qkv + 3 * GDN_WIDTH
    o_a = o_z + GDN_WIDTH
    o_b = o_a + GDN_HEADS
    o_m = o_b + GDN_HEADS
    kvw = NSA_KV_WIDTH
    w1 = jnp.concatenate([w_in[:, o_q:o_kv], w_in[:, o_kv + 2 * kvw:o_kv + 6 * kvw], w_in[:, o_m:]], axis=1)
    small = jnp.concatenate([w_in[:, o_gate:o_qkv], w_in[:, o_a:o_m]], axis=1)
    used = (R2_SMALL * LANES) + small.shape[1]
    w2 = jnp.concatenate([w_in[:, o_z:o_a], w_in[:, o_kv:o_kv + 2 * kvw], w_in[:, o_qkv:o_z], small,
                          jnp.zeros((w_in.shape[0], R2_COLS - used), w_in.dtype)], axis=1)
    return jnp.concatenate([w1, w2], axis=1).astype(BF16)


def _r1_col_scale():
    c = np.ones((1, R1_COLS), np.float32)
    c[:, :NSA_WIDTH] = NSA_HEAD_DIM ** -0.5 * LOG2E
    return jnp.asarray(c)


def _layer(x, bias_c, bias_d, ln1_w, w_in, pe_k, pe_v, w1_k, w2_k, w1_v, w2_v, conv_w, a_log, dt_bias,
           gdn_norm_w, w_pa, w_pb, w_o, ln2_w, w_up, w_down, final_norm_w, b, s):
    t = b * s
    r1, r2 = input_projection(x, ln1_w, _split_w_in(w_in), _r1_col_scale())
    r1 = r1.reshape(b, s, R1_COLS)
    r2 = r2.reshape(b, s, R2_COLS)
    kcvc = nsa_compress(r2, jnp.stack([pe_k, pe_v]), jnp.stack([w1_k, w1_v]).astype(BF16),
                        w2_k.astype(BF16), w2_v.T.astype(BF16))
    o_a = nsa_attention(r1, r2, kcvc, bias_c, bias_d).reshape(t, NSA_WIDTH)
    o_b = gated_deltanet(r2, conv_w, a_log, dt_bias, gdn_norm_w).reshape(t, GDN_WIDTH)
    m = merge_branches(o_a, o_b, w_pa.astype(BF16), w_pb.astype(BF16), r1.reshape(t, R1_COLS))
    x, h2 = matmul(m, w_o.astype(BF16), F32, tm=512, tn=D_MODEL, tk=D_MODEL, epilogue="residual_norm", res=x,
                   norm_w=ln2_w, norm_dtype=BF16)
    u = matmul_rows(h2, w_up.astype(BF16), BF16, tm=1024, tn=1024, epilogue="relu2")
    x = matmul_rows(u, w_down.astype(BF16), F32, tm=1024, tn=256, epilogue="residual", res=x)
    if final_norm_w is None:
        return x
    return rmsnorm(x, final_norm_w, F32)


def kernel(x, rel_table, ln1_w, w_in, cmp_pe_k, cmp_pe_v, cmp_w1_k, cmp_w2_k, cmp_w1_v, cmp_w2_v, conv_w, a_log,
           dt_bias, gdn_norm_w, w_pa, w_pb, w_o, ln2_w, w_up, w_down, ln_f_w):
    b, s, d = x.shape
    depth = w_in.shape[0]
    bias_c, bias_d = bias_tables(rel_table, s)
    xt = x.reshape(b * s, d)
    for l in range(depth):
        xt = _layer(xt, bias_c, bias_d, ln1_w[l], w_in[l], cmp_pe_k[l], cmp_pe_v[l], cmp_w1_k[l], cmp_w2_k[l],
                    cmp_w1_v[l], cmp_w2_v[l], conv_w[l], a_log[l], dt_bias[l], gdn_norm_w[l], w_pa[l], w_pb[l],
                    w_o[l], ln2_w[l], w_up[l], w_down[l], ln_f_w if l == depth - 1 else None, b, s)
    return xt.reshape(b, s, d)
```

```python
import functools
import math

import numpy as np
import jax
import jax.numpy as jnp
from jax import lax
from jax.experimental import pallas as pl
from jax.experimental.pallas import tpu as pltpu

F32 = jnp.float32
BF16 = jnp.bfloat16
HIGHEST = lax.Precision.HIGHEST

D_MODEL = 2048
DEPTH = 2
NSA_HEADS = 8
NSA_KV_GROUPS = 2
NSA_HPG = NSA_HEADS // NSA_KV_GROUPS
NSA_HEAD_DIM = 128
NSA_WIDTH = NSA_HEADS * NSA_HEAD_DIM
NSA_KV_WIDTH = NSA_KV_GROUPS * NSA_HEAD_DIM
CMP_BLOCK = 32
CMP_STRIDE = 16
CMP_HIDDEN = 256
SEL_BLOCK = 64
SEL_TOP_N = 16
SEL_FORCE = 1000.0
WINDOW = 512
GDN_HEADS = 8
GDN_HEAD_DIM = 128
GDN_WIDTH = GDN_HEADS * GDN_HEAD_DIM
CONV_WIDTH = 4
GDN_CHUNK = 64
REL_BUCKETS = 32
REL_MAX_DIST = 128
D_FF = 4 * D_MODEL
NORM_EPS = 1e-6

LANES = 128
VMEM_LIMIT = 56 * 1024 * 1024
TQ = 256
TK = 256
NEG_INF = float("-inf")
LOG2E = math.log2(math.e)

TILE_IN_PROJ = (1024, 1024)
TILE_MERGE = (1024, 512)
TILE_OUT_PROJ = 512
TILE_UP = (1024, 1024)
TILE_DOWN = (1024, 256)

R1_KS, R1_VS, R1_KW, R1_VW, R1_MA, R1_MB = 8, 10, 12, 14, 16, 32
R1_COLS = 48 * LANES
R2_Z, R2_KC, R2_VC, R2_GQ, R2_GK, R2_GV, R2_SMALL = 0, 8, 10, 12, 20, 28, 36
R2_COLS = 40 * LANES
SMALL_GATE, SMALL_A, SMALL_B = 0, 24, 32


def _t5_thresholds():
    exact = REL_BUCKETS // 2
    n = np.arange(exact, REL_MAX_DIST * 2, dtype=np.int64)
    nf = n.astype(np.float32)
    large = exact + (np.log(nf / np.float32(exact)) / np.float32(math.log(REL_MAX_DIST / exact))
                     * np.float32(REL_BUCKETS - exact)).astype(np.int32)
    large = np.minimum(large, REL_BUCKETS - 1)
    thr = []
    for b in range(exact + 1, REL_BUCKETS):
        thr.append(int(n[np.argmax(large >= b)]))
    return tuple(thr)


T5_THRESHOLDS = _t5_thresholds()


def _cparams(*sem):
    return pltpu.CompilerParams(dimension_semantics=sem, vmem_limit_bytes=VMEM_LIMIT)


def _rmsnorm_kernel(x_ref, w_ref, o_ref):
    x = x_ref[...]
    y = x * lax.rsqrt(jnp.mean(x * x, axis=-1, keepdims=True) + NORM_EPS)
    o_ref[...] = (y * w_ref[...]).astype(o_ref.dtype)


def rmsnorm(x, w, out_dtype, tm=512):
    t, d = x.shape
    return pl.pallas_call(
        _rmsnorm_kernel,
        grid=(t // tm,),
        in_specs=[pl.BlockSpec((tm, d), lambda i: (i, 0)), pl.BlockSpec((1, d), lambda i: (0, 0))],
        out_specs=pl.BlockSpec((tm, d), lambda i: (i, 0)),
        out_shape=jax.ShapeDtypeStruct((t, d), out_dtype),
        compiler_params=_cparams("parallel"),
        name="rmsnorm",
    )(x, w.reshape(1, d))


def _mm_resnorm_kernel(a_ref, w_ref, res_ref, norm_ref, o_ref, n_ref):
    acc = res_ref[...] + jnp.dot(a_ref[...], w_ref[...], preferred_element_type=F32)
    normed = acc * lax.rsqrt(jnp.mean(acc * acc, axis=-1, keepdims=True) + NORM_EPS) * norm_ref[...]
    n_ref[...] = normed.astype(n_ref.dtype)
    o_ref[...] = acc


def matmul_residual_norm(a, w, res, norm_w, norm_dtype, *, tm):
    m, kdim = a.shape
    n = w.shape[1]
    row = lambda width: pl.BlockSpec((tm, width), lambda i: (i, 0))
    return pl.pallas_call(
        _mm_resnorm_kernel,
        grid=(m // tm,),
        in_specs=[row(kdim), pl.BlockSpec((kdim, n), lambda i: (0, 0)), row(n), pl.BlockSpec((1, n), lambda i: (0, 0))],
        out_specs=[row(n), row(n)],
        out_shape=[jax.ShapeDtypeStruct((m, n), F32), jax.ShapeDtypeStruct((m, n), norm_dtype)],
        compiler_params=_cparams("parallel"),
        name="matmul_residual_norm",
    )(a, w, res, norm_w.reshape(1, n))


def _column_tiles(w, tn):
    k, n = w.shape
    return w.reshape(k, n // tn, tn).transpose(1, 0, 2)


def _row_tile_copy(src_hbm, buf, sem, tile, slot):
    tm = buf.shape[1]
    return pltpu.make_async_copy(src_hbm.at[pl.ds(tile * tm, tm), :], buf.at[slot], sem.at[slot])


def _prefetch_row_tile(src_hbm, buf, sem):
    i = pl.program_id(0)
    slot = i % 2

    @pl.when(pl.program_id(1) == 0)
    def _():
        @pl.when(i == 0)
        def _():
            _row_tile_copy(src_hbm, buf, sem, 0, 0).start()

        _row_tile_copy(src_hbm, buf, sem, i, slot).wait()

        @pl.when(i + 1 < pl.num_programs(0))
        def _():
            _row_tile_copy(src_hbm, buf, sem, i + 1, 1 - slot).start()

    return slot


def _mm_rows_kernel(*refs, epilogue):
    a_hbm, w_ref = refs[0], refs[1]
    res_ref = refs[2] if epilogue == "residual" else None
    o_ref, a_buf, sem = refs[-3], refs[-2], refs[-1]
    slot = _prefetch_row_tile(a_hbm, a_buf, sem)
    acc = jnp.dot(a_buf[slot], w_ref[...], preferred_element_type=F32)
    if epilogue == "relu2":
        acc = jnp.square(jnp.maximum(acc, 0.0))
    elif epilogue == "residual":
        acc = res_ref[...] + acc
    o_ref[...] = acc.astype(o_ref.dtype)


def matmul_rows(a, w, out_dtype, *, tm, tn, epilogue, res=None):
    m, kdim = a.shape
    n = w.shape[1]
    in_specs = [pl.BlockSpec(memory_space=pl.ANY), pl.BlockSpec((None, kdim, tn), lambda i, j: (j, 0, 0))]
    args = [a, _column_tiles(w, tn)]
    if epilogue == "residual":
        in_specs.append(pl.BlockSpec((tm, tn), lambda i, j: (i, j)))
        args.append(res)
    return pl.pallas_call(
        functools.partial(_mm_rows_kernel, epilogue=epilogue),
        grid=(m // tm, n // tn),
        in_specs=in_specs,
        out_specs=pl.BlockSpec((tm, tn), lambda i, j: (i, j)),
        out_shape=jax.ShapeDtypeStruct((m, n), out_dtype),
        scratch_shapes=[pltpu.VMEM((2, tm, kdim), a.dtype), pltpu.SemaphoreType.DMA((2,))],
        compiler_params=_cparams("arbitrary", "arbitrary"),
        name="matmul_rows_" + epilogue,
    )(*args)


def _inproj_kernel(x_hbm, lnw_ref, w_ref, cs_ref, r1_ref, r2_ref, h_ref, x_buf, sem, *, n1):
    j = pl.program_id(1)
    slot = _prefetch_row_tile(x_hbm, x_buf, sem)

    @pl.when(j == 0)
    def _():
        x = x_buf[slot]
        y = x * lax.rsqrt(jnp.mean(x * x, axis=-1, keepdims=True) + NORM_EPS)
        h_ref[...] = (y * lnw_ref[...]).astype(BF16)

    acc = jnp.dot(h_ref[...], w_ref[...], preferred_element_type=F32)

    @pl.when(j < n1)
    def _():
        r1_ref[...] = (acc * cs_ref[...]).astype(r1_ref.dtype)

    @pl.when(j >= n1)
    def _():
        r2_ref[...] = acc


def input_projection(x, ln_w, w_all, col_scale, *, tm=TILE_IN_PROJ[0], tn=TILE_IN_PROJ[1]):
    t, d = x.shape
    n1 = R1_COLS // tn
    n2 = R2_COLS // tn
    assert w_all.shape == (d, R1_COLS + R2_COLS) and R1_COLS % tn == 0 and R2_COLS % tn == 0
    return pl.pallas_call(
        functools.partial(_inproj_kernel, n1=n1),
        grid=(t // tm, n1 + n2),
        in_specs=[pl.BlockSpec(memory_space=pl.ANY),
                  pl.BlockSpec((1, d), lambda i, j: (0, 0)),
                  pl.BlockSpec((None, d, tn), lambda i, j: (j, 0, 0)),
                  pl.BlockSpec((1, tn), lambda i, j: (0, jnp.minimum(j, n1 - 1)))],
        out_specs=[pl.BlockSpec((tm, tn), lambda i, j: (i, jnp.minimum(j, n1 - 1))),
                   pl.BlockSpec((tm, tn), lambda i, j: (i, jnp.maximum(j - n1, 0)))],
        out_shape=[jax.ShapeDtypeStruct((t, R1_COLS), BF16), jax.ShapeDtypeStruct((t, R2_COLS), F32)],
        scratch_shapes=[pltpu.VMEM((tm, d), BF16), pltpu.VMEM((2, tm, d), x.dtype), pltpu.SemaphoreType.DMA((2,))],
        compiler_params=_cparams("arbitrary", "arbitrary"),
        name="input_projection",
    )(x, ln_w.reshape(1, d), _column_tiles(w_all, tn), col_scale)


def _merge_kernel(oa_ref, ob_ref, wa_ref, wb_ref, ma_ref, mb_ref, o_ref):
    pa = jnp.dot(oa_ref[...], wa_ref[...], preferred_element_type=F32)
    pb = jnp.dot(ob_ref[...], wb_ref[...], preferred_element_type=F32)
    ga = jax.nn.sigmoid(ma_ref[...].astype(F32))
    gb = jax.nn.sigmoid(mb_ref[...].astype(F32))
    o_ref[...] = (ga * pa + gb * pb).astype(o_ref.dtype)


def merge_branches(o_a, o_b, w_pa, w_pb, r1, *, tm=TILE_MERGE[0], tn=TILE_MERGE[1]):
    t, ka = o_a.shape
    kb = o_b.shape[1]
    n = w_pa.shape[1]
    ma0 = R1_MA * LANES // tn
    mb0 = R1_MB * LANES // tn
    return pl.pallas_call(
        _merge_kernel,
        grid=(t // tm, n // tn),
        in_specs=[
            pl.BlockSpec((tm, ka), lambda i, j: (i, 0)),
            pl.BlockSpec((tm, kb), lambda i, j: (i, 0)),
            pl.BlockSpec((ka, tn), lambda i, j: (0, j)),
            pl.BlockSpec((kb, tn), lambda i, j: (0, j)),
            pl.BlockSpec((tm, tn), lambda i, j: (i, ma0 + j)),
            pl.BlockSpec((tm, tn), lambda i, j: (i, mb0 + j)),
        ],
        out_specs=pl.BlockSpec((tm, tn), lambda i, j: (i, j)),
        out_shape=jax.ShapeDtypeStruct((t, n), BF16),
        compiler_params=_cparams("parallel", "parallel"),
        name="merge_branches",
    )(o_a, o_b, w_pa, w_pb, r1, r1)


def _t5_lookup(dist, tab_ref, h):
    n = jnp.maximum(dist, 0)
    exact = REL_BUCKETS // 2
    large = jnp.full(n.shape, exact, jnp.int32)
    for thr in T5_THRESHOLDS:
        large = large + (n >= thr).astype(jnp.int32)
    bucket = jnp.where(n < exact, n, large)
    out = jnp.zeros(n.shape, F32)
    for b in range(REL_BUCKETS):
        out = jnp.where(bucket == b, tab_ref[b, h] * LOG2E, out)
    return out


def _bias_kernel(tab_ref, bc_ref, bd_ref):
    g = pl.program_id(0)
    i = pl.program_id(1)
    ncr = bc_ref.shape[0]
    blk = lax.broadcasted_iota(jnp.int32, (ncr, TQ), 0)
    qry_c = lax.broadcasted_iota(jnp.int32, (ncr, TQ), 1)
    for j in range(NSA_HPG):
        h = g * NSA_HPG + j
        bc_ref[:, j * TQ:(j + 1) * TQ] = _t5_lookup(i * TQ + qry_c - (blk * CMP_STRIDE + CMP_BLOCK - 1), tab_ref, h)

    @pl.when(i == 0)
    def _():
        key = lax.broadcasted_iota(jnp.int32, (TK, TQ), 0)
        qry = lax.broadcasted_iota(jnp.int32, (TK, TQ), 1)
        for v in range(3):
            for j in range(NSA_HPG):
                bd_ref[v, :, j * TQ:(j + 1) * TQ] = _t5_lookup(v * TK + qry - key, tab_ref, g * NSA_HPG + j)


def bias_tables(rel_table, s):
    nq = s // TQ
    cols = NSA_HPG * TQ
    return pl.pallas_call(
        _bias_kernel,
        grid=(NSA_KV_GROUPS, nq),
        in_specs=[pl.BlockSpec(memory_space=pltpu.SMEM)],
        out_specs=[pl.BlockSpec((None, None, s // CMP_STRIDE, cols), lambda g, i: (g, i, 0, 0)),
                   pl.BlockSpec((None, 3, TK, cols), lambda g, i: (g, 0, 0, 0))],
        out_shape=[jax.ShapeDtypeStruct((NSA_KV_GROUPS, nq, s // CMP_STRIDE, cols), F32),
                   jax.ShapeDtypeStruct((NSA_KV_GROUPS, 3, TK, cols), F32)],
        compiler_params=_cparams("arbitrary", "arbitrary"),
        name="t5_bias_tables",
    )(rel_table)


def _gelu_tanh(x):
    return 0.5 * x * (1.0 + jnp.tanh(math.sqrt(2.0 / math.pi) * (x + 0.044715 * (x * x * x))))


def _compress_kernel(x_ref, pe_ref, w1_ref, w2k_ref, w2vt_ref, o_ref, *, nblk):
    pe = pe_ref[...]
    acc_a = jnp.zeros((nblk, CMP_HIDDEN), F32)
    acc_b = jnp.zeros((nblk, CMP_HIDDEN), F32)
    for r in range(CMP_STRIDE):
        xr = x_ref[pl.ds(r, nblk, stride=CMP_STRIDE), :]
        xa = (xr + pe[r:r + 1, :]).astype(BF16)
        xb = (xr + pe[CMP_STRIDE + r:CMP_STRIDE + r + 1, :]).astype(BF16)
        acc_a = acc_a + jnp.dot(xa, w1_ref[r * NSA_HEAD_DIM:(r + 1) * NSA_HEAD_DIM, :], preferred_element_type=F32)
        acc_b = acc_b + jnp.dot(xb, w1_ref[(CMP_STRIDE + r) * NSA_HEAD_DIM:(CMP_STRIDE + r + 1) * NSA_HEAD_DIM, :],
                                preferred_element_type=F32)
    hidden = acc_a + jnp.concatenate([acc_b[1:], acc_b[:1]], axis=0)
    act = _gelu_tanh(hidden).astype(BF16)

    @pl.when(pl.program_id(2) == 0)
    def _():
        o_ref[...] = jnp.dot(act, w2k_ref[...], preferred_element_type=F32).astype(o_ref.dtype)

    @pl.when(pl.program_id(2) == 1)
    def _():
        o_ref[...] = lax.dot_general(w2vt_ref[...], act, (((1,), (1,)), ((), ())),
                                     preferred_element_type=F32).astype(o_ref.dtype)


def nsa_compress(r2, pe, w1, w2k, w2vt):
    b, s, _ = r2.shape
    nblk = s // CMP_STRIDE
    assert CMP_BLOCK == 2 * CMP_STRIDE and nblk == NSA_HEAD_DIM
    return pl.pallas_call(
        functools.partial(_compress_kernel, nblk=nblk),
        grid=(b, NSA_KV_GROUPS, 2),
        in_specs=[
            pl.BlockSpec((None, s, LANES), lambda bi, g, c: (bi, 0, R2_KC + 2 * c + g)),
            pl.BlockSpec((None, CMP_BLOCK, NSA_HEAD_DIM), lambda bi, g, c: (c, 0, 0)),
            pl.BlockSpec((None, CMP_BLOCK * NSA_HEAD_DIM, CMP_HIDDEN), lambda bi, g, c: (c, 0, 0)),
            pl.BlockSpec((CMP_HIDDEN, NSA_HEAD_DIM), lambda bi, g, c: (0, 0)),
            pl.BlockSpec((NSA_HEAD_DIM, CMP_HIDDEN), lambda bi, g, c: (0, 0)),
        ],
        out_specs=pl.BlockSpec((None, None, None, nblk, NSA_HEAD_DIM), lambda bi, g, c: (bi, g, c, 0, 0)),
        out_shape=jax.ShapeDtypeStruct((b, NSA_KV_GROUPS, 2, nblk, NSA_HEAD_DIM), BF16),
        compiler_params=_cparams("parallel", "parallel", "arbitrary"),
        name="nsa_compress",
    )(r2, pe, w1, w2k, w2vt)


def _nsa_attn_kernel(q_ref, ks_ref, kw_ref, vst_ref, vwt_ref, kcvc_ref, gate_ref, biasc_ref, biasd_ref,
                     covert_ref, expandt_ref, o_ref, sel_ref, m_ref, l_ref, acc_ref, res_ref, *, n_sel, top_n):
    g = pl.program_id(1)
    i = pl.program_id(2)
    cols = NSA_HPG * TQ
    nt = (((1,), (1,)), ((), ()))

    qf = q_ref[...]
    qs = jnp.concatenate([qf[:, j * NSA_HEAD_DIM:(j + 1) * NSA_HEAD_DIM] for j in range(NSA_HPG)], axis=0)
    key = lax.broadcasted_iota(jnp.int32, (TK, cols), 0)
    t_loc = lax.broadcasted_iota(jnp.int32, (TK, cols), 1) % TQ

    def gate_row(c):
        return jnp.concatenate(
            [jax.nn.sigmoid(gate_ref[pl.ds((g * NSA_HPG + j) * 3 + c, 1), :]) for j in range(NSA_HPG)], axis=1)

    def lanes4(x):
        return jnp.concatenate([x] * NSA_HPG, axis=1)

    s = lax.dot_general(kcvc_ref[0], qs, nt, preferred_element_type=F32) + biasc_ref[...]
    cmp_end = lax.broadcasted_iota(jnp.int32, s.shape, 0) * CMP_STRIDE + (CMP_BLOCK - 1)
    s = jnp.where(cmp_end <= i * TQ + lax.broadcasted_iota(jnp.int32, s.shape, 1) % TQ, s, NEG_INF)
    mx = jnp.max(s, axis=0, keepdims=True)
    mx = jnp.where(mx == NEG_INF, 0.0, mx)
    e = jnp.exp2(s - mx)
    p = e / jnp.maximum(jnp.sum(e, axis=0, keepdims=True), 1e-30)
    o_c = jnp.dot(kcvc_ref[1], p.astype(BF16), preferred_element_type=F32)
    res_ref[...] = gate_row(0) * o_c

    psum = p[:, 0:TQ]
    for j in range(1, NSA_HPG):
        psum = psum + p[:, j * TQ:(j + 1) * TQ]
    imp = jnp.dot(covert_ref[...], psum, preferred_element_type=F32, precision=HIGHEST)[:n_sel]
    blk = lax.broadcasted_iota(jnp.int32, (n_sel, TQ), 0)
    cur = (i * TQ + lax.broadcasted_iota(jnp.int32, (n_sel, TQ), 1)) // SEL_BLOCK
    causal_blk = blk <= cur
    forced = (blk == 0) | (blk == cur) | (blk == cur - 1)
    score = jnp.where(causal_blk, imp + jnp.where(forced, SEL_FORCE, 0.0), NEG_INF)
    rank = jnp.zeros((n_sel, TQ), jnp.int32)
    for mp in range(n_sel):
        other = score[mp:mp + 1, :]
        beats = (other > score) | ((other == score) & (blk > mp))
        rank = rank + beats.astype(jnp.int32)
    sel = jnp.where((rank < top_n) & causal_blk, 1.0, 0.0)
    sel = jnp.concatenate([sel, jnp.zeros((LANES - n_sel, TQ), F32)], axis=0).astype(BF16)
    sel_ref[...] = jnp.where(jnp.dot(expandt_ref[...], sel, preferred_element_type=F32) > 0.5, 0.0, NEG_INF)

    def reset():
        m_ref[...] = jnp.full(m_ref.shape, NEG_INF, F32)
        l_ref[...] = jnp.zeros(l_ref.shape, F32)
        acc_ref[...] = jnp.zeros(acc_ref.shape, F32)

    far_bias = biasd_ref[2, 0:1, :]

    def attend(items):
        def scores(item):
            return lax.dot_general(item[1], qs, nt, preferred_element_type=F32)

        def softmax_step(item, s):
            st, _, _, variant, mask_add = item
            if variant < 2:
                s = s + biasd_ref[variant]
            if mask_add is not None:
                s = s + mask_add
            smax = jnp.max(s, axis=0, keepdims=True)
            if variant == 2:
                smax = smax + far_bias
            m_prev = m_ref[st]
            m_new = jnp.maximum(m_prev, smax)
            m_safe = jnp.where(m_new == NEG_INF, 0.0, m_new)
            alpha = jnp.exp2(m_prev - m_safe)
            p = jnp.exp2(s + ((far_bias - m_safe) if variant == 2 else -m_safe))
            l_ref[st] = alpha * l_ref[st] + jnp.sum(p, axis=0, keepdims=True)
            m_ref[st] = m_new
            return alpha, p.astype(BF16)

        def value_step(item, alpha, pb):
            st, vt = item[0], item[2]
            acc_ref[st] = alpha * acc_ref[st] + jnp.dot(vt, pb, preferred_element_type=F32)

        s_next = scores(items[0])
        pending = None
        for n, item in enumerate(items):
            s_cur = s_next
            if n + 1 < len(items):
                s_next = scores(items[n + 1])
            alpha, pb = softmax_step(item, s_cur)
            if pending is not None:
                value_step(*pending)
            pending = (item, alpha, pb)
        value_step(*pending)

    def tile(ref, k):
        return ref[pl.ds(pl.multiple_of(k * TK, TK), TK), :]

    causal_add = jnp.where(key <= t_loc, 0.0, NEG_INF)
    edge_add = jnp.where(key > t_loc, 0.0, NEG_INF)
    n_win = WINDOW // TK
    SEL, WIN, SEL2 = 0, 1, 2
    reset()

    def near_items(depth):
        items = []
        for d in range(depth + 1):
            k = i - d
            sel_mask = lanes4(tile(sel_ref, k))
            items.append((SEL, tile(ks_ref, k), vst_ref[k], min(d, 2), sel_mask + causal_add if d == 0 else sel_mask))
            win_mask = causal_add if d == 0 else (edge_add if d == n_win else None)
            items.append((WIN, tile(kw_ref, k), vwt_ref[k], min(d, 2), win_mask))
        return items

    for depth in range(n_win + 1):
        @pl.when((i == depth) if depth < n_win else (i >= depth))
        def _(depth=depth):
            attend(near_items(depth))

    n_far = jnp.maximum(i - n_win, 0)

    def far_item(stream, k):
        return (stream, tile(ks_ref, k), vst_ref[k], 2, lanes4(tile(sel_ref, k)))

    def far_pair(kk, carry):
        attend([far_item(SEL, 2 * kk), far_item(SEL2, 2 * kk + 1)])
        return carry

    lax.fori_loop(0, n_far // 2, far_pair, 0)

    @pl.when(n_far % 2 == 1)
    def _():
        attend([far_item(SEL, n_far - 1)])

    m_a, m_b = m_ref[SEL], m_ref[SEL2]
    m_ab = jnp.maximum(m_a, m_b)
    m_ab = jnp.where(m_ab == NEG_INF, 0.0, m_ab)
    w_a = jnp.exp2(m_a - m_ab)
    w_b = jnp.exp2(m_b - m_ab)
    l_sel = w_a * l_ref[SEL] + w_b * l_ref[SEL2]
    o_sel = (w_a * acc_ref[SEL] + w_b * acc_ref[SEL2]) * (1.0 / jnp.maximum(l_sel, 1e-30))
    o_win = acc_ref[WIN] * (1.0 / jnp.maximum(l_ref[WIN], 1e-30))
    res = res_ref[...] + gate_row(1) * o_sel + gate_row(2) * o_win
    for j in range(NSA_HPG):
        o_ref[:, j * NSA_HEAD_DIM:(j + 1) * NSA_HEAD_DIM] = res[:, j * TQ:(j + 1) * TQ].T.astype(o_ref.dtype)


def nsa_attention(r1, r2, kcvc, bias_c, bias_d):
    b, s, _ = r1.shape
    n_cmp_rows = s // CMP_STRIDE
    n_sel = s // SEL_BLOCK
    assert n_cmp_rows % LANES == 0 and n_sel <= LANES and n_sel % 8 == 0 and WINDOW % TK == 0 and TQ == TK
    assert 2 * TK - (TQ - 1) >= REL_MAX_DIST
    top_n = min(SEL_TOP_N, n_sel)
    nq = s // TQ
    nk = s // TK
    gw = NSA_HPG * NSA_HEAD_DIM
    cols = NSA_HPG * TQ
    dh = NSA_HEAD_DIM

    n_cmp = n_cmp_rows - CMP_BLOCK // CMP_STRIDE + 1
    c_start = np.arange(n_cmp_rows) * CMP_STRIDE
    s_start = np.arange(LANES) * SEL_BLOCK
    cover_t = ((c_start[None, :] <= s_start[:, None] + SEL_BLOCK - 1) & (c_start[None, :] + CMP_BLOCK - 1 >= s_start[:, None])
               & (np.arange(n_cmp_rows)[None, :] < n_cmp) & (np.arange(LANES)[:, None] < n_sel))
    cover_t = jnp.asarray(cover_t.astype(np.float32))
    expand_t = jnp.asarray(((np.arange(s)[:, None] // SEL_BLOCK) == np.arange(LANES)[None, :]).astype(np.float32), BF16)

    v0 = R1_VS * LANES
    v1 = R1_VW * LANES
    vt = jnp.stack([r1[:, :, v0:v0 + NSA_KV_WIDTH], r1[:, :, v1:v1 + NSA_KV_WIDTH]], axis=1)
    vt = vt.reshape(b, 2, nk, TK, NSA_KV_GROUPS, dh).transpose(0, 1, 4, 2, 5, 3)
    g0 = R2_SMALL * LANES + SMALL_GATE
    gate_t = jnp.swapaxes(r2[:, :, g0:g0 + 3 * NSA_HEADS], 1, 2)

    k_spec = lambda blk: pl.BlockSpec((None, s, LANES), lambda bi, g, i, blk=blk: (bi, 0, blk + g))
    vt_spec = lambda w: pl.BlockSpec((None, None, None, nk, dh, TK), lambda bi, g, i, w=w: (bi, w, g, 0, 0, 0))
    return pl.pallas_call(
        functools.partial(_nsa_attn_kernel, n_sel=n_sel, top_n=top_n),
        grid=(b, NSA_KV_GROUPS, nq),
        in_specs=[
            pl.BlockSpec((None, TQ, gw), lambda bi, g, i: (bi, i, g)),
            k_spec(R1_KS), k_spec(R1_KW), vt_spec(0), vt_spec(1),
            pl.BlockSpec((None, None, 2, n_cmp_rows, dh), lambda bi, g, i: (bi, g, 0, 0, 0)),
            pl.BlockSpec((None, 3 * NSA_HEADS, TQ), lambda bi, g, i: (bi, 0, i)),
            pl.BlockSpec((None, None, n_cmp_rows, cols), lambda bi, g, i: (g, i, 0, 0)),
            pl.BlockSpec((None, 3, TK, cols), lambda bi, g, i: (g, 0, 0, 0)),
            pl.BlockSpec((LANES, n_cmp_rows), lambda bi, g, i: (0, 0)),
            pl.BlockSpec((s, LANES), lambda bi, g, i: (0, 0)),
        ],
        out_specs=pl.BlockSpec((None, TQ, gw), lambda bi, g, i: (bi, i, g)),
        out_shape=jax.ShapeDtypeStruct((b, s, NSA_WIDTH), BF16),
        scratch_shapes=[
            pltpu.VMEM((s, TQ), F32),
            pltpu.VMEM((3, 1, cols), F32),
            pltpu.VMEM((3, 1, cols), F32),
            pltpu.VMEM((3, dh, cols), F32),
            pltpu.VMEM((dh, cols), F32),
        ],
        compiler_params=_cparams("parallel", "parallel", "arbitrary"),
        name="nsa_attention",
    )(r1, r1, r1, vt, vt, kcvc, gate_t, bias_c, bias_d, cover_t, expand_t)


GDN_PAIR = 2 * GDN_CHUNK
GDN_GROUP = 4
GDN_ROWS = 512


def _gdn_prep_kernel(q_ref, k_ref, v_ref, smt_ref, alog_ref, dtb_ref, cwq_ref, cwk_ref, cwv_ref,
                     w_out, qg_out, u_out, qk_out, kdt_out, egl_out,
                     kb_s, kk_s, qq_s, vk_s, kd_s, gcol_s, grow_s, xp_s):
    h = pl.program_id(1)
    s = q_ref.shape[0]
    c = GDN_CHUNK
    pr = GDN_PAIR
    dk = GDN_HEAD_DIM
    nt = (((1,), (1,)), ((), ()))
    tn = (((0,), (0,)), ((), ()))

    xg = smt_ref[0:GDN_HEADS, :] + dtb_ref[...]
    g8 = -jnp.exp(alog_ref[...]) * (jnp.maximum(xg, 0.0) + jnp.log1p(jnp.exp(-jnp.abs(xg))))
    beta8 = jax.nn.sigmoid(smt_ref[GDN_HEADS:2 * GDN_HEADS, :])
    pos = lax.broadcasted_iota(jnp.int32, g8.shape, 1) % c
    gcum8 = g8
    sh = 1
    while sh < c:
        gcum8 = gcum8 + jnp.where(pos >= sh, pltpu.roll(gcum8, sh, axis=1), 0.0)
        sh *= 2
    for p in range(s // pr):
        grow_s[p] = gcum8[:, p * pr:(p + 1) * pr]
    sel_k = lax.broadcasted_iota(jnp.int32, (2 * GDN_HEADS, 2 * dk), 0)
    sel_l = lax.broadcasted_iota(jnp.int32, (2 * GDN_HEADS, 2 * dk), 1)
    pick = jnp.where(((sel_k == h) & (sel_l < dk)) | ((sel_k == GDN_HEADS + h) & (sel_l >= dk)), 1.0, 0.0)
    both = jnp.concatenate([gcum8, beta8], axis=0)
    hi = both.astype(BF16)
    rest = both - hi.astype(F32)
    mid = rest.astype(BF16)
    lo = (rest - mid.astype(F32)).astype(BF16)
    col = lax.dot_general(jnp.concatenate([hi, mid, lo], axis=0), jnp.concatenate([pick] * 3, axis=0).astype(BF16),
                          tn, preferred_element_type=F32)
    gcol = col[:, :dk]
    bcol = col[:, dk:]
    gcol_s[...] = gcol

    hist = 8
    xp_s[0:hist, :] = jnp.zeros((hist, dk), F32)

    def conv_silu(x_ref, w_ref):
        x = x_ref[...]
        w = w_ref[...]
        xp_s[hist:hist + s, :] = x
        y = x * w[CONV_WIDTH - 1:CONV_WIDTH, :]
        for sft in range(1, CONV_WIDTH):
            y = y + xp_s[hist - sft:hist - sft + s, :] * w[CONV_WIDTH - 1 - sft:CONV_WIDTH - sft, :]
        return y * jax.nn.sigmoid(y)

    def l2n(x):
        return x * lax.rsqrt(jnp.sum(x * x, axis=-1, keepdims=True) + NORM_EPS)

    q = l2n(conv_silu(q_ref, cwq_ref)) * (dk ** -0.5)
    k = l2n(conv_silu(k_ref, cwk_ref))
    v = conv_silu(v_ref, cwv_ref)
    eg = jnp.exp(gcol)
    kb = k * bcol
    qg_out[...] = (q * eg).astype(BF16)
    kk_s[...] = k.astype(BF16)
    kb_s[...] = kb.astype(BF16)
    qq_s[...] = q.astype(BF16)
    vk_s[:, :dk] = (v * bcol).astype(BF16)
    vk_s[:, dk:] = (kb * eg).astype(BF16)
    g3 = gcol.reshape(s // c, c, dk)
    g_last = g3[:, c - 1:c, :]
    kd_s[...] = (k * jnp.exp(jnp.broadcast_to(g_last, g3.shape).reshape(s, dk) - gcol)).astype(BF16)
    egl_out[...] = jnp.broadcast_to(jnp.exp(g_last), egl_out.shape)

    r_i = lax.broadcasted_iota(jnp.int32, (pr, pr), 0)
    l_i = lax.broadcasted_iota(jnp.int32, (pr, pr), 1)
    tri = ((r_i // c) == (l_i // c)) & (r_i >= l_i)
    eye = r_i == l_i
    eye_f = jnp.where(eye, 1.0, 0.0)
    eye_b = eye_f.astype(BF16)

    def pair_group(ps):
        rows = [slice(p * pr, (p + 1) * pr) for p in ps]
        n = len(ps)
        x = [lax.dot_general(jnp.concatenate([kb_s[r, :], qq_s[r, :]], axis=0), kk_s[r, :], nt,
                             preferred_element_type=F32) for r in rows]
        decay = []
        for p, r in zip(ps, rows):
            diff = gcol_s[r, :] - grow_s[p, pl.ds(h, 1), :]
            decay.append(jnp.where(tri, jnp.exp(jnp.where(tri, diff, 0.0)), 0.0))
        for a in range(n):
            qk_out[rows[a], :] = (x[a][pr:] * decay[a]).astype(BF16)
            kdt_out[ps[a]] = lax.dot_general(eye_b, kd_s[rows[a], :], nt, preferred_element_type=F32).astype(BF16)
        def split(m):
            hi = m.astype(BF16)
            return hi, (m - hi.astype(F32)).astype(BF16)

        def mm3(lhs, rhs):
            return (jnp.dot(lhs[0], rhs[0], preferred_element_type=F32)
                    + jnp.dot(lhs[1], rhs[0], preferred_element_type=F32)
                    + jnp.dot(lhs[0], rhs[1], preferred_element_type=F32))

        def stack(u, v):
            return (jnp.concatenate([u[0], v[0]], axis=0), jnp.concatenate([u[1], v[1]], axis=0))

        qm = [jnp.where(eye, 0.0, -(x[a][:pr] * decay[a])) for a in range(n)]
        tinv = [eye_f + qm[a] for a in range(n)]
        qs = [split(qm[a]) for a in range(n)]
        qm = [mm3(qs[a], qs[a]) for a in range(n)]
        steps = int(math.log2(c)) - 1
        for it in range(steps):
            qs = [split(qm[a]) for a in range(n)]
            ts = [split(tinv[a]) for a in range(n)]
            if it < steps - 1:
                r = [mm3(stack(qs[a], ts[a]), qs[a]) for a in range(n)]
                qm = [r[a][:pr] for a in range(n)]
                tinv = [tinv[a] + r[a][pr:] for a in range(n)]
            else:
                tinv = [tinv[a] + mm3(ts[a], qs[a]) for a in range(n)]
        for a in range(n):
            uw = jnp.dot(tinv[a].astype(BF16), vk_s[rows[a], :], preferred_element_type=F32)
            u_out[rows[a], :] = uw[:, :dk]
            w_out[rows[a], :] = uw[:, dk:].astype(BF16)

    npair = s // pr
    for p0 in range(0, npair, GDN_GROUP):
        pair_group(list(range(p0, min(p0 + GDN_GROUP, npair))))


def _gdn_scan_kernel(w_ref, qg_ref, u_ref, qk_ref, kdt_ref, egl_ref, z_ref, nw_ref, o_ref, state_ref, os_ref):
    c = GDN_CHUNK
    pr = GDN_PAIR
    dk = GDN_HEAD_DIM
    nh = w_ref.shape[0]
    rb = w_ref.shape[1]

    @pl.when(pl.program_id(1) == 0)
    def _():
        state_ref[...] = jnp.zeros(state_ref.shape, F32)

    zeros = jnp.zeros((c, dk), BF16)

    def pair(p, carry):
        r0 = pl.multiple_of(p * pr, pr)
        for half in range(2):
            rows = pl.ds(r0 + half * c, c)
            state = [state_ref[h] for h in range(nh)]
            r1 = [jnp.dot(jnp.concatenate([w_ref[h, rows, :], qg_ref[h, rows, :]], axis=0), state[h].astype(BF16),
                          preferred_element_type=F32) for h in range(nh)]
            vnb = [(u_ref[h, rows, :] - r1[h][:c]).astype(BF16) for h in range(nh)]
            rhs = [jnp.concatenate([vnb[h], zeros] if half == 0 else [zeros, vnb[h]], axis=0) for h in range(nh)]
            r2 = [jnp.dot(jnp.concatenate([qk_ref[h, rows, :], kdt_ref[h, p]], axis=0), rhs[h],
                          preferred_element_type=F32) for h in range(nh)]
            for h in range(nh):
                os_ref[rows, h * dk:(h + 1) * dk] = r1[h][c:] + r2[h][:c]
                state_ref[h] = state[h] * egl_ref[h, 2 * p + half, 0:1, :] + r2[h][c:]
        return carry

    lax.fori_loop(0, rb // pr, pair, 0)

    for h in range(nh):
        o = os_ref[:, h * dk:(h + 1) * dk]
        o = o * lax.rsqrt(jnp.mean(o * o, axis=-1, keepdims=True) + NORM_EPS) * nw_ref[...]
        z = z_ref[:, h * dk:(h + 1) * dk]
        o_ref[:, h * dk:(h + 1) * dk] = (o * (z * jax.nn.sigmoid(z))).astype(o_ref.dtype)


def gated_deltanet(r2, conv_w, a_log, dt_bias, norm_w):
    b, s, _ = r2.shape
    dk = GDN_HEAD_DIM
    nh = GDN_HEADS
    npair = s // GDN_PAIR
    nchunk = s // GDN_CHUNK
    rb = GDN_ROWS
    assert s % GDN_PAIR == 0 and s % rb == 0 and (R2_Z * LANES) % GDN_WIDTH == 0
    a0 = R2_SMALL * LANES + SMALL_A
    smt = jnp.swapaxes(r2[:, :, a0:a0 + 2 * nh], 1, 2)
    col = lambda blk: pl.BlockSpec((None, s, dk), lambda bi, h, blk=blk: (bi, 0, blk + h))
    cw = lambda blk: pl.BlockSpec((CONV_WIDTH, dk), lambda bi, h, blk=blk: (0, blk + h))
    vec = pl.BlockSpec((nh, 1), lambda bi, h: (0, 0))
    per_head = lambda *shape: pl.BlockSpec((None, None) + shape, lambda bi, h: (bi, h) + (0,) * len(shape))
    w, qg, u, qk, kdt, egl = pl.pallas_call(
        _gdn_prep_kernel,
        grid=(b, nh),
        in_specs=[col(R2_GQ), col(R2_GK), col(R2_GV),
                  pl.BlockSpec((None, 2 * nh, s), lambda bi, h: (bi, 0, 0)), vec, vec,
                  cw(0), cw(nh), cw(2 * nh)],
        out_specs=[per_head(s, dk), per_head(s, dk), per_head(s, dk), per_head(s, dk),
                   per_head(npair, GDN_PAIR, GDN_PAIR), per_head(nchunk, 8, dk)],
        out_shape=[jax.ShapeDtypeStruct((b, nh, s, dk), BF16), jax.ShapeDtypeStruct((b, nh, s, dk), BF16),
                   jax.ShapeDtypeStruct((b, nh, s, dk), F32), jax.ShapeDtypeStruct((b, nh, s, dk), BF16),
                   jax.ShapeDtypeStruct((b, nh, npair, GDN_PAIR, GDN_PAIR), BF16),
                   jax.ShapeDtypeStruct((b, nh, nchunk, 8, dk), F32)],
        scratch_shapes=[pltpu.VMEM((s, dk), BF16), pltpu.VMEM((s, dk), BF16), pltpu.VMEM((s, dk), BF16),
                        pltpu.VMEM((s, 2 * dk), BF16), pltpu.VMEM((s, dk), BF16), pltpu.VMEM((s, dk), F32),
                        pltpu.VMEM((npair, nh, GDN_PAIR), F32), pltpu.VMEM((s + 8, dk), F32)],
        compiler_params=_cparams("parallel", "parallel"),
        name="gdn_prep",
    )(r2, r2, r2, smt, a_log.reshape(nh, 1), dt_bias.reshape(nh, 1), conv_w, conv_w, conv_w)

    seq = lambda *shape: pl.BlockSpec((None, nh) + shape, lambda bi, j: (bi, 0, j) + (0,) * (len(shape) - 1))
    return pl.pallas_call(
        _gdn_scan_kernel,
        grid=(b, s // rb),
        in_specs=[seq(rb, dk), seq(rb, dk), seq(rb, dk), seq(rb, dk),
                  seq(rb // GDN_PAIR, GDN_PAIR, GDN_PAIR), seq(rb // GDN_CHUNK, 8, dk),
                  pl.BlockSpec((None, rb, GDN_WIDTH), lambda bi, j: (bi, j, R2_Z * LANES // GDN_WIDTH)),
                  pl.BlockSpec((1, dk), lambda bi, j: (0, 0))],
        out_specs=pl.BlockSpec((None, rb, GDN_WIDTH), lambda bi, j: (bi, j, 0)),
        out_shape=jax.ShapeDtypeStruct((b, s, GDN_WIDTH), BF16),
        scratch_shapes=[pltpu.VMEM((nh, dk, dk), F32), pltpu.VMEM((rb, GDN_WIDTH), F32)],
        compiler_params=_cparams("parallel", "arbitrary"),
        name="gdn_scan",
    )(w, qg, u, qk, kdt, egl, r2, norm_w.reshape(1, dk))


def _split_w_in(w_in):
    o_q = 0
    o_kv = NSA_WIDTH
    o_gate = o_kv + 6 * NSA_KV_WIDTH
    o_qkv = o_gate + 3 * NSA_HEADS
    o_z = o_qkv + 3 * GDN_WIDTH
    o_a = o_z + GDN_WIDTH
    o_b = o_a + GDN_HEADS
    o_m = o_b + GDN_HEADS
    kvw = NSA_KV_WIDTH
    w1 = jnp.concatenate([w_in[:, o_q:o_kv], w_in[:, o_kv + 2 * kvw:o_kv + 6 * kvw], w_in[:, o_m:]], axis=1)
    small = jnp.concatenate([w_in[:, o_gate:o_qkv], w_in[:, o_a:o_m]], axis=1)
    used = (R2_SMALL * LANES) + small.shape[1]
    w2 = jnp.concatenate([w_in[:, o_z:o_a], w_in[:, o_kv:o_kv + 2 * kvw], w_in[:, o_qkv:o_z], small,
                          jnp.zeros((w_in.shape[0], R2_COLS - used), w_in.dtype)], axis=1)
    return jnp.concatenate([w1, w2], axis=1).astype(BF16)


def _r1_col_scale():
    c = np.ones((1, R1_COLS), np.float32)
    c[:, :NSA_WIDTH] = NSA_HEAD_DIM ** -0.5 * LOG2E
    return jnp.asarray(c)


def _layer(x, bias_c, bias_d, ln1_w, w_in, pe_k, pe_v, w1_k, w2_k, w1_v, w2_v, conv_w, a_log, dt_bias,
           gdn_norm_w, w_pa, w_pb, w_o, ln2_w, w_up, w_down, final_norm_w, b, s):
    t = b * s
    r1, r2 = input_projection(x, ln1_w, _split_w_in(w_in), _r1_col_scale())
    r1 = r1.reshape(b, s, R1_COLS)
    r2 = r2.reshape(b, s, R2_COLS)
    kcvc = nsa_compress(r2, jnp.stack([pe_k, pe_v]), jnp.stack([w1_k, w1_v]).astype(BF16),
                        w2_k.astype(BF16), w2_v.T.astype(BF16))
    o_a = nsa_attention(r1, r2, kcvc, bias_c, bias_d).reshape(t, NSA_WIDTH)
    o_b = gated_deltanet(r2, conv_w, a_log, dt_bias, gdn_norm_w).reshape(t, GDN_WIDTH)
    m = merge_branches(o_a, o_b, w_pa.astype(BF16), w_pb.astype(BF16), r1.reshape(t, R1_COLS))
    x, h2 = matmul_residual_norm(m, w_o.astype(BF16), x, ln2_w, BF16, tm=TILE_OUT_PROJ)
    u = matmul_rows(h2, w_up.astype(BF16), BF16, tm=TILE_UP[0], tn=TILE_UP[1], epilogue="relu2")
    x = matmul_rows(u, w_down.astype(BF16), F32, tm=TILE_DOWN[0], tn=TILE_DOWN[1], epilogue="residual", res=x)
    if final_norm_w is None:
        return x
    return rmsnorm(x, final_norm_w, F32)


def kernel(x, rel_table, ln1_w, w_in, cmp_pe_k, cmp_pe_v, cmp_w1_k, cmp_w2_k, cmp_w1_v, cmp_w2_v, conv_w, a_log,
           dt_bias, gdn_norm_w, w_pa, w_pb, w_o, ln2_w, w_up, w_down, ln_f_w):
    b, s, d = x.shape
    depth = w_in.shape[0]
    bias_c, bias_d = bias_tables(rel_table, s)
    xt = x.reshape(b * s, d)
    for l in range(depth):
        xt = _layer(xt, bias_c, bias_d, ln1_w[l], w_in[l], cmp_pe_k[l], cmp_pe_v[l], cmp_w1_k[l], cmp_w2_k[l],
                    cmp_w1_v[l], cmp_w2_v[l], conv_w[l], a_log[l], dt_bias[l], gdn_norm_w[l], w_pa[l], w_pb[l],
                    w_o[l], ln2_w[l], w_up[l], w_down[l], ln_f_w if l == depth - 1 else None, b, s)
    return xt.reshape(b, s, d)
```

```python
import functools
import math

import numpy as np
import jax
import jax.numpy as jnp
from jax import lax
from jax.experimental import pallas as pl
from jax.experimental.pallas import tpu as pltpu

F32 = jnp.float32
BF16 = jnp.bfloat16
HIGHEST = lax.Precision.HIGHEST

D_MODEL = 2048
DEPTH = 2
NSA_HEADS = 8
NSA_KV_GROUPS = 2
NSA_HPG = NSA_HEADS // NSA_KV_GROUPS
NSA_HEAD_DIM = 128
NSA_WIDTH = NSA_HEADS * NSA_HEAD_DIM
NSA_KV_WIDTH = NSA_KV_GROUPS * NSA_HEAD_DIM
CMP_BLOCK = 32
CMP_STRIDE = 16
CMP_HIDDEN = 256
SEL_BLOCK = 64
SEL_TOP_N = 16
SEL_FORCE = 1000.0
WINDOW = 512
GDN_HEADS = 8
GDN_HEAD_DIM = 128
GDN_WIDTH = GDN_HEADS * GDN_HEAD_DIM
CONV_WIDTH = 4
GDN_CHUNK = 64
REL_BUCKETS = 32
REL_MAX_DIST = 128
D_FF = 4 * D_MODEL
NORM_EPS = 1e-6

LANES = 128
VMEM_LIMIT = 56 * 1024 * 1024
TQ = 256
TK = 256
NEG_INF = float("-inf")
LOG2E = math.log2(math.e)

TILE_IN_PROJ = (1024, 1024)
TILE_MERGE = (1024, 512)
TILE_OUT_PROJ = 512
TILE_UP = (1024, 1024)
TILE_DOWN = (1024, 256)

R1_KS, R1_VS, R1_KW, R1_VW, R1_MA, R1_MB = 8, 10, 12, 14, 16, 32
R1_COLS = 48 * LANES
R2_Z, R2_KC, R2_VC, R2_GQ, R2_GK, R2_GV, R2_SMALL = 0, 8, 10, 12, 20, 28, 36
R2_COLS = 40 * LANES
SMALL_GATE, SMALL_A, SMALL_B = 0, 24, 32


def _t5_thresholds():
    exact = REL_BUCKETS // 2
    n = np.arange(exact, REL_MAX_DIST * 2, dtype=np.int64)
    nf = n.astype(np.float32)
    large = exact + (np.log(nf / np.float32(exact)) / np.float32(math.log(REL_MAX_DIST / exact))
                     * np.float32(REL_BUCKETS - exact)).astype(np.int32)
    large = np.minimum(large, REL_BUCKETS - 1)
    thr = []
    for b in range(exact + 1, REL_BUCKETS):
        thr.append(int(n[np.argmax(large >= b)]))
    return tuple(thr)


T5_THRESHOLDS = _t5_thresholds()


def _cparams(*sem):
    return pltpu.CompilerParams(dimension_semantics=sem, vmem_limit_bytes=VMEM_LIMIT)


def _rmsnorm_kernel(x_ref, w_ref, o_ref):
    x = x_ref[...]
    y = x * lax.rsqrt(jnp.mean(x * x, axis=-1, keepdims=True) + NORM_EPS)
    o_ref[...] = (y * w_ref[...]).astype(o_ref.dtype)


def rmsnorm(x, w, out_dtype, tm=512):
    t, d = x.shape
    return pl.pallas_call(
        _rmsnorm_kernel,
        grid=(t // tm,),
        in_specs=[pl.BlockSpec((tm, d), lambda i: (i, 0)), pl.BlockSpec((1, d), lambda i: (0, 0))],
        out_specs=pl.BlockSpec((tm, d), lambda i: (i, 0)),
        out_shape=jax.ShapeDtypeStruct((t, d), out_dtype),
        compiler_params=_cparams("parallel"),
        name="rmsnorm",
    )(x, w.reshape(1, d))


def _mm_resnorm_kernel(a_ref, w_ref, res_ref, norm_ref, o_ref, n_ref):
    acc = res_ref[...] + jnp.dot(a_ref[...], w_ref[...], preferred_element_type=F32)
    normed = acc * lax.rsqrt(jnp.mean(acc * acc, axis=-1, keepdims=True) + NORM_EPS) * norm_ref[...]
    n_ref[...] = normed.astype(n_ref.dtype)
    o_ref[...] = acc


def matmul_residual_norm(a, w, res, norm_w, norm_dtype, *, tm):
    m, kdim = a.shape
    n = w.shape[1]
    row = lambda width: pl.BlockSpec((tm, width), lambda i: (i, 0))
    return pl.pallas_call(
        _mm_resnorm_kernel,
        grid=(m // tm,),
        in_specs=[row(kdim), pl.BlockSpec((kdim, n), lambda i: (0, 0)), row(n), pl.BlockSpec((1, n), lambda i: (0, 0))],
        out_specs=[row(n), row(n)],
        out_shape=[jax.ShapeDtypeStruct((m, n), F32), jax.ShapeDtypeStruct((m, n), norm_dtype)],
        compiler_params=_cparams("parallel"),
        name="matmul_residual_norm",
    )(a, w, res, norm_w.reshape(1, n))


def _column_tiles(w, tn):
    k, n = w.shape
    return w.reshape(k, n // tn, tn).transpose(1, 0, 2)


def _row_tile_copy(src_hbm, buf, sem, tile, slot):
    tm = buf.shape[1]
    return pltpu.make_async_copy(src_hbm.at[pl.ds(tile * tm, tm), :], buf.at[slot], sem.at[slot])


def _prefetch_row_tile(src_hbm, buf, sem):
    i = pl.program_id(0)
    slot = i % 2

    @pl.when(pl.program_id(1) == 0)
    def _():
        @pl.when(i == 0)
        def _():
            _row_tile_copy(src_hbm, buf, sem, 0, 0).start()

        _row_tile_copy(src_hbm, buf, sem, i, slot).wait()

        @pl.when(i + 1 < pl.num_programs(0))
        def _():
            _row_tile_copy(src_hbm, buf, sem, i + 1, 1 - slot).start()

    return slot


def _mm_rows_kernel(*refs, epilogue):
    a_hbm, w_ref = refs[0], refs[1]
    res_ref = refs[2] if epilogue == "residual" else None
    o_ref, a_buf, sem = refs[-3], refs[-2], refs[-1]
    slot = _prefetch_row_tile(a_hbm, a_buf, sem)
    acc = jnp.dot(a_buf[slot], w_ref[...], preferred_element_type=F32)
    if epilogue == "relu2":
        acc = jnp.square(jnp.maximum(acc, 0.0))
    elif epilogue == "residual":
        acc = res_ref[...] + acc
    o_ref[...] = acc.astype(o_ref.dtype)


def matmul_rows(a, w, out_dtype, *, tm, tn, epilogue, res=None):
    m, kdim = a.shape
    n = w.shape[1]
    in_specs = [pl.BlockSpec(memory_space=pl.ANY), pl.BlockSpec((None, kdim, tn), lambda i, j: (j, 0, 0))]
    args = [a, _column_tiles(w, tn)]
    if epilogue == "residual":
        in_specs.append(pl.BlockSpec((tm, tn), lambda i, j: (i, j)))
        args.append(res)
    return pl.pallas_call(
        functools.partial(_mm_rows_kernel, epilogue=epilogue),
        grid=(m // tm, n // tn),
        in_specs=in_specs,
        out_specs=pl.BlockSpec((tm, tn), lambda i, j: (i, j)),
        out_shape=jax.ShapeDtypeStruct((m, n), out_dtype),
        scratch_shapes=[pltpu.VMEM((2, tm, kdim), a.dtype), pltpu.SemaphoreType.DMA((2,))],
        compiler_params=_cparams("arbitrary", "arbitrary"),
        name="matmul_rows_" + epilogue,
    )(*args)


def _inproj_kernel(x_hbm, lnw_ref, w_ref, cs_ref, r1_ref, r2_ref, h_ref, x_buf, sem, *, n1):
    j = pl.program_id(1)
    slot = _prefetch_row_tile(x_hbm, x_buf, sem)

    @pl.when(j == 0)
    def _():
        x = x_buf[slot]
        y = x * lax.rsqrt(jnp.mean(x * x, axis=-1, keepdims=True) + NORM_EPS)
        h_ref[...] = (y * lnw_ref[...]).astype(BF16)

    acc = jnp.dot(h_ref[...], w_ref[...], preferred_element_type=F32)

    @pl.when(j < n1)
    def _():
        r1_ref[...] = (acc * cs_ref[...]).astype(r1_ref.dtype)

    @pl.when(j >= n1)
    def _():
        r2_ref[...] = acc


def input_projection(x, ln_w, w_all, col_scale, *, tm=TILE_IN_PROJ[0], tn=TILE_IN_PROJ[1]):
    t, d = x.shape
    n1 = R1_COLS // tn
    n2 = R2_COLS // tn
    assert w_all.shape == (d, R1_COLS + R2_COLS) and R1_COLS % tn == 0 and R2_COLS % tn == 0
    return pl.pallas_call(
        functools.partial(_inproj_kernel, n1=n1),
        grid=(t // tm, n1 + n2),
        in_specs=[pl.BlockSpec(memory_space=pl.ANY),
                  pl.BlockSpec((1, d), lambda i, j: (0, 0)),
                  pl.BlockSpec((None, d, tn), lambda i, j: (j, 0, 0)),
                  pl.BlockSpec((1, tn), lambda i, j: (0, jnp.minimum(j, n1 - 1)))],
        out_specs=[pl.BlockSpec((tm, tn), lambda i, j: (i, jnp.minimum(j, n1 - 1))),
                   pl.BlockSpec((tm, tn), lambda i, j: (i, jnp.maximum(j - n1, 0)))],
        out_shape=[jax.ShapeDtypeStruct((t, R1_COLS), BF16), jax.ShapeDtypeStruct((t, R2_COLS), F32)],
        scratch_shapes=[pltpu.VMEM((tm, d), BF16), pltpu.VMEM((2, tm, d), x.dtype), pltpu.SemaphoreType.DMA((2,))],
        compiler_params=_cparams("arbitrary", "arbitrary"),
        name="input_projection",
    )(x, ln_w.reshape(1, d), _column_tiles(w_all, tn), col_scale)


def _merge_kernel(oa_ref, ob_ref, wa_ref, wb_ref, ma_ref, mb_ref, o_ref):
    pa = jnp.dot(oa_ref[...], wa_ref[...], preferred_element_type=F32)
    pb = jnp.dot(ob_ref[...], wb_ref[...], preferred_element_type=F32)
    ga = jax.nn.sigmoid(ma_ref[...].astype(F32))
    gb = jax.nn.sigmoid(mb_ref[...].astype(F32))
    o_ref[...] = (ga * pa + gb * pb).astype(o_ref.dtype)


def merge_branches(o_a, o_b, w_pa, w_pb, r1, *, tm=TILE_MERGE[0], tn=TILE_MERGE[1]):
    t, ka = o_a.shape
    kb = o_b.shape[1]
    n = w_pa.shape[1]
    ma0 = R1_MA * LANES // tn
    mb0 = R1_MB * LANES // tn
    return pl.pallas_call(
        _merge_kernel,
        grid=(t // tm, n // tn),
        in_specs=[
            pl.BlockSpec((tm, ka), lambda i, j: (i, 0)),
            pl.BlockSpec((tm, kb), lambda i, j: (i, 0)),
            pl.BlockSpec((ka, tn), lambda i, j: (0, j)),
            pl.BlockSpec((kb, tn), lambda i, j: (0, j)),
            pl.BlockSpec((tm, tn), lambda i, j: (i, ma0 + j)),
            pl.BlockSpec((tm, tn), lambda i, j: (i, mb0 + j)),
        ],
        out_specs=pl.BlockSpec((tm, tn), lambda i, j: (i, j)),
        out_shape=jax.ShapeDtypeStruct((t, n), BF16),
        compiler_params=_cparams("parallel", "parallel"),
        name="merge_branches",
    )(o_a, o_b, w_pa, w_pb, r1, r1)


def _t5_lookup(dist, tab_ref, h):
    n = jnp.maximum(dist, 0)
    exact = REL_BUCKETS // 2
    large = jnp.full(n.shape, exact, jnp.int32)
    for thr in T5_THRESHOLDS:
        large = large + (n >= thr).astype(jnp.int32)
    bucket = jnp.where(n < exact, n, large)
    out = jnp.zeros(n.shape, F32)
    for b in range(REL_BUCKETS):
        out = jnp.where(bucket == b, tab_ref[b, h] * LOG2E, out)
    return out


def _bias_kernel(tab_ref, bc_ref, bd_ref):
    g = pl.program_id(0)
    i = pl.program_id(1)
    ncr = bc_ref.shape[0]
    blk = lax.broadcasted_iota(jnp.int32, (ncr, TQ), 0)
    qry_c = lax.broadcasted_iota(jnp.int32, (ncr, TQ), 1)
    for j in range(NSA_HPG):
        h = g * NSA_HPG + j
        bc_ref[:, j * TQ:(j + 1) * TQ] = _t5_lookup(i * TQ + qry_c - (blk * CMP_STRIDE + CMP_BLOCK - 1), tab_ref, h)

    @pl.when(i == 0)
    def _():
        key = lax.broadcasted_iota(jnp.int32, (TK, TQ), 0)
        qry = lax.broadcasted_iota(jnp.int32, (TK, TQ), 1)
        for v in range(3):
            for j in range(NSA_HPG):
                bd_ref[v, :, j * TQ:(j + 1) * TQ] = _t5_lookup(v * TK + qry - key, tab_ref, g * NSA_HPG + j)


def bias_tables(rel_table, s):
    nq = s // TQ
    cols = NSA_HPG * TQ
    return pl.pallas_call(
        _bias_kernel,
        grid=(NSA_KV_GROUPS, nq),
        in_specs=[pl.BlockSpec(memory_space=pltpu.SMEM)],
        out_specs=[pl.BlockSpec((None, None, s // CMP_STRIDE, cols), lambda g, i: (g, i, 0, 0)),
                   pl.BlockSpec((None, 3, TK, cols), lambda g, i: (g, 0, 0, 0))],
        out_shape=[jax.ShapeDtypeStruct((NSA_KV_GROUPS, nq, s // CMP_STRIDE, cols), F32),
                   jax.ShapeDtypeStruct((NSA_KV_GROUPS, 3, TK, cols), F32)],
        compiler_params=_cparams("arbitrary", "arbitrary"),
        name="t5_bias_tables",
    )(rel_table)


def _gelu_tanh(x):
    return 0.5 * x * (1.0 + jnp.tanh(math.sqrt(2.0 / math.pi) * (x + 0.044715 * (x * x * x))))


def _compress_kernel(x_ref, pe_ref, w1_ref, w2k_ref, w2vt_ref, o_ref, *, nblk):
    pe = pe_ref[...]
    acc_a = jnp.zeros((nblk, CMP_HIDDEN), F32)
    acc_b = jnp.zeros((nblk, CMP_HIDDEN), F32)
    for r in range(CMP_STRIDE):
        xr = x_ref[pl.ds(r, nblk, stride=CMP_STRIDE), :]
        xa = (xr + pe[r:r + 1, :]).astype(BF16)
        xb = (xr + pe[CMP_STRIDE + r:CMP_STRIDE + r + 1, :]).astype(BF16)
        acc_a = acc_a + jnp.dot(xa, w1_ref[r * NSA_HEAD_DIM:(r + 1) * NSA_HEAD_DIM, :], preferred_element_type=F32)
        acc_b = acc_b + jnp.dot(xb, w1_ref[(CMP_STRIDE + r) * NSA_HEAD_DIM:(CMP_STRIDE + r + 1) * NSA_HEAD_DIM, :],
                                preferred_element_type=F32)
    hidden = acc_a + jnp.concatenate([acc_b[1:], acc_b[:1]], axis=0)
    act = _gelu_tanh(hidden).astype(BF16)

    @pl.when(pl.program_id(2) == 0)
    def _():
        o_ref[...] = jnp.dot(act, w2k_ref[...], preferred_element_type=F32).astype(o_ref.dtype)

    @pl.when(pl.program_id(2) == 1)
    def _():
        o_ref[...] = lax.dot_general(w2vt_ref[...], act, (((1,), (1,)), ((), ())),
                                     preferred_element_type=F32).astype(o_ref.dtype)


def nsa_compress(r2, pe, w1, w2k, w2vt):
    b, s, _ = r2.shape
    nblk = s // CMP_STRIDE
    assert CMP_BLOCK == 2 * CMP_STRIDE and nblk == NSA_HEAD_DIM
    return pl.pallas_call(
        functools.partial(_compress_kernel, nblk=nblk),
        grid=(b, NSA_KV_GROUPS, 2),
        in_specs=[
            pl.BlockSpec((None, s, LANES), lambda bi, g, c: (bi, 0, R2_KC + 2 * c + g)),
            pl.BlockSpec((None, CMP_BLOCK, NSA_HEAD_DIM), lambda bi, g, c: (c, 0, 0)),
            pl.BlockSpec((None, CMP_BLOCK * NSA_HEAD_DIM, CMP_HIDDEN), lambda bi, g, c: (c, 0, 0)),
            pl.BlockSpec((CMP_HIDDEN, NSA_HEAD_DIM), lambda bi, g, c: (0, 0)),
            pl.BlockSpec((NSA_HEAD_DIM, CMP_HIDDEN), lambda bi, g, c: (0, 0)),
        ],
        out_specs=pl.BlockSpec((None, None, None, nblk, NSA_HEAD_DIM), lambda bi, g, c: (bi, g, c, 0, 0)),
        out_shape=jax.ShapeDtypeStruct((b, NSA_KV_GROUPS, 2, nblk, NSA_HEAD_DIM), BF16),
        compiler_params=_cparams("parallel", "parallel", "arbitrary"),
        name="nsa_compress",
    )(r2, pe, w1, w2k, w2vt)


def _nsa_attn_kernel(q_ref, ks_ref, kw_ref, vst_ref, vwt_ref, kcvc_ref, gate_ref, biasc_ref, biasd_ref,
                     covert_ref, expandt_ref, o_ref, sel_ref, m_ref, l_ref, acc_ref, res_ref, *, n_sel, top_n):
    g = pl.program_id(1)
    i = pl.program_id(2)
    cols = NSA_HPG * TQ
    nt = (((1,), (1,)), ((), ()))

    qf = q_ref[...]
    qs = jnp.concatenate([qf[:, j * NSA_HEAD_DIM:(j + 1) * NSA_HEAD_DIM] for j in range(NSA_HPG)], axis=0)
    key = lax.broadcasted_iota(jnp.int32, (TK, cols), 0)
    t_loc = lax.broadcasted_iota(jnp.int32, (TK, cols), 1) % TQ

    def gate_row(c):
        return jnp.concatenate(
            [jax.nn.sigmoid(gate_ref[pl.ds((g * NSA_HPG + j) * 3 + c, 1), :]) for j in range(NSA_HPG)], axis=1)

    def lanes4(x):
        return jnp.concatenate([x] * NSA_HPG, axis=1)

    s = lax.dot_general(kcvc_ref[0], qs, nt, preferred_element_type=F32) + biasc_ref[...]
    cmp_end = lax.broadcasted_iota(jnp.int32, s.shape, 0) * CMP_STRIDE + (CMP_BLOCK - 1)
    s = jnp.where(cmp_end <= i * TQ + lax.broadcasted_iota(jnp.int32, s.shape, 1) % TQ, s, NEG_INF)
    mx = jnp.max(s, axis=0, keepdims=True)
    mx = jnp.where(mx == NEG_INF, 0.0, mx)
    e = jnp.exp2(s - mx)
    p = e / jnp.maximum(jnp.sum(e, axis=0, keepdims=True), 1e-30)
    o_c = jnp.dot(kcvc_ref[1], p.astype(BF16), preferred_element_type=F32)
    res_ref[...] = gate_row(0) * o_c

    psum = p[:, 0:TQ]
    for j in range(1, NSA_HPG):
        psum = psum + p[:, j * TQ:(j + 1) * TQ]
    imp = jnp.dot(covert_ref[...], psum, preferred_element_type=F32, precision=HIGHEST)[:n_sel]
    blk = lax.broadcasted_iota(jnp.int32, (n_sel, TQ), 0)
    cur = (i * TQ + lax.broadcasted_iota(jnp.int32, (n_sel, TQ), 1)) // SEL_BLOCK
    causal_blk = blk <= cur
    forced = (blk == 0) | (blk == cur) | (blk == cur - 1)
    score = jnp.where(causal_blk, imp + jnp.where(forced, SEL_FORCE, 0.0), NEG_INF)
    rank = jnp.zeros((n_sel, TQ), jnp.int32)
    for mp in range(n_sel):
        other = score[mp:mp + 1, :]
        beats = (other > score) | ((other == score) & (blk > mp))
        rank = rank + beats.astype(jnp.int32)
    sel = jnp.where((rank < top_n) & causal_blk, 1.0, 0.0)
    sel = jnp.concatenate([sel, jnp.zeros((LANES - n_sel, TQ), F32)], axis=0).astype(BF16)
    sel_ref[...] = jnp.where(jnp.dot(expandt_ref[...], sel, preferred_element_type=F32) > 0.5, 0.0, NEG_INF)

    def reset():
        m_ref[...] = jnp.full(m_ref.shape, NEG_INF, F32)
        l_ref[...] = jnp.zeros(l_ref.shape, F32)
        acc_ref[...] = jnp.zeros(acc_ref.shape, F32)

    far_bias = biasd_ref[2, 0:1, :]

    def attend(items):
        def scores(item):
            return lax.dot_general(item[1], qs, nt, preferred_element_type=F32)

        def softmax_step(item, s):
            st, _, _, variant, mask_add = item
            if variant < 2:
                s = s + biasd_ref[variant]
            if mask_add is not None:
                s = s + mask_add
            smax = jnp.max(s, axis=0, keepdims=True)
            if variant == 2:
                smax = smax + far_bias
            m_prev = m_ref[st]
            m_new = jnp.maximum(m_prev, smax)
            m_safe = jnp.where(m_new == NEG_INF, 0.0, m_new)
            alpha = jnp.exp2(m_prev - m_safe)
            p = jnp.exp2(s + ((far_bias - m_safe) if variant == 2 else -m_safe))
            l_ref[st] = alpha * l_ref[st] + jnp.sum(p, axis=0, keepdims=True)
            m_ref[st] = m_new
            return alpha, p.astype(BF16)

        def value_step(item, alpha, pb):
            st, vt = item[0], item[2]
            acc_ref[st] = alpha * acc_ref[st] + jnp.dot(vt, pb, preferred_element_type=F32)

        s_next = scores(items[0])
        pending = None
        for n, item in enumerate(items):
            s_cur = s_next
            if n + 1 < len(items):
                s_next = scores(items[n + 1])
            alpha, pb = softmax_step(item, s_cur)
            if pending is not None:
                value_step(*pending)
            pending = (item, alpha, pb)
        value_step(*pending)

    def tile(ref, k):
        return ref[pl.ds(pl.multiple_of(k * TK, TK), TK), :]

    causal_add = jnp.where(key <= t_loc, 0.0, NEG_INF)
    edge_add = jnp.where(key > t_loc, 0.0, NEG_INF)
    n_win = WINDOW // TK
    SEL, WIN, SEL2 = 0, 1, 2
    reset()

    def near_items(depth):
        items = []
        for d in range(depth + 1):
            k = i - d
            sel_mask = lanes4(tile(sel_ref, k))
            items.append((SEL, tile(ks_ref, k), vst_ref[k], min(d, 2), sel_mask + causal_add if d == 0 else sel_mask))
            win_mask = causal_add if d == 0 else (edge_add if d == n_win else None)
            items.append((WIN, tile(kw_ref, k), vwt_ref[k], min(d, 2), win_mask))
        return items

    for depth in range(n_win + 1):
        @pl.when((i == depth) if depth < n_win else (i >= depth))
        def _(depth=depth):
            attend(near_items(depth))

    n_far = jnp.maximum(i - n_win, 0)

    def far_item(stream, k):
        return (stream, tile(ks_ref, k), vst_ref[k], 2, lanes4(tile(sel_ref, k)))

    def far_pair(kk, carry):
        attend([far_item(SEL, 2 * kk), far_item(SEL2, 2 * kk + 1)])
        return carry

    lax.fori_loop(0, n_far // 2, far_pair, 0)

    @pl.when(n_far % 2 == 1)
    def _():
        attend([far_item(SEL, n_far - 1)])

    m_a, m_b = m_ref[SEL], m_ref[SEL2]
    m_ab = jnp.maximum(m_a, m_b)
    m_ab = jnp.where(m_ab == NEG_INF, 0.0, m_ab)
    w_a = jnp.exp2(m_a - m_ab)
    w_b = jnp.exp2(m_b - m_ab)
    l_sel = w_a * l_ref[SEL] + w_b * l_ref[SEL2]
    o_sel = (w_a * acc_ref[SEL] + w_b * acc_ref[SEL2]) * (1.0 / jnp.maximum(l_sel, 1e-30))
    o_win = acc_ref[WIN] * (1.0 / jnp.maximum(l_ref[WIN], 1e-30))
    res = res_ref[...] + gate_row(1) * o_sel + gate_row(2) * o_win
    for j in range(NSA_HPG):
        o_ref[:, j * NSA_HEAD_DIM:(j + 1) * NSA_HEAD_DIM] = res[:, j * TQ:(j + 1) * TQ].T.astype(o_ref.dtype)


def nsa_attention(r1, r2, kcvc, bias_c, bias_d):
    b, s, _ = r1.shape
    n_cmp_rows = s // CMP_STRIDE
    n_sel = s // SEL_BLOCK
    assert n_cmp_rows % LANES == 0 and n_sel <= LANES and n_sel % 8 == 0 and WINDOW % TK == 0 and TQ == TK
    assert 2 * TK - (TQ - 1) >= REL_MAX_DIST
    top_n = min(SEL_TOP_N, n_sel)
    nq = s // TQ
    nk = s // TK
    gw = NSA_HPG * NSA_HEAD_DIM
    cols = NSA_HPG * TQ
    dh = NSA_HEAD_DIM

    n_cmp = n_cmp_rows - CMP_BLOCK // CMP_STRIDE + 1
    c_start = np.arange(n_cmp_rows) * CMP_STRIDE
    s_start = np.arange(LANES) * SEL_BLOCK
    cover_t = ((c_start[None, :] <= s_start[:, None] + SEL_BLOCK - 1) & (c_start[None, :] + CMP_BLOCK - 1 >= s_start[:, None])
               & (np.arange(n_cmp_rows)[None, :] < n_cmp) & (np.arange(LANES)[:, None] < n_sel))
    cover_t = jnp.asarray(cover_t.astype(np.float32))
    expand_t = jnp.asarray(((np.arange(s)[:, None] // SEL_BLOCK) == np.arange(LANES)[None, :]).astype(np.float32), BF16)

    v0 = R1_VS * LANES
    v1 = R1_VW * LANES
    vt = jnp.stack([r1[:, :, v0:v0 + NSA_KV_WIDTH], r1[:, :, v1:v1 + NSA_KV_WIDTH]], axis=1)
    vt = vt.reshape(b, 2, nk, TK, NSA_KV_GROUPS, dh).transpose(0, 1, 4, 2, 5, 3)
    g0 = R2_SMALL * LANES + SMALL_GATE
    gate_t = jnp.swapaxes(r2[:, :, g0:g0 + 3 * NSA_HEADS], 1, 2)

    k_spec = lambda blk: pl.BlockSpec((None, s, LANES), lambda bi, g, i, blk=blk: (bi, 0, blk + g))
    vt_spec = lambda w: pl.BlockSpec((None, None, None, nk, dh, TK), lambda bi, g, i, w=w: (bi, w, g, 0, 0, 0))
    return pl.pallas_call(
        functools.partial(_nsa_attn_kernel, n_sel=n_sel, top_n=top_n),
        grid=(b, NSA_KV_GROUPS, nq),
        in_specs=[
            pl.BlockSpec((None, TQ, gw), lambda bi, g, i: (bi, i, g)),
            k_spec(R1_KS), k_spec(R1_KW), vt_spec(0), vt_spec(1),
            pl.BlockSpec((None, None, 2, n_cmp_rows, dh), lambda bi, g, i: (bi, g, 0, 0, 0)),
            pl.BlockSpec((None, 3 * NSA_HEADS, TQ), lambda bi, g, i: (bi, 0, i)),
            pl.BlockSpec((None, None, n_cmp_rows, cols), lambda bi, g, i: (g, i, 0, 0)),
            pl.BlockSpec((None, 3, TK, cols), lambda bi, g, i: (g, 0, 0, 0)),
            pl.BlockSpec((LANES, n_cmp_rows), lambda bi, g, i: (0, 0)),
            pl.BlockSpec((s, LANES), lambda bi, g, i: (0, 0)),
        ],
        out_specs=pl.BlockSpec((None, TQ, gw), lambda bi, g, i: (bi, i, g)),
        out_shape=jax.ShapeDtypeStruct((b, s, NSA_WIDTH), BF16),
        scratch_shapes=[
            pltpu.VMEM((s, TQ), F32),
            pltpu.VMEM((3, 1, cols), F32),
            pltpu.VMEM((3, 1, cols), F32),
            pltpu.VMEM((3, dh, cols), F32),
            pltpu.VMEM((dh, cols), F32),
        ],
        compiler_params=_cparams("parallel", "parallel", "arbitrary"),
        name="nsa_attention",
    )(r1, r1, r1, vt, vt, kcvc, gate_t, bias_c, bias_d, cover_t, expand_t)


GDN_PAIR = 2 * GDN_CHUNK
GDN_GROUP = 8
GDN_ROWS = 512


def _gdn_prep_kernel(q_ref, k_ref, v_ref, smt_ref, alog_ref, dtb_ref, cwq_ref, cwk_ref, cwv_ref,
                     w_out, qg_out, u_out, qk_out, kdt_out, egl_out,
                     kb_s, kk_s, qq_s, vk_s, kd_s, gcol_s, grow_s, xp_s):
    h = pl.program_id(1)
    s = q_ref.shape[0]
    c = GDN_CHUNK
    pr = GDN_PAIR
    dk = GDN_HEAD_DIM
    nt = (((1,), (1,)), ((), ()))
    tn = (((0,), (0,)), ((), ()))

    xg = smt_ref[0:GDN_HEADS, :] + dtb_ref[...]
    g8 = -jnp.exp(alog_ref[...]) * (jnp.maximum(xg, 0.0) + jnp.log1p(jnp.exp(-jnp.abs(xg))))
    beta8 = jax.nn.sigmoid(smt_ref[GDN_HEADS:2 * GDN_HEADS, :])
    pos = lax.broadcasted_iota(jnp.int32, g8.shape, 1) % c
    gcum8 = g8
    sh = 1
    while sh < c:
        gcum8 = gcum8 + jnp.where(pos >= sh, pltpu.roll(gcum8, sh, axis=1), 0.0)
        sh *= 2
    for p in range(s // pr):
        grow_s[p] = gcum8[:, p * pr:(p + 1) * pr]
    sel_k = lax.broadcasted_iota(jnp.int32, (2 * GDN_HEADS, 2 * dk), 0)
    sel_l = lax.broadcasted_iota(jnp.int32, (2 * GDN_HEADS, 2 * dk), 1)
    pick = jnp.where(((sel_k == h) & (sel_l < dk)) | ((sel_k == GDN_HEADS + h) & (sel_l >= dk)), 1.0, 0.0)
    both = jnp.concatenate([gcum8, beta8], axis=0)
    hi = both.astype(BF16)
    rest = both - hi.astype(F32)
    mid = rest.astype(BF16)
    lo = (rest - mid.astype(F32)).astype(BF16)
    col = lax.dot_general(jnp.concatenate([hi, mid, lo], axis=0), jnp.concatenate([pick] * 3, axis=0).astype(BF16),
                          tn, preferred_element_type=F32)
    gcol = col[:, :dk]
    bcol = col[:, dk:]
    gcol_s[...] = gcol

    hist = 8
    xp_s[0:hist, :] = jnp.zeros((hist, dk), F32)

    def conv_silu(x_ref, w_ref):
        x = x_ref[...]
        w = w_ref[...]
        xp_s[hist:hist + s, :] = x
        y = x * w[CONV_WIDTH - 1:CONV_WIDTH, :]
        for sft in range(1, CONV_WIDTH):
            y = y + xp_s[hist - sft:hist - sft + s, :] * w[CONV_WIDTH - 1 - sft:CONV_WIDTH - sft, :]
        return y * jax.nn.sigmoid(y)

    def l2n(x):
        return x * lax.rsqrt(jnp.sum(x * x, axis=-1, keepdims=True) + NORM_EPS)

    q = l2n(conv_silu(q_ref, cwq_ref)) * (dk ** -0.5)
    k = l2n(conv_silu(k_ref, cwk_ref))
    v = conv_silu(v_ref, cwv_ref)
    eg = jnp.exp(gcol)
    kb = k * bcol
    qg_out[...] = (q * eg).astype(BF16)
    kk_s[...] = k.astype(BF16)
    kb_s[...] = kb.astype(BF16)
    qq_s[...] = q.astype(BF16)
    vk_s[:, :dk] = (v * bcol).astype(BF16)
    vk_s[:, dk:] = (kb * eg).astype(BF16)
    g3 = gcol.reshape(s // c, c, dk)
    g_last = g3[:, c - 1:c, :]
    kd_s[...] = (k * jnp.exp(jnp.broadcast_to(g_last, g3.shape).reshape(s, dk) - gcol)).astype(BF16)
    egl_out[...] = jnp.broadcast_to(jnp.exp(g_last), egl_out.shape)

    r_i = lax.broadcasted_iota(jnp.int32, (pr, pr), 0)
    l_i = lax.broadcasted_iota(jnp.int32, (pr, pr), 1)
    tri = ((r_i // c) == (l_i // c)) & (r_i >= l_i)
    eye = r_i == l_i
    eye_f = jnp.where(eye, 1.0, 0.0)
    eye_b = eye_f.astype(BF16)

    def pair_group(ps):
        rows = [slice(p * pr, (p + 1) * pr) for p in ps]
        n = len(ps)
        x = [lax.dot_general(jnp.concatenate([kb_s[r, :], qq_s[r, :]], axis=0), kk_s[r, :], nt,
                             preferred_element_type=F32) for r in rows]
        decay = []
        for p, r in zip(ps, rows):
            diff = gcol_s[r, :] - grow_s[p, pl.ds(h, 1), :]
            decay.append(jnp.where(tri, jnp.exp(jnp.where(tri, diff, 0.0)), 0.0))
        for a in range(n):
            qk_out[rows[a], :] = (x[a][pr:] * decay[a]).astype(BF16)
            kdt_out[ps[a]] = lax.dot_general(eye_b, kd_s[rows[a], :], nt, preferred_element_type=F32).astype(BF16)
        def split(m):
            hi = m.astype(BF16)
            return hi, (m - hi.astype(F32)).astype(BF16)

        def mm3(lhs, rhs):
            return (jnp.dot(lhs[0], rhs[0], preferred_element_type=F32)
                    + jnp.dot(lhs[1], rhs[0], preferred_element_type=F32)
                    + jnp.dot(lhs[0], rhs[1], preferred_element_type=F32))

        def stack(u, v):
            return (jnp.concatenate([u[0], v[0]], axis=0), jnp.concatenate([u[1], v[1]], axis=0))

        qm = [jnp.where(eye, 0.0, -(x[a][:pr] * decay[a])) for a in range(n)]
        tinv = [eye_f + qm[a] for a in range(n)]
        qs = [split(qm[a]) for a in range(n)]
        qm = [mm3(qs[a], qs[a]) for a in range(n)]
        steps = int(math.log2(c)) - 1
        for it in range(steps):
            qs = [split(qm[a]) for a in range(n)]
            ts = [split(tinv[a]) for a in range(n)]
            if it < steps - 1:
                r = [mm3(stack(qs[a], ts[a]), qs[a]) for a in range(n)]
                qm = [r[a][:pr] for a in range(n)]
                tinv = [tinv[a] + r[a][pr:] for a in range(n)]
            else:
                tinv = [tinv[a] + mm3(ts[a], qs[a]) for a in range(n)]
        for a in range(n):
            uw = jnp.dot(tinv[a].astype(BF16), vk_s[rows[a], :], preferred_element_type=F32)
            u_out[rows[a], :] = uw[:, :dk]
            w_out[rows[a], :] = uw[:, dk:].astype(BF16)

    npair = s // pr
    for p0 in range(0, npair, GDN_GROUP):
        pair_group(list(range(p0, min(p0 + GDN_GROUP, npair))))


def _gdn_scan_kernel(w_ref, qg_ref, u_ref, qk_ref, kdt_ref, egl_ref, z_ref, nw_ref, o_ref, state_ref, os_ref):
    c = GDN_CHUNK
    pr = GDN_PAIR
    dk = GDN_HEAD_DIM
    nh = w_ref.shape[0]
    rb = w_ref.shape[1]

    @pl.when(pl.program_id(1) == 0)
    def _():
        state_ref[...] = jnp.zeros(state_ref.shape, F32)

    zeros = jnp.zeros((c, dk), BF16)

    def pair(p, carry):
        r0 = pl.multiple_of(p * pr, pr)
        for half in range(2):
            rows = pl.ds(r0 + half * c, c)
            state = [state_ref[h] for h in range(nh)]
            r1 = [jnp.dot(jnp.concatenate([w_ref[h, rows, :], qg_ref[h, rows, :]], axis=0), state[h].astype(BF16),
                          preferred_element_type=F32) for h in range(nh)]
            vnb = [(u_ref[h, rows, :] - r1[h][:c]).astype(BF16) for h in range(nh)]
            rhs = [jnp.concatenate([vnb[h], zeros] if half == 0 else [zeros, vnb[h]], axis=0) for h in range(nh)]
            r2 = [jnp.dot(jnp.concatenate([qk_ref[h, rows, :], kdt_ref[h, p]], axis=0), rhs[h],
                          preferred_element_type=F32) for h in range(nh)]
            for h in range(nh):
                os_ref[rows, h * dk:(h + 1) * dk] = r1[h][c:] + r2[h][:c]
                state_ref[h] = state[h] * egl_ref[h, 2 * p + half, 0:1, :] + r2[h][c:]
        return carry

    lax.fori_loop(0, rb // pr, pair, 0)

    for h in range(nh):
        o = os_ref[:, h * dk:(h + 1) * dk]
        o = o * lax.rsqrt(jnp.mean(o * o, axis=-1, keepdims=True) + NORM_EPS) * nw_ref[...]
        z = z_ref[:, h * dk:(h + 1) * dk]
        o_ref[:, h * dk:(h + 1) * dk] = (o * (z * jax.nn.sigmoid(z))).astype(o_ref.dtype)


def gated_deltanet(r2, conv_w, a_log, dt_bias, norm_w):
    b, s, _ = r2.shape
    dk = GDN_HEAD_DIM
    nh = GDN_HEADS
    npair = s // GDN_PAIR
    nchunk = s // GDN_CHUNK
    rb = GDN_ROWS
    assert s % GDN_PAIR == 0 and s % rb == 0 and (R2_Z * LANES) % GDN_WIDTH == 0
    a0 = R2_SMALL * LANES + SMALL_A
    smt = jnp.swapaxes(r2[:, :, a0:a0 + 2 * nh], 1, 2)
    col = lambda blk: pl.BlockSpec((None, s, dk), lambda bi, h, blk=blk: (bi, 0, blk + h))
    cw = lambda blk: pl.BlockSpec((CONV_WIDTH, dk), lambda bi, h, blk=blk: (0, blk + h))
    vec = pl.BlockSpec((nh, 1), lambda bi, h: (0, 0))
    per_head = lambda *shape: pl.BlockSpec((None, None) + shape, lambda bi, h: (bi, h) + (0,) * len(shape))
    w, qg, u, qk, kdt, egl = pl.pallas_call(
        _gdn_prep_kernel,
        grid=(b, nh),
        in_specs=[col(R2_GQ), col(R2_GK), col(R2_GV),
                  pl.BlockSpec((None, 2 * nh, s), lambda bi, h: (bi, 0, 0)), vec, vec,
                  cw(0), cw(nh), cw(2 * nh)],
        out_specs=[per_head(s, dk), per_head(s, dk), per_head(s, dk), per_head(s, dk),
                   per_head(npair, GDN_PAIR, GDN_PAIR), per_head(nchunk, 8, dk)],
        out_shape=[jax.ShapeDtypeStruct((b, nh, s, dk), BF16), jax.ShapeDtypeStruct((b, nh, s, dk), BF16),
                   jax.ShapeDtypeStruct((b, nh, s, dk), F32), jax.ShapeDtypeStruct((b, nh, s, dk), BF16),
                   jax.ShapeDtypeStruct((b, nh, npair, GDN_PAIR, GDN_PAIR), BF16),
                   jax.ShapeDtypeStruct((b, nh, nchunk, 8, dk), F32)],
        scratch_shapes=[pltpu.VMEM((s, dk), BF16), pltpu.VMEM((s, dk), BF16), pltpu.VMEM((s, dk), BF16),
                        pltpu.VMEM((s, 2 * dk), BF16), pltpu.VMEM((s, dk), BF16), pltpu.VMEM((s, dk), F32),
                        pltpu.VMEM((npair, nh, GDN_PAIR), F32), pltpu.VMEM((s + 8, dk), F32)],
        compiler_params=_cparams("parallel", "parallel"),
        name="gdn_prep",
    )(r2, r2, r2, smt, a_log.reshape(nh, 1), dt_bias.reshape(nh, 1), conv_w, conv_w, conv_w)

    seq = lambda *shape: pl.BlockSpec((None, nh) + shape, lambda bi, j: (bi, 0, j) + (0,) * (len(shape) - 1))
    return pl.pallas_call(
        _gdn_scan_kernel,
        grid=(b, s // rb),
        in_specs=[seq(rb, dk), seq(rb, dk), seq(rb, dk), seq(rb, dk),
                  seq(rb // GDN_PAIR, GDN_PAIR, GDN_PAIR), seq(rb // GDN_CHUNK, 8, dk),
                  pl.BlockSpec((None, rb, GDN_WIDTH), lambda bi, j: (bi, j, R2_Z * LANES // GDN_WIDTH)),
                  pl.BlockSpec((1, dk), lambda bi, j: (0, 0))],
        out_specs=pl.BlockSpec((None, rb, GDN_WIDTH), lambda bi, j: (bi, j, 0)),
        out_shape=jax.ShapeDtypeStruct((b, s, GDN_WIDTH), BF16),
        scratch_shapes=[pltpu.VMEM((nh, dk, dk), F32), pltpu.VMEM((rb, GDN_WIDTH), F32)],
        compiler_params=_cparams("parallel", "arbitrary"),
        name="gdn_scan",
    )(w, qg, u, qk, kdt, egl, r2, norm_w.reshape(1, dk))


def _split_w_in(w_in):
    o_q = 0
    o_kv = NSA_WIDTH
    o_gate = o_kv + 6 * NSA_KV_WIDTH
    o_qkv = o_gate + 3 * NSA_HEADS
    o_z = o_qkv + 3 * GDN_WIDTH
    o_a = o_z + GDN_WIDTH
    o_b = o_a + GDN_HEADS
    o_m = o_b + GDN_HEADS
    kvw = NSA_KV_WIDTH
    w1 = jnp.concatenate([w_in[:, o_q:o_kv], w_in[:, o_kv + 2 * kvw:o_kv + 6 * kvw], w_in[:, o_m:]], axis=1)
    small = jnp.concatenate([w_in[:, o_gate:o_qkv], w_in[:, o_a:o_m]], axis=1)
    used = (R2_SMALL * LANES) + small.shape[1]
    w2 = jnp.concatenate([w_in[:, o_z:o_a], w_in[:, o_kv:o_kv + 2 * kvw], w_in[:, o_qkv:o_z], small,
                          jnp.zeros((w_in.shape[0], R2_COLS - used), w_in.dtype)], axis=1)
    return jnp.concatenate([w1, w2], axis=1).astype(BF16)


def _r1_col_scale():
    c = np.ones((1, R1_COLS), np.float32)
    c[:, :NSA_WIDTH] = NSA_HEAD_DIM ** -0.5 * LOG2E
    return jnp.asarray(c)


def _layer(x, bias_c, bias_d, ln1_w, w_in, pe_k, pe_v, w1_k, w2_k, w1_v, w2_v, conv_w, a_log, dt_bias,
           gdn_norm_w, w_pa, w_pb, w_o, ln2_w, w_up, w_down, final_norm_w, b, s):
    t = b * s
    r1, r2 = input_projection(x, ln1_w, _split_w_in(w_in), _r1_col_scale())
    r1 = r1.reshape(b, s, R1_COLS)
    r2 = r2.reshape(b, s, R2_COLS)
    kcvc = nsa_compress(r2, jnp.stack([pe_k, pe_v]), jnp.stack([w1_k, w1_v]).astype(BF16),
                        w2_k.astype(BF16), w2_v.T.astype(BF16))
    o_a = nsa_attention(r1, r2, kcvc, bias_c, bias_d).reshape(t, NSA_WIDTH)
    o_b = gated_deltanet(r2, conv_w, a_log, dt_bias, gdn_norm_w).reshape(t, GDN_WIDTH)
    m = merge_branches(o_a, o_b, w_pa.astype(BF16), w_pb.astype(BF16), r1.reshape(t, R1_COLS))
    x, h2 = matmul_residual_norm(m, w_o.astype(BF16), x, ln2_w, BF16, tm=TILE_OUT_PROJ)
    u = matmul_rows(h2, w_up.astype(BF16), BF16, tm=TILE_UP[0], tn=TILE_UP[1], epilogue="relu2")
    x = matmul_rows(u, w_down.astype(BF16), F32, tm=TILE_DOWN[0], tn=TILE_DOWN[1], epilogue="residual", res=x)
    if final_norm_w is None:
        return x
    return rmsnorm(x, final_norm_w, F32)


def kernel(x, rel_table, ln1_w, w_in, cmp_pe_k, cmp_pe_v, cmp_w1_k, cmp_w2_k, cmp_w1_v, cmp_w2_v, conv_w, a_log,
           dt_bias, gdn_norm_w, w_pa, w_pb, w_o, ln2_w, w_up, w_down, ln_f_w):
    b, s, d = x.shape
    depth = w_in.shape[0]
    bias_c, bias_d = bias_tables(rel_table, s)
    xt = x.reshape(b * s, d)
    for l in range(depth):
        xt = _layer(xt, bias_c, bias_d, ln1_w[l], w_in[l], cmp_pe_k[l], cmp_pe_v[l], cmp_w1_k[l], cmp_w2_k[l],
                    cmp_w1_v[l], cmp_w2_v[l], conv_w[l], a_log[l], dt_bias[l], gdn_norm_w[l], w_pa[l], w_pb[l],
                    w_o[l], ln2_w[l], w_up[l], w_down[l], ln_f_w if l == depth - 1 else None, b, s)
    return xt.reshape(b, s, d)
```
